```python
import math, functools
import jax, jax.numpy as jnp
from jax import lax
import numpy as np

D_MODEL = 2048
BATCH = 2
SEQ = 4096
DEPTH = 1
DEC_BATCH = 128
DEC_SEQ = 8
PAST_LEN = 2048
PAGE_SIZE = 128

W_POOL = D_MODEL // 2
POOL_WINDOWS = (2, 4, 8, 16)
N_POOL_GROUPS = len(POOL_WINDOWS)
POOL_GROUP = W_POOL // N_POOL_GROUPS
POOL_HIST = max(POOL_WINDOWS) - 1
HEAD_DIM = 128
N_HEADS = (D_MODEL // 2) // HEAD_DIM
W_ATT = N_HEADS * HEAD_DIM
MOBA_BLOCK = 256
MOBA_TOPK = 3
Q_BLOCK = 64
SAMPLE_CHUNK = 16
D_FF = ((8 * D_MODEL // 3) + 255) // 256 * 256
CONV_W = 3
SPLIT_POINTS = (W_POOL, W_POOL + W_ATT, W_POOL + 2 * W_ATT, W_POOL + 3 * W_ATT, W_POOL + 3 * W_ATT + D_MODEL)
IN_COLS = W_POOL + 3 * W_ATT + 2 * D_MODEL
RMS_EPS = 1e-6
NEG_INF = -1e30

kernel_name = 'hybrid_pool_moba_convffn_step'


def rmsnorm(x, g):
    x32 = x.astype(jnp.float32)
    y = x32 * lax.rsqrt(jnp.mean(x32 * x32, axis=-1, keepdims=True) + RMS_EPS)
    return (y * g.astype(jnp.float32)).astype(x.dtype)


def alibi_slopes():
    return 2.0 ** (-8.0 * jnp.arange(1, N_HEADS + 1, dtype=jnp.float32) / N_HEADS)


def pool_mix(u_ext, pos0, w_maps, scale):
    b, n, _ = u_ext.shape
    t = n - POOL_HIST
    csum = jnp.cumsum(u_ext.astype(jnp.float32), axis=1)
    csum = jnp.concatenate([jnp.zeros((b, 1, W_POOL), jnp.float32), csum], axis=1)
    pos = pos0 + jnp.arange(t)
    u_cur = u_ext[:, POOL_HIST:].astype(jnp.float32)
    diffs = []
    for gi, w in enumerate(POOL_WINDOWS):
        cs = slice(gi * POOL_GROUP, (gi + 1) * POOL_GROUP)
        win_sum = (csum[:, POOL_HIST + 1:POOL_HIST + 1 + t, cs]
                   - csum[:, POOL_HIST + 1 - w:POOL_HIST + 1 - w + t, cs])
        cnt = jnp.minimum(w, pos + 1).astype(jnp.float32)[None, :, None]
        diffs.append(win_sum / cnt - u_cur[:, :, cs])
    d = jnp.stack(diffs, axis=2).astype(u_ext.dtype)
    m = jnp.einsum('btgc,gcd->btgd', d, w_maps).reshape(b, t, W_POOL)
    return m * scale


def moba_prepare(k, v):
    b, L, h, dh = k.shape
    nb = max(-(-L // MOBA_BLOCK), MOBA_TOPK)
    pad_cfg = ((0, 0), (0, nb * MOBA_BLOCK - L), (0, 0), (0, 0))
    kb = jnp.pad(k, pad_cfg).reshape(b, nb, MOBA_BLOCK, h, dh).transpose(0, 3, 1, 2, 4)
    vb = jnp.pad(v, pad_cfg).reshape(b, nb, MOBA_BLOCK, h, dh).transpose(0, 3, 1, 2, 4)
    kmean = jnp.mean(kb.astype(jnp.float32), axis=3)
    return kb, vb, kmean


def moba_attend(q, kb, vb, kmean, q_pos):
    b, nq, h, dh = q.shape
    nb = kb.shape[2]
    qh = q.transpose(0, 2, 1, 3)
    own = (q_pos // MOBA_BLOCK).astype(jnp.int32)
    route = jnp.einsum('bhqd,bhnd->bhqn', qh.astype(jnp.float32), kmean)
    fully_past = jnp.arange(nb)[None, :] < own[:, None]
    route = jnp.where(fully_past, route, -jnp.inf)
    _, top_idx = lax.top_k(route, MOBA_TOPK)
    own_b = jnp.broadcast_to(own[None, None, :, None], (b, h, nq, 1)).astype(top_idx.dtype)
    sel = jnp.concatenate([top_idx, own_b], axis=-1)
    sel_ok = jnp.concatenate([jnp.arange(MOBA_TOPK)[None, :] < own[:, None],
                              jnp.ones((nq, 1), dtype=bool)], axis=-1)
    bi = jnp.arange(b)[:, None, None, None]
    hi = jnp.arange(h)[None, :, None, None]
    kg = kb[bi, hi, sel]
    vg = vb[bi, hi, sel]
    kpos = sel[..., None] * MOBA_BLOCK + jnp.arange(MOBA_BLOCK)
    dist = (q_pos[None, None, :, None, None] - kpos).astype(jnp.float32)
    s = jnp.einsum('bhqd,bhqnkd->bhqnk', qh, kg, preferred_element_type=jnp.float32)
    s = s * (HEAD_DIM ** -0.5) - alibi_slopes()[None, :, None, None, None] * dist
    mask = sel_ok[None, None, :, :, None] & (dist >= 0)
    s = jnp.where(mask, s, NEG_INF)
    p = jax.nn.softmax(s.reshape(b, h, nq, -1), axis=-1).reshape(s.shape)
    o = jnp.einsum('bhqnk,bhqnkd->bhqd', p.astype(vg.dtype), vg)
    return o.transpose(0, 2, 1, 3)


def moba_prompt(q, k, v):
    b, t, h, dh = q.shape
    kb, vb, km = moba_prepare(k, v)
    nqb = t // Q_BLOCK
    qb = q.reshape(b, nqb, Q_BLOCK, h, dh).transpose(1, 0, 2, 3, 4)
    pb = jnp.arange(t, dtype=jnp.int32).reshape(nqb, Q_BLOCK)
    o = lax.map(lambda xs: moba_attend(xs[0], kb, vb, km, xs[1]), (qb, pb))
    return o.transpose(1, 0, 2, 3, 4).reshape(b, t, h, dh)


def moba_sample(cache_k, cache_v, page_table, q, k, v):
    b, t, h, dh = q.shape
    n_pages = page_table.shape[1]
    past = n_pages * cache_k.shape[1]
    c = math.gcd(b, SAMPLE_CHUNK)
    nc = b // c
    q_pos = past + jnp.arange(t, dtype=jnp.int32)

    def chunk(xs):
        pt, qc, kc, vc = xs
        kp = cache_k[pt].reshape(c, past, h, dh).astype(kc.dtype)
        vp = cache_v[pt].reshape(c, past, h, dh).astype(vc.dtype)
        kb, vb, km = moba_prepare(jnp.concatenate([kp, kc], axis=1), jnp.concatenate([vp, vc], axis=1))
        return moba_attend(qc, kb, vb, km, q_pos)

    split = lambda a: a.reshape((nc, c) + a.shape[1:])
    o = lax.map(chunk, (split(page_table), split(q), split(k), split(v)))
    return o.reshape(b, t, h, dh)


def hybrid_layer(x, pos0, pool_hist, conv_hist, attend, g1, w_in, pool_maps, pool_scale,
                 w_pool_up, w_att_up, w_out, g2, w_up, conv_w, conv_b, w_down):
    b, t, _ = x.shape
    xn = rmsnorm(x, g1)
    u, q, k, v, ga, gb = jnp.split(xn @ w_in, SPLIT_POINTS, axis=-1)
    heads = lambda a: a.reshape(b, t, N_HEADS, HEAD_DIM)
    q, k, v = heads(q), heads(k), heads(v)
    u_ext = jnp.concatenate([pool_hist.astype(u.dtype), u], axis=1)
    a_out = pool_mix(u_ext, pos0, pool_maps, pool_scale)
    b_out = attend(q, k, v).reshape(b, t, W_ATT)
    merged = jax.nn.sigmoid(ga) * (a_out @ w_pool_up) + jax.nn.sigmoid(gb) * (b_out @ w_att_up)
    h = x + merged @ w_out
    hn = rmsnorm(h, g2)
    a, gv = jnp.split(hn @ w_up, 2, axis=-1)
    ext = jnp.concatenate([conv_hist.astype(a.dtype), a], axis=1)
    ac = conv_b + conv_w[0] * ext[:, 0:t]
    for i in range(1, CONV_W):
        ac = ac + conv_w[i] * ext[:, i:i + t]
    y = h + (jax.nn.gelu(ac, approximate=False) * gv) @ w_down
    return y, k, v, u_ext[:, -POOL_HIST:], ext[:, -(CONV_W - 1):]


def setup_inputs(seed: int = 0) -> dict:
    key = jax.random.key(seed)
    ks = jax.random.split(key, 24)
    normal = lambda k, shape, scale: jax.random.normal(k, shape, jnp.float32) * scale
    n_pages = PAST_LEN // PAGE_SIZE
    n_used = DEC_BATCH * n_pages
    n_pool = -(-(n_used * 5) // 4)
    page_table = jax.random.permutation(ks[0], n_pool)[:n_used].reshape(DEC_BATCH, n_pages).astype(jnp.int32)
    return {
        'x_prompt': normal(ks[1], (BATCH, SEQ, D_MODEL), 1.0),
        'x_sample': normal(ks[2], (DEC_BATCH, DEC_SEQ, D_MODEL), 1.0),
        'cache_k': normal(ks[3], (DEPTH, n_pool, PAGE_SIZE, N_HEADS, HEAD_DIM), 1.0),
        'cache_v': normal(ks[4], (DEPTH, n_pool, PAGE_SIZE, N_HEADS, HEAD_DIM), 1.0),
        'state_pool': normal(ks[5], (DEPTH, DEC_BATCH, POOL_HIST, W_POOL), 1.0),
        'state_conv': normal(ks[6], (DEPTH, DEC_BATCH, CONV_W - 1, D_FF), 1.0),
        'page_table': page_table,
        'norm1_g': 1.0 + normal(ks[7], (DEPTH, D_MODEL), 0.02),
        'w_in': normal(ks[8], (DEPTH, D_MODEL, IN_COLS), D_MODEL ** -0.5),
        'pool_maps': normal(ks[9], (DEPTH, N_POOL_GROUPS, POOL_GROUP, POOL_GROUP), POOL_GROUP ** -0.5),
        'pool_scale': 1.0 + normal(ks[10], (DEPTH, W_POOL), 0.02),
        'w_pool_up': normal(ks[11], (DEPTH, W_POOL, D_MODEL), W_POOL ** -0.5),
        'w_att_up': normal(ks[12], (DEPTH, W_ATT, D_MODEL), W_ATT ** -0.5),
        'w_out': normal(ks[13], (DEPTH, D_MODEL, D_MODEL), D_MODEL ** -0.5),
        'norm2_g': 1.0 + normal(ks[14], (DEPTH, D_MODEL), 0.02),
        'w_up': normal(ks[15], (DEPTH, D_MODEL, 2 * D_FF), D_MODEL ** -0.5),
        'conv_w': normal(ks[16], (DEPTH, CONV_W, D_FF), CONV_W ** -0.5),
        'conv_b': normal(ks[17], (DEPTH, D_FF), 0.01),
        'w_down': normal(ks[18], (DEPTH, D_FF, D_MODEL), D_FF ** -0.5),
        'final_norm_g': 1.0 + normal(ks[19], (D_MODEL,), 0.02),
    }


def reference(x_prompt, x_sample, cache_k, cache_v, state_pool, state_conv, page_table,
              norm1_g, w_in, pool_maps, pool_scale, w_pool_up, w_att_up, w_out,
              norm2_g, w_up, conv_w, conv_b, w_down, final_norm_g):
    hp, hs = x_prompt, x_sample
    bp = x_prompt.shape[0]
    past_len = page_table.shape[1] * cache_k.shape[2]
    kps, vps, kss, vss, pps, pss, cps, css = [], [], [], [], [], [], [], []
    for l in range(DEPTH):
        w = (norm1_g[l], w_in[l], pool_maps[l], pool_scale[l], w_pool_up[l], w_att_up[l], w_out[l],
             norm2_g[l], w_up[l], conv_w[l], conv_b[l], w_down[l])
        pool0 = jnp.zeros((bp, POOL_HIST, W_POOL), hp.dtype)
        conv0 = jnp.zeros((bp, CONV_W - 1, D_FF), hp.dtype)
        hp, kp, vp, pp, cp = hybrid_layer(hp, 0, pool0, conv0, moba_prompt, *w)
        attend_s = functools.partial(moba_sample, cache_k[l], cache_v[l], page_table)
        hs, k_s, v_s, p_s, c_s = hybrid_layer(hs, past_len, state_pool[l], state_conv[l], attend_s, *w)
        kps.append(kp); vps.append(vp); pps.append(pp); cps.append(cp)
        kss.append(k_s); vss.append(v_s); pss.append(p_s); css.append(c_s)
    y_prompt = rmsnorm(hp, final_norm_g)
    y_sample = rmsnorm(hs, final_norm_g)
    return (y_prompt, y_sample, jnp.stack(kps), jnp.stack(vps), jnp.stack(kss), jnp.stack(vss),
            jnp.stack(pps), jnp.stack(pss), jnp.stack(cps), jnp.stack(css))
```

```python
import functools

import jax
import jax.numpy as jnp
from jax import lax
from jax.experimental import pallas as pl
from jax.experimental.pallas import tpu as pltpu

POOL_WINDOWS = (2, 4, 8, 16)
POOL_HIST = max(POOL_WINDOWS) - 1
HEAD_DIM = 128
MOBA_BLOCK = 256
MOBA_TOPK = 3
CONV_W = 3
RMS_EPS = 1e-6
NEG_INF = -1e30

SUBLANES = 8
HALO_ROWS = 16
V7X_VMEM_BYTES = 64 * 1024 * 1024
VMEM_LIMIT_BYTES = 56 * 1024 * 1024

BF16 = jnp.bfloat16
F32 = jnp.float32


def _params(*semantics):
    return pltpu.CompilerParams(dimension_semantics=semantics, vmem_limit_bytes=VMEM_LIMIT_BYTES)


def _rmsnorm(x, g):
    return x * lax.rsqrt(jnp.mean(x * x, axis=-1, keepdims=True) + RMS_EPS) * g


def _dot(a, b):
    return jnp.dot(a, b, preferred_element_type=F32)


def _dot_nt(a, b):
    return lax.dot_general(a, b, (((1,), (1,)), ((), ())), preferred_element_type=F32)


def _split_bf16(x):
    hi = x.astype(BF16)
    lo = (x - hi.astype(F32)).astype(BF16)
    return hi, lo


def _dot_nt_3pass(a, b):
    a_hi, a_lo = _split_bf16(a)
    b_hi, b_lo = _split_bf16(b)
    return _dot_nt(a_hi, b_hi) + (_dot_nt(a_hi, b_lo) + _dot_nt(a_lo, b_hi))


def _norm_matmul_kernel(x_ref, g_ref, w_ref, o_ref, xn_ref):
    @pl.when(pl.program_id(1) == 0)
    def _():
        xn_ref[...] = _rmsnorm(x_ref[...], g_ref[...]).astype(BF16)

    o_ref[...] = _dot(xn_ref[...], w_ref[...])


def _norm_matmul(x, g, w, *, tm, tn):
    rows, d = x.shape
    n = w.shape[1]
    return pl.pallas_call(
        _norm_matmul_kernel,
        grid=(rows // tm, n // tn),
        in_specs=[
            pl.BlockSpec((tm, d), lambda i, j: (i, 0)),
            pl.BlockSpec((1, d), lambda i, j: (0, 0)),
            pl.BlockSpec((d, tn), lambda i, j: (0, j)),
        ],
        out_specs=pl.BlockSpec((tm, tn), lambda i, j: (i, j)),
        out_shape=jax.ShapeDtypeStruct((rows, n), F32),
        scratch_shapes=[pltpu.VMEM((tm, d), BF16)],
        compiler_params=_params("parallel", "arbitrary"),
        name="in_proj",
    )(x, g, w)


def _pool_kernel(u_ref, hist_ref, maps_ref, scale_ref, o_ref, ext_ref, *, pos0, zero_first_hist):
    s_blk, l_blk, width = u_ref.shape
    group = width // len(POOL_WINDOWS)
    tile = pl.program_id(1)
    u = u_ref[...]
    ext_ref[:, HALO_ROWS:, :] = u
    hist_rows = hist_ref.shape[1]
    ext_ref[:, HALO_ROWS - hist_rows:HALO_ROWS, :] = hist_ref[...]
    if zero_first_hist:
        @pl.when(tile == 0)
        def _():
            ext_ref[:, 0:HALO_ROWS, :] = jnp.zeros((s_blk, HALO_ROWS, width), F32)

    pos = pos0 + tile * l_blk + lax.broadcasted_iota(jnp.int32, (s_blk, l_blk, group), 1)
    outs = []
    for gi, w in enumerate(POOL_WINDOWS):
        cols = slice(gi * group, (gi + 1) * group)
        win = u[:, :, cols]
        for k in range(1, w):
            win = win + ext_ref[:, HALO_ROWS - k:HALO_ROWS - k + l_blk, cols]
        cnt = jnp.minimum(w, pos + 1).astype(F32)
        d = win / cnt - u[:, :, cols]
        d2 = d.reshape(s_blk * l_blk, group).astype(BF16)
        outs.append(_dot(d2, maps_ref[gi]))
    m = jnp.concatenate(outs, axis=-1) * scale_ref[...]
    o_ref[...] = m.astype(o_ref.dtype)


def _pool_mix(proj3, hist, maps, scale, *, s_blk, l_blk, pos0, out_dtype):
    n_seq, seq_len, _ = proj3.shape
    n_groups, group, _ = maps.shape
    width = n_groups * group
    tiles = seq_len // l_blk
    if hist is None:
        halo_blocks = l_blk // HALO_ROWS
        hist_arr = proj3
        hist_spec = pl.BlockSpec((s_blk, HALO_ROWS, width),
                                 lambda s, t: (s, jnp.maximum(t * halo_blocks - 1, 0), 0))
    else:
        assert tiles == 1
        hist_arr = hist
        hist_spec = pl.BlockSpec((s_blk, hist.shape[1], width), lambda s, t: (s, 0, 0))
    kern = functools.partial(_pool_kernel, pos0=pos0, zero_first_hist=hist is None)
    return pl.pallas_call(
        kern,
        grid=(n_seq // s_blk, tiles),
        in_specs=[
            pl.BlockSpec((s_blk, l_blk, width), lambda s, t: (s, t, 0)),
            hist_spec,
            pl.BlockSpec((n_groups, group, group), lambda s, t: (0, 0, 0)),
            pl.BlockSpec((1, width), lambda s, t: (0, 0)),
        ],
        out_specs=pl.BlockSpec((s_blk * l_blk, width), lambda s, t: (s * tiles + t, 0)),
        out_shape=jax.ShapeDtypeStruct((n_seq * seq_len, width), out_dtype),
        scratch_shapes=[pltpu.VMEM((s_blk, HALO_ROWS + l_blk, width), F32)],
        compiler_params=_params("parallel", "arbitrary"),
        name="pool_mix",
    )(proj3, hist_arr, maps, scale)


def _topk_select(route, n_valid):
    rows, nb = route.shape
    blk = lax.broadcasted_iota(jnp.int32, (rows, nb), 1)
    rank = jnp.zeros((rows, nb), jnp.int32)
    for other in range(nb):
        c = route[:, other:other + 1]
        beats = jnp.where((c > route) | ((c == route) & (other < blk)), 1, 0)
        if isinstance(n_valid, int):
            if other < n_valid:
                rank = rank + beats
        else:
            rank = rank + beats * (other < n_valid).astype(jnp.int32)
    return (blk < n_valid) & (rank < MOBA_TOPK)


def _prompt_attn_kernel(slopes_ref, q_ref, k_ref, v_ref, o_ref, kb_ref, vb_ref, kmean_ref, m_ref, l_ref, acc_ref):
    head = pl.program_id(1)
    qt = pl.program_id(2)
    seq_len, dh = k_ref.shape[1], k_ref.shape[2]
    tq = q_ref.shape[1]
    nb = seq_len // MOBA_BLOCK
    slope = slopes_ref[head]

    @pl.when(qt == 0)
    def _():
        k = k_ref[0]
        kb_ref[...] = k.astype(BF16)
        vb_ref[...] = v_ref[0].astype(BF16)
        kmean_ref[...] = jnp.mean(k.reshape(nb, MOBA_BLOCK, dh), axis=1)

    q = q_ref[0]
    qb = q.astype(BF16)
    route = _dot_nt_3pass(q, kmean_ref[...])
    sel = _topk_select(route, qt)
    blk = lax.broadcasted_iota(jnp.int32, (tq, nb), 1)
    sel_bits = jnp.sum(jnp.where(sel, jnp.left_shift(1, blk).astype(F32), 0.0), axis=-1, keepdims=True)
    sel_bits = jnp.broadcast_to(sel_bits.astype(jnp.int32), (tq, MOBA_BLOCK))

    m_ref[...] = jnp.full((tq, 1), NEG_INF, F32)
    l_ref[...] = jnp.zeros((tq, 1), F32)
    acc_ref[...] = jnp.zeros((tq, dh), F32)
    row_minus_col = (lax.broadcasted_iota(jnp.int32, (tq, MOBA_BLOCK), 0)
                     - lax.broadcasted_iota(jnp.int32, (tq, MOBA_BLOCK), 1))
    scale = dh ** -0.5

    def update(j, own):
        start = pl.multiple_of(j * MOBA_BLOCK, MOBA_BLOCK)
        kj = kb_ref[pl.ds(start, MOBA_BLOCK), :]
        vj = vb_ref[pl.ds(start, MOBA_BLOCK), :]
        dist = row_minus_col + (qt - j) * MOBA_BLOCK
        s = _dot_nt(qb, kj) * scale - slope * dist.astype(F32)
        keep = (dist >= 0) if own else ((jnp.right_shift(sel_bits, j) & 1) == 1)
        s = jnp.where(keep, s, NEG_INF)
        m_old = m_ref[...]
        m_new = jnp.maximum(m_old, jnp.max(s, axis=-1, keepdims=True))
        alpha = jnp.exp(m_old - m_new)
        p = jnp.where(keep, jnp.exp(s - m_new), 0.0)
        l_ref[...] = alpha * l_ref[...] + jnp.sum(p, axis=-1, keepdims=True)
        acc_ref[...] = alpha * acc_ref[...] + _dot(p.astype(BF16), vj)
        m_ref[...] = m_new

    def past_block(j, carry):
        update(j, False)
        return carry

    lax.fori_loop(0, qt, past_block, 0)
    update(qt, True)
    o_ref[...] = (acc_ref[...] / l_ref[...]).astype(o_ref.dtype)


def _prompt_attention(proj3, slopes, *, n_heads, q_col, k_col, v_col):
    bsz, seq_len, _ = proj3.shape
    tq = MOBA_BLOCK
    nq = seq_len // tq
    nb = seq_len // MOBA_BLOCK
    qb0, kb0, vb0 = q_col // HEAD_DIM, k_col // HEAD_DIM, v_col // HEAD_DIM
    return pl.pallas_call(
        _prompt_attn_kernel,
        grid=(bsz, n_heads, nq),
        in_specs=[
            pl.BlockSpec(memory_space=pltpu.SMEM),
            pl.BlockSpec((1, tq, HEAD_DIM), lambda b, h, i: (b, i, qb0 + h)),
            pl.BlockSpec((1, seq_len, HEAD_DIM), lambda b, h, i: (b, 0, kb0 + h)),
            pl.BlockSpec((1, seq_len, HEAD_DIM), lambda b, h, i: (b, 0, vb0 + h)),
        ],
        out_specs=pl.BlockSpec((tq, HEAD_DIM), lambda b, h, i: (b * nq + i, h)),
        out_shape=jax.ShapeDtypeStruct((bsz * seq_len, n_heads * HEAD_DIM), BF16),
        scratch_shapes=[
            pltpu.VMEM((seq_len, HEAD_DIM), BF16),
            pltpu.VMEM((seq_len, HEAD_DIM), BF16),
            pltpu.VMEM((nb, HEAD_DIM), F32),
            pltpu.VMEM((tq, 1), F32),
            pltpu.VMEM((tq, 1), F32),
            pltpu.VMEM((tq, HEAD_DIM), F32),
        ],
        compiler_params=_params("parallel", "parallel", "arbitrary"),
        name="prompt_attn",
    )(slopes, proj3, proj3, proj3)


def _block_diag_rows(x, n_heads, dh):
    t = x.shape[0]
    tiled = jnp.concatenate([x] * n_heads, axis=0)
    row_head = lax.broadcasted_iota(jnp.int32, tiled.shape, 0) // t
    col_head = lax.broadcasted_iota(jnp.int32, tiled.shape, 1) // dh
    return jnp.where(row_head == col_head, tiled, 0.0)


def _diag_blocks(x, n_heads, t, dh):
    return jnp.concatenate([x[h * t:(h + 1) * t, h * dh:(h + 1) * dh] for h in range(n_heads)], axis=0)


def _sample_attn_kernel(pt_ref, slopes_ref, q_ref, kn_ref, vn_ref, k0_ref, k1_ref, v0_ref, v1_ref, o_ref,
                        wq_ref, route_ref, m_ref, l_ref, acc_ref, *, past_len):
    del pt_ref
    j = pl.program_id(1)
    n_blocks = pl.num_programs(1)
    t, width = q_ref.shape[1], q_ref.shape[2]
    dh = HEAD_DIM
    n_heads = width // dh
    rows = n_heads * t
    scale = dh ** -0.5

    @pl.when(j == 0)
    def _():
        wq_ref[...] = _block_diag_rows(q_ref[0], n_heads, dh)

    wq = wq_ref[...]
    wqb = wq.astype(BF16)
    row_head = lax.broadcasted_iota(jnp.int32, (rows, 1), 0) // t
    slope_rows = jnp.zeros((rows, 1), F32)
    for h in range(n_heads):
        slope_rows = jnp.where(row_head == h, slopes_ref[h], slope_rows)
    q_pos = past_len + lax.broadcasted_iota(jnp.int32, (rows, 1), 0) % t

    kblk = jnp.concatenate([k0_ref[0], k1_ref[0]], axis=0)
    vblk = jnp.concatenate([v0_ref[0], v1_ref[0]], axis=0)
    kmean = jnp.mean(kblk, axis=0, keepdims=True)
    route_ref[j] = jnp.sum(wq * kmean, axis=-1, keepdims=True)
    k_pos = j * MOBA_BLOCK + lax.broadcasted_iota(jnp.int32, (rows, MOBA_BLOCK), 1)
    s = _dot_nt(wqb, kblk.astype(BF16)) * scale - slope_rows * (q_pos - k_pos).astype(F32)
    m_j = jnp.max(s, axis=-1, keepdims=True)
    p = jnp.exp(s - m_j)
    m_ref[j] = m_j
    l_ref[j] = jnp.sum(p, axis=-1, keepdims=True)
    acc_ref[j] = _diag_blocks(_dot(p.astype(BF16), vblk.astype(BF16)), n_heads, t, dh)

    @pl.when(j == n_blocks - 1)
    def _():
        nb = route_ref.shape[0]
        route = jnp.concatenate([route_ref[b] for b in range(nb)], axis=-1)
        sel = _topk_select(route, nb)
        knb = kn_ref[0].astype(BF16)
        vnb = vn_ref[0].astype(BF16)
        dist = (lax.broadcasted_iota(jnp.int32, (rows, t), 0) % t
                - lax.broadcasted_iota(jnp.int32, (rows, t), 1))
        s_own = _dot_nt(wqb, knb) * scale - slope_rows * dist.astype(F32)
        keep = dist >= 0
        s_own = jnp.where(keep, s_own, NEG_INF)
        m_all = jnp.max(s_own, axis=-1, keepdims=True)
        for b in range(nb):
            m_all = jnp.maximum(m_all, jnp.where(sel[:, b:b + 1], m_ref[b], NEG_INF))
        p_own = jnp.where(keep, jnp.exp(s_own - m_all), 0.0)
        den = jnp.sum(p_own, axis=-1, keepdims=True)
        num = _diag_blocks(_dot(p_own.astype(BF16), vnb), n_heads, t, dh)
        for b in range(nb):
            w_b = jnp.where(sel[:, b:b + 1], jnp.exp(m_ref[b] - m_all), 0.0)
            den = den + w_b * l_ref[b]
            num = num + w_b * acc_ref[b]
        out = num / den
        o_ref[...] = jnp.concatenate([out[h * t:(h + 1) * t, :] for h in range(n_heads)], axis=-1)


def _sample_attention(proj3, cache_k, cache_v, page_table, slopes, *, q_col, k_col, v_col):
    n_seq, t, _ = proj3.shape
    _, page, width = cache_k.shape
    n_pages = page_table.shape[1]
    assert MOBA_BLOCK == 2 * page and n_pages % 2 == 0 and t <= MOBA_BLOCK
    past_len = n_pages * page
    nb = past_len // MOBA_BLOCK
    n_heads = width // HEAD_DIM
    rows = n_heads * t
    qb0, kb0, vb0 = q_col // width, k_col // width, v_col // width
    page_spec = lambda half: pl.BlockSpec((1, page, width), lambda s, j, pt: (pt[s, 2 * j + half], 0, 0))
    grid_spec = pltpu.PrefetchScalarGridSpec(
        num_scalar_prefetch=1,
        grid=(n_seq, nb),
        in_specs=[
            pl.BlockSpec(memory_space=pltpu.SMEM),
            pl.BlockSpec((1, t, width), lambda s, j, pt: (s, 0, qb0)),
            pl.BlockSpec((1, t, width), lambda s, j, pt: (s, 0, kb0)),
            pl.BlockSpec((1, t, width), lambda s, j, pt: (s, 0, vb0)),
            page_spec(0), page_spec(1), page_spec(0), page_spec(1),
        ],
        out_specs=pl.BlockSpec((t, width), lambda s, j, pt: (s, 0)),
        scratch_shapes=[
            pltpu.VMEM((rows, width), F32),
            pltpu.VMEM((nb, rows, 1), F32),
            pltpu.VMEM((nb, rows, 1), F32),
            pltpu.VMEM((nb, rows, 1), F32),
            pltpu.VMEM((nb, rows, HEAD_DIM), F32),
        ],
    )
    return pl.pallas_call(
        functools.partial(_sample_attn_kernel, past_len=past_len),
        grid_spec=grid_spec,
        out_shape=jax.ShapeDtypeStruct((n_seq * t, width), F32),
        compiler_params=_params("parallel", "arbitrary"),
        name="sample_attn",
    )(page_table, slopes, proj3, proj3, proj3, cache_k, cache_k, cache_v, cache_v)


def _merge_kernel(a_ref, b_ref, wp_ref, wa_ref, ga_ref, gb_ref, o_ref):
    pa = _dot(a_ref[...].astype(BF16), wp_ref[...])
    pb = _dot(b_ref[...].astype(BF16), wa_ref[...])
    merged = jax.nn.sigmoid(ga_ref[...]) * pa + jax.nn.sigmoid(gb_ref[...]) * pb
    o_ref[...] = merged.astype(o_ref.dtype)


def _merge(a_out, b_out, w_pool_up, w_att_up, proj, *, ga_col, gb_col, tm, tn):
    rows, wa = a_out.shape
    wb = b_out.shape[1]
    n = w_pool_up.shape[1]
    ga0, gb0 = ga_col // tn, gb_col // tn
    return pl.pallas_call(
        _merge_kernel,
        grid=(rows // tm, n // tn),
        in_specs=[
            pl.BlockSpec((tm, wa), lambda i, j: (i, 0)),
            pl.BlockSpec((tm, wb), lambda i, j: (i, 0)),
            pl.BlockSpec((wa, tn), lambda i, j: (0, j)),
            pl.BlockSpec((wb, tn), lambda i, j: (0, j)),
            pl.BlockSpec((tm, tn), lambda i, j: (i, ga0 + j)),
            pl.BlockSpec((tm, tn), lambda i, j: (i, gb0 + j)),
        ],
        out_specs=pl.BlockSpec((tm, tn), lambda i, j: (i, j)),
        out_shape=jax.ShapeDtypeStruct((rows, n), BF16),
        compiler_params=_params("parallel", "arbitrary"),
        name="merge",
    )(a_out, b_out, w_pool_up, w_att_up, proj, proj)


def _matmul_residual_kernel(a_ref, w_ref, r_ref, o_ref):
    o_ref[...] = r_ref[...] + _dot(a_ref[...], w_ref[...])


def _matmul_residual(a, w, res, *, tm, tn):
    rows, k = a.shape
    n = w.shape[1]
    return pl.pallas_call(
        _matmul_residual_kernel,
        grid=(rows // tm, n // tn),
        in_specs=[
            pl.BlockSpec((tm, k), lambda i, j: (i, 0)),
            pl.BlockSpec((k, tn), lambda i, j: (0, j)),
            pl.BlockSpec((tm, tn), lambda i, j: (i, j)),
        ],
        out_specs=pl.BlockSpec((tm, tn), lambda i, j: (i, j)),
        out_shape=jax.ShapeDtypeStruct((rows, n), F32),
        compiler_params=_params("parallel", "arbitrary"),
        name="out_proj",
    )(a, w, res)


def _gelu_exact(x):
    return 0.5 * x * (1.0 + lax.erf(x * (2.0 ** -0.5)))


def _ffn_kernel(h_ref, g2_ref, wa_ref, wg_ref, cw_ref, cb_ref, wd_ref, gf_ref, *rest,
                seq_rows, tiles_per_seq, per_seq_hist):
    hist_ref = rest[0] if per_seq_hist else None
    y_ref, tail_ref, hn_ref, acc_ref, carry_ref = rest[-5:]
    i = pl.program_id(0)
    j = pl.program_id(1)
    tm = h_ref.shape[0]

    @pl.when(j == 0)
    def _():
        hn_ref[...] = _rmsnorm(h_ref[...], g2_ref[...]).astype(BF16)
        acc_ref[...] = jnp.zeros(acc_ref.shape, F32)

    hn = hn_ref[...]
    a = _dot(hn, wa_ref[...])
    gv = _dot(hn, wg_ref[...])
    tn = a.shape[1]
    row = lax.broadcasted_iota(jnp.int32, (tm, tn), 0)
    if hist_ref is not None:
        n_seq = tm // seq_rows
        hist = hist_ref[...]
        prev = [jnp.broadcast_to(hist[:, r:r + 1, :], (n_seq, seq_rows, tn)).reshape(tm, tn)
                for r in range(CONV_W - 1)]
        t_in_seq = row % seq_rows
        tail_ref[...] = a
    else:
        @pl.when(i % tiles_per_seq == 0)
        def _():
            carry_ref[j] = jnp.zeros((SUBLANES, tn), F32)
        carry = carry_ref[j]
        prev = [jnp.broadcast_to(carry[SUBLANES - (CONV_W - 1) + r:SUBLANES - (CONV_W - 1) + r + 1, :], (tm, tn))
                for r in range(CONV_W - 1)]
        t_in_seq = row
        tail = a[tm - SUBLANES:, :]
        carry_ref[j] = tail
        tail_ref[0] = tail
    ac = cb_ref[...] + cw_ref[CONV_W - 1:CONV_W, :] * a
    for back in range(1, CONV_W):
        shifted = pltpu.roll(a, back, axis=0)
        for r in range(back):
            shifted = jnp.where(t_in_seq == r, prev[CONV_W - 1 - back + r], shifted)
        ac = ac + cw_ref[CONV_W - 1 - back:CONV_W - back, :] * shifted
    act = (_gelu_exact(ac) * gv).astype(BF16)
    acc_ref[...] += _dot(act, wd_ref[...])

    @pl.when(j == pl.num_programs(1) - 1)
    def _():
        y = h_ref[...] + acc_ref[...]
        y_ref[...] = _rmsnorm(y, gf_ref[...])


def _ffn(h, g2, w_up, conv_w, conv_b, w_down, gf, hist, *, seq_rows, tm, tn):
    rows, d = h.shape
    d_ff = w_down.shape[0]
    n_col = d_ff // tn
    n_row = rows // tm
    per_seq_hist = hist is not None
    if per_seq_hist:
        assert tm % seq_rows == 0
        tiles_per_seq = 1
        tail_shape = jax.ShapeDtypeStruct((rows, d_ff), F32)
        tail_spec = pl.BlockSpec((tm, tn), lambda i, j: (i, j))
        hist_args = (hist,)
        hist_specs = [pl.BlockSpec((tm // seq_rows, CONV_W - 1, tn), lambda i, j: (i, 0, j))]
    else:
        assert seq_rows % tm == 0
        tiles_per_seq = seq_rows // tm
        tail_shape = jax.ShapeDtypeStruct((n_row, SUBLANES, d_ff), F32)
        tail_spec = pl.BlockSpec((1, SUBLANES, tn), lambda i, j: (i, 0, j))
        hist_args = ()
        hist_specs = []
    y, tail = pl.pallas_call(
        functools.partial(_ffn_kernel, seq_rows=seq_rows, tiles_per_seq=tiles_per_seq, per_seq_hist=per_seq_hist),
        grid=(n_row, n_col),
        in_specs=[
            pl.BlockSpec((tm, d), lambda i, j: (i, 0)),
            pl.BlockSpec((1, d), lambda i, j: (0, 0)),
            pl.BlockSpec((d, tn), lambda i, j: (0, j)),
            pl.BlockSpec((d, tn), lambda i, j: (0, n_col + j)),
            pl.BlockSpec((CONV_W, tn), lambda i, j: (0, j)),
            pl.BlockSpec((1, tn), lambda i, j: (0, j)),
            pl.BlockSpec((tn, d), lambda i, j: (j, 0)),
            pl.BlockSpec((1, d), lambda i, j: (0, 0)),
            *hist_specs,
        ],
        out_specs=[pl.BlockSpec((tm, d), lambda i, j: (i, 0)), tail_spec],
        out_shape=[jax.ShapeDtypeStruct((rows, d), F32), tail_shape],
        scratch_shapes=[
            pltpu.VMEM((tm, d), BF16),
            pltpu.VMEM((tm, d), F32),
            pltpu.VMEM((n_col, SUBLANES, tn), F32),
        ],
        compiler_params=_params("arbitrary", "arbitrary"),
        name="ffn",
    )(h, g2, w_up, w_up, conv_w, conv_b, w_down, gf, *hist_args)
    return y, tail


def _layer(x3, pool_hist, conv_hist, attend, pos0, w, *, tm):
    n_seq, seq_len, d = x3.shape
    rows = n_seq * seq_len
    x = x3.reshape(rows, d)
    w_pool = w["pool_scale"].shape[1]
    w_att = w["w_att_up"].shape[0]
    proj = _norm_matmul(x, w["g1"], w["w_in"], tm=tm, tn=1024)
    proj3 = proj.reshape(n_seq, seq_len, -1)
    q_col, k_col, v_col = w_pool, w_pool + w_att, w_pool + 2 * w_att
    ga_col, gb_col = w_pool + 3 * w_att, w_pool + 3 * w_att + d
    if pool_hist is None:
        a_out = _pool_mix(proj3, None, w["pool_maps"], w["pool_scale"], s_blk=1, l_blk=tm, pos0=pos0,
                          out_dtype=BF16)
    else:
        a_out = _pool_mix(proj3, pool_hist, w["pool_maps"], w["pool_scale"], s_blk=32, l_blk=seq_len, pos0=pos0,
                          out_dtype=F32)
    b_out = attend(proj3, q_col, k_col, v_col)
    merged = _merge(a_out, b_out, w["w_pool_up"], w["w_att_up"], proj, ga_col=ga_col, gb_col=gb_col, tm=tm, tn=1024)
    h = _matmul_residual(merged, w["w_out"], x, tm=tm, tn=1024)
    y, tail = _ffn(h, w["g2"], w["w_up"], w["conv_w"], w["conv_b"], w["w_down"], w["gf"], conv_hist,
                   seq_rows=seq_len, tm=tm, tn=512)
    return y.reshape(n_seq, seq_len, d), proj3, tail


def kernel(x_prompt, x_sample, cache_k, cache_v, state_pool, state_conv, page_table, norm1_g, w_in, pool_maps,
           pool_scale, w_pool_up, w_att_up, w_out, norm2_g, w_up, conv_w, conv_b, w_down, final_norm_g):
    depth = w_in.shape[0]
    assert depth == 1, "single-layer trunk"
    bsz, seq, d = x_prompt.shape
    n_dec, dec_seq, _ = x_sample.shape
    _, n_pool, page, n_heads, dh = cache_k.shape
    assert dh == HEAD_DIM
    w_att = n_heads * dh
    w_pool = pool_scale.shape[1]
    d_ff = w_down.shape[1]
    past_len = page_table.shape[1] * page
    slopes = 2.0 ** (-8.0 * jnp.arange(1, n_heads + 1, dtype=F32) / n_heads)

    l = 0
    w = dict(
        g1=norm1_g[l][None, :], w_in=w_in[l].astype(BF16), pool_maps=pool_maps[l].astype(BF16),
        pool_scale=pool_scale[l][None, :], w_pool_up=w_pool_up[l].astype(BF16), w_att_up=w_att_up[l].astype(BF16),
        w_out=w_out[l].astype(BF16), g2=norm2_g[l][None, :], w_up=w_up[l].astype(BF16), conv_w=conv_w[l],
        conv_b=conv_b[l][None, :], w_down=w_down[l].astype(BF16), gf=final_norm_g[None, :],
    )
    tm = 512

    attend_p = lambda proj3, qc, kc, vc: _prompt_attention(proj3, slopes, n_heads=n_heads, q_col=qc, k_col=kc, v_col=vc)
    y_p, proj_p, tail_p = _layer(x_prompt, None, None, attend_p, 0, w, tm=tm)

    ck = cache_k[l].reshape(n_pool, page, w_att)
    cv = cache_v[l].reshape(n_pool, page, w_att)
    attend_s = lambda proj3, qc, kc, vc: _sample_attention(proj3, ck, cv, page_table, slopes, q_col=qc, k_col=kc, v_col=vc)
    y_s, proj_s, tail_s = _layer(x_sample, state_pool[l], state_conv[l], attend_s, past_len, w, tm=tm)

    heads = lambda p3, col: p3[:, :, col:col + w_att].reshape(p3.shape[0], p3.shape[1], n_heads, dh)[None]
    k_p, v_p = heads(proj_p, w_pool + w_att), heads(proj_p, w_pool + 2 * w_att)
    k_s, v_s = heads(proj_s, w_pool + w_att), heads(proj_s, w_pool + 2 * w_att)
    pool_p = proj_p[:, seq - POOL_HIST:, :w_pool][None]
    pool_s = jnp.concatenate([state_pool[l], proj_s[:, :, :w_pool]], axis=1)[:, -POOL_HIST:][None]
    tiles_per_seq = seq // tm
    conv_p = tail_p.reshape(bsz, tiles_per_seq, SUBLANES, d_ff)[:, -1, SUBLANES - (CONV_W - 1):][None]
    conv_s = jnp.concatenate([state_conv[l], tail_s.reshape(n_dec, dec_seq, d_ff)], axis=1)[:, -(CONV_W - 1):][None]
    return (y_p, y_s, k_p, v_p, k_s, v_s, pool_p, pool_s, conv_p, conv_s)
```

```python
import functools

import jax
import jax.numpy as jnp
from jax import lax
from jax.experimental import pallas as pl
from jax.experimental.pallas import tpu as pltpu

POOL_WINDOWS = (2, 4, 8, 16)
POOL_HIST = max(POOL_WINDOWS) - 1
HEAD_DIM = 128
MOBA_BLOCK = 256
MOBA_TOPK = 3
CONV_W = 3
RMS_EPS = 1e-6
NEG_INF = -1e30

SUBLANES = 8
HALO_ROWS = 16
V7X_VMEM_BYTES = 64 * 1024 * 1024
VMEM_LIMIT_BYTES = 56 * 1024 * 1024

BF16 = jnp.bfloat16
F32 = jnp.float32


def _params(*semantics):
    return pltpu.CompilerParams(dimension_semantics=semantics, vmem_limit_bytes=VMEM_LIMIT_BYTES)


def _rmsnorm(x, g):
    return x * lax.rsqrt(jnp.mean(x * x, axis=-1, keepdims=True) + RMS_EPS) * g


def _dot(a, b):
    return jnp.dot(a, b, preferred_element_type=F32)


def _dot_nt(a, b):
    return lax.dot_general(a, b, (((1,), (1,)), ((), ())), preferred_element_type=F32)


def _split_bf16(x):
    hi = x.astype(BF16)
    lo = (x - hi.astype(F32)).astype(BF16)
    return hi, lo


def _dot_nt_3pass(a, b):
    a_hi, a_lo = _split_bf16(a)
    b_hi, b_lo = _split_bf16(b)
    return _dot_nt(a_hi, b_hi) + (_dot_nt(a_hi, b_lo) + _dot_nt(a_lo, b_hi))


def _norm_matmul_kernel(x_ref, g_ref, w_ref, o_ref, xn_ref):
    @pl.when(pl.program_id(1) == 0)
    def _():
        xn_ref[...] = _rmsnorm(x_ref[...], g_ref[...]).astype(BF16)

    o_ref[...] = _dot(xn_ref[...], w_ref[...])


def _norm_matmul(x, g, w, *, tm, tn):
    rows, d = x.shape
    n = w.shape[1]
    return pl.pallas_call(
        _norm_matmul_kernel,
        grid=(rows // tm, n // tn),
        in_specs=[
            pl.BlockSpec((tm, d), lambda i, j: (i, 0)),
            pl.BlockSpec((1, d), lambda i, j: (0, 0)),
            pl.BlockSpec((d, tn), lambda i, j: (0, j)),
        ],
        out_specs=pl.BlockSpec((tm, tn), lambda i, j: (i, j)),
        out_shape=jax.ShapeDtypeStruct((rows, n), F32),
        scratch_shapes=[pltpu.VMEM((tm, d), BF16)],
        compiler_params=_params("parallel", "arbitrary"),
        name="in_proj",
    )(x, g, w)


def _pool_kernel(u_ref, hist_ref, maps_ref, scale_ref, o_ref, ext_ref, *, pos0, zero_first_hist):
    s_blk, l_blk, width = u_ref.shape
    group = width // len(POOL_WINDOWS)
    tile = pl.program_id(1)
    u = u_ref[...]
    ext_ref[:, HALO_ROWS:, :] = u
    hist_rows = hist_ref.shape[1]
    ext_ref[:, HALO_ROWS - hist_rows:HALO_ROWS, :] = hist_ref[...]
    if zero_first_hist:
        @pl.when(tile == 0)
        def _():
            ext_ref[:, 0:HALO_ROWS, :] = jnp.zeros((s_blk, HALO_ROWS, width), F32)

    pos = pos0 + tile * l_blk + lax.broadcasted_iota(jnp.int32, (s_blk, l_blk, group), 1)
    outs = []
    for gi, w in enumerate(POOL_WINDOWS):
        cols = slice(gi * group, (gi + 1) * group)
        win = u[:, :, cols]
        for k in range(1, w):
            win = win + ext_ref[:, HALO_ROWS - k:HALO_ROWS - k + l_blk, cols]
        cnt = jnp.minimum(w, pos + 1).astype(F32)
        d = win / cnt - u[:, :, cols]
        d2 = d.reshape(s_blk * l_blk, group).astype(BF16)
        outs.append(_dot(d2, maps_ref[gi]))
    m = jnp.concatenate(outs, axis=-1) * scale_ref[...]
    o_ref[...] = m.astype(o_ref.dtype)


def _pool_mix(proj3, hist, maps, scale, *, s_blk, l_blk, pos0, out_dtype):
    n_seq, seq_len, _ = proj3.shape
    n_groups, group, _ = maps.shape
    width = n_groups * group
    tiles = seq_len // l_blk
    if hist is None:
        halo_blocks = l_blk // HALO_ROWS
        hist_arr = proj3
        hist_spec = pl.BlockSpec((s_blk, HALO_ROWS, width),
                                 lambda s, t: (s, jnp.maximum(t * halo_blocks - 1, 0), 0))
    else:
        assert tiles == 1
        hist_arr = hist
        hist_spec = pl.BlockSpec((s_blk, hist.shape[1], width), lambda s, t: (s, 0, 0))
    kern = functools.partial(_pool_kernel, pos0=pos0, zero_first_hist=hist is None)
    return pl.pallas_call(
        kern,
        grid=(n_seq // s_blk, tiles),
        in_specs=[
            pl.BlockSpec((s_blk, l_blk, width), lambda s, t: (s, t, 0)),
            hist_spec,
            pl.BlockSpec((n_groups, group, group), lambda s, t: (0, 0, 0)),
            pl.BlockSpec((1, width), lambda s, t: (0, 0)),
        ],
        out_specs=pl.BlockSpec((s_blk * l_blk, width), lambda s, t: (s * tiles + t, 0)),
        out_shape=jax.ShapeDtypeStruct((n_seq * seq_len, width), out_dtype),
        scratch_shapes=[pltpu.VMEM((s_blk, HALO_ROWS + l_blk, width), F32)],
        compiler_params=_params("parallel", "arbitrary"),
        name="pool_mix",
    )(proj3, hist_arr, maps, scale)


def _topk_select(route, n_valid):
    nb, n = route.shape
    blk = lax.broadcasted_iota(jnp.int32, (nb, n), 0)
    rank = jnp.zeros((nb, n), jnp.int32)
    for other in range(nb):
        c = route[other:other + 1, :]
        beats = jnp.where((c > route) | ((c == route) & (other < blk)), 1, 0)
        rank = rank + beats * (other < n_valid).astype(jnp.int32)
    return (blk < n_valid) & (rank < MOBA_TOPK)


def _topk_select_list(scores):
    sel = []
    for j, s_j in enumerate(scores):
        rank = jnp.zeros(s_j.shape, jnp.int32)
        for other, s_o in enumerate(scores):
            if other < j:
                rank = rank + jnp.where(s_o >= s_j, 1, 0)
            elif other > j:
                rank = rank + jnp.where(s_o > s_j, 1, 0)
        sel.append(rank < MOBA_TOPK)
    return sel


KV_TILE = 2 * MOBA_BLOCK
NEVER = 1 << 30
PROMPT_HEADS_PER_STEP = 4


def _prompt_attn_kernel(slopes_ref, q_ref, k_ref, v_ref, o_ref,
                        kb_ref, vt_ref, kmean_ref, alibi_ref, m_ref, l_ref, acc_ref):
    hgroup = pl.program_id(1)
    qt = pl.program_id(2)
    n_hb, seq_len, dh = kb_ref.shape
    tq = q_ref.shape[1]
    nb = seq_len // MOBA_BLOCK
    blocks_per_tile = KV_TILE // MOBA_BLOCK
    slopes = [slopes_ref[hgroup * n_hb + hh] for hh in range(n_hb)]
    q_minus_k = (lax.broadcasted_iota(jnp.int32, (KV_TILE, tq), 1)
                 - lax.broadcasted_iota(jnp.int32, (KV_TILE, tq), 0))

    @pl.when(qt == 0)
    def _():
        for hh in range(n_hb):
            cols = slice(hh * dh, (hh + 1) * dh)
            k = k_ref[0, :, cols]
            kb_ref[hh] = k.astype(BF16)
            kmean_ref[hh] = jnp.mean(k.reshape(nb, MOBA_BLOCK, dh), axis=1)
            for c in range(seq_len // KV_TILE):
                vt_ref[hh, c] = v_ref[0, c * KV_TILE:(c + 1) * KV_TILE, cols].T.astype(BF16)
            alibi_ref[hh] = slopes[hh] * q_minus_k.astype(F32)

    blk = lax.broadcasted_iota(jnp.int32, (nb, tq), 0)
    qbs, sel_bits = [], []
    for hh in range(n_hb):
        q = q_ref[0, :, hh * dh:(hh + 1) * dh]
        qbs.append((q * dh ** -0.5).astype(BF16))
        route = _dot_nt_3pass(kmean_ref[hh], q)
        sel = _topk_select(route, qt) | (blk == qt)
        bits = jnp.sum(jnp.where(sel, jnp.left_shift(1, blk).astype(F32), 0.0), axis=0, keepdims=True)
        sel_bits.append(bits.astype(jnp.int32))

    m_ref[...] = jnp.full(m_ref.shape, NEG_INF, F32)
    l_ref[...] = jnp.zeros(l_ref.shape, F32)
    acc_ref[...] = jnp.zeros(acc_ref.shape, F32)

    def kv_step(p, carry):
        start = pl.multiple_of(p * KV_TILE, KV_TILE)
        offset = qt * tq - start
        scores = [_dot_nt(kb_ref[hh, pl.ds(start, KV_TILE), :], qbs[hh]) for hh in range(n_hb)]
        for hh in range(n_hb):
            halves = []
            for half in range(blocks_per_tile):
                bit = jnp.right_shift(sel_bits[hh], p * blocks_per_tile + half) & 1
                thr = jnp.where(bit == 1, 0, NEVER) - offset
                halves.append(jnp.broadcast_to(thr, (MOBA_BLOCK, tq)))
            keep = q_minus_k >= jnp.concatenate(halves, axis=0)
            s = scores[hh] - alibi_ref[hh] - slopes[hh] * offset.astype(F32)
            s = jnp.where(keep, s, NEG_INF)
            m_old = m_ref[hh]
            m_new = jnp.maximum(m_old, jnp.max(s, axis=0, keepdims=True))
            alpha = jnp.exp(m_old - m_new)
            pr = jnp.where(keep, jnp.exp(s - m_new), 0.0)
            l_ref[hh] = alpha * l_ref[hh] + jnp.sum(pr, axis=0, keepdims=True)
            acc_ref[hh] = alpha * acc_ref[hh] + _dot(vt_ref[hh, p], pr.astype(BF16))
            m_ref[hh] = m_new
        return carry

    lax.fori_loop(0, qt // blocks_per_tile + 1, kv_step, 0)
    for hh in range(n_hb):
        o_ref[:, hh * dh:(hh + 1) * dh] = (acc_ref[hh] / l_ref[hh]).T.astype(o_ref.dtype)


def _prompt_attention(proj3, slopes, *, n_heads, q_col, k_col, v_col):
    bsz, seq_len, _ = proj3.shape
    tq = MOBA_BLOCK
    nq = seq_len // tq
    nb = seq_len // MOBA_BLOCK
    hb = PROMPT_HEADS_PER_STEP
    gw = hb * HEAD_DIM
    assert n_heads % hb == 0 and seq_len % KV_TILE == 0
    qb0, kb0, vb0 = q_col // gw, k_col // gw, v_col // gw
    return pl.pallas_call(
        _prompt_attn_kernel,
        grid=(bsz, n_heads // hb, nq),
        in_specs=[
            pl.BlockSpec(memory_space=pltpu.SMEM),
            pl.BlockSpec((1, tq, gw), lambda b, h, i: (b, i, qb0 + h)),
            pl.BlockSpec((1, seq_len, gw), lambda b, h, i: (b, 0, kb0 + h)),
            pl.BlockSpec((1, seq_len, gw), lambda b, h, i: (b, 0, vb0 + h)),
        ],
        out_specs=pl.BlockSpec((tq, gw), lambda b, h, i: (b * nq + i, h)),
        out_shape=jax.ShapeDtypeStruct((bsz * seq_len, n_heads * HEAD_DIM), BF16),
        scratch_shapes=[
            pltpu.VMEM((hb, seq_len, HEAD_DIM), BF16),
            pltpu.VMEM((hb, seq_len // KV_TILE, HEAD_DIM, KV_TILE), BF16),
            pltpu.VMEM((hb, nb, HEAD_DIM), F32),
            pltpu.VMEM((hb, KV_TILE, tq), F32),
            pltpu.VMEM((hb, 1, tq), F32),
            pltpu.VMEM((hb, 1, tq), F32),
            pltpu.VMEM((hb, HEAD_DIM, tq), F32),
        ],
        compiler_params=_params("parallel", "parallel", "arbitrary"),
        name="prompt_attn",
    )(slopes, proj3, proj3, proj3)


def _block_diag_rows(x, n_heads, dh):
    t = x.shape[0]
    tiled = jnp.concatenate([x] * n_heads, axis=0)
    row_head = lax.broadcasted_iota(jnp.int32, tiled.shape, 0) // t
    col_head = lax.broadcasted_iota(jnp.int32, tiled.shape, 1) // dh
    return jnp.where(row_head == col_head, tiled, 0.0)


def _diag_blocks(x, n_heads, t, dh):
    return jnp.concatenate([x[h * t:(h + 1) * t, h * dh:(h + 1) * dh] for h in range(n_heads)], axis=0)


def _sample_attn_kernel(pt_ref, slopes_ref, q_ref, kn_ref, vn_ref, *rest, past_len, n_pages):
    del pt_ref
    k_refs, v_refs, o_ref = rest[:n_pages], rest[n_pages:2 * n_pages], rest[2 * n_pages]
    t, width = q_ref.shape[1], q_ref.shape[2]
    page = k_refs[0].shape[1]
    pages_per_block = MOBA_BLOCK // page
    nb = n_pages // pages_per_block
    dh = HEAD_DIM
    n_heads = width // dh
    rows = n_heads * t

    wq = _block_diag_rows(q_ref[0], n_heads, dh)
    wqb = (wq * dh ** -0.5).astype(BF16)
    row_head = lax.broadcasted_iota(jnp.int32, (rows, 1), 0) // t
    slope_rows = jnp.zeros((rows, 1), F32)
    for h in range(n_heads):
        slope_rows = jnp.where(row_head == h, slopes_ref[h], slope_rows)
    q_pos = past_len + lax.broadcasted_iota(jnp.int32, (rows, 1), 0) % t

    route, scores = [], []
    for b in range(nb):
        pages = range(b * pages_per_block, (b + 1) * pages_per_block)
        kblk = jnp.concatenate([k_refs[pg][0] for pg in pages], axis=0)
        kmean = jnp.mean(kblk, axis=0, keepdims=True)
        route.append(jnp.sum(wq * kmean, axis=-1, keepdims=True))
        scores.append(_dot_nt(wqb, kblk.astype(BF16)))
    m_blk, l_blk, acc_blk = [], [], []
    for b in range(nb):
        pages = range(b * pages_per_block, (b + 1) * pages_per_block)
        vblk = jnp.concatenate([v_refs[pg][0] for pg in pages], axis=0)
        k_pos = b * MOBA_BLOCK + lax.broadcasted_iota(jnp.int32, (rows, MOBA_BLOCK), 1)
        s = scores[b] - slope_rows * (q_pos - k_pos).astype(F32)
        m_b = jnp.max(s, axis=-1, keepdims=True)
        p = jnp.exp(s - m_b)
        m_blk.append(m_b)
        l_blk.append(jnp.sum(p, axis=-1, keepdims=True))
        acc_blk.append(_diag_blocks(_dot(p.astype(BF16), vblk.astype(BF16)), n_heads, t, dh))

    sel = _topk_select_list(route)
    dist = (lax.broadcasted_iota(jnp.int32, (rows, t), 0) % t
            - lax.broadcasted_iota(jnp.int32, (rows, t), 1))
    keep = dist >= 0
    s_own = _dot_nt(wqb, kn_ref[0].astype(BF16)) - slope_rows * dist.astype(F32)
    s_own = jnp.where(keep, s_own, NEG_INF)
    m_all = jnp.max(s_own, axis=-1, keepdims=True)
    for b in range(nb):
        m_all = jnp.maximum(m_all, jnp.where(sel[b], m_blk[b], NEG_INF))
    p_own = jnp.where(keep, jnp.exp(s_own - m_all), 0.0)
    den = jnp.sum(p_own, axis=-1, keepdims=True)
    num = _diag_blocks(_dot(p_own.astype(BF16), vn_ref[0].astype(BF16)), n_heads, t, dh)
    for b in range(nb):
        w_b = jnp.where(sel[b], jnp.exp(m_blk[b] - m_all), 0.0)
        den = den + w_b * l_blk[b]
        num = num + w_b * acc_blk[b]
    out = num / den
    o_ref[...] = jnp.concatenate([out[h * t:(h + 1) * t, :] for h in range(n_heads)], axis=-1)


def _sample_attention(proj3, cache_k, cache_v, page_table, slopes, *, q_col, k_col, v_col):
    n_seq, t, _ = proj3.shape
    _, page, width = cache_k.shape
    n_pages = page_table.shape[1]
    assert MOBA_BLOCK % page == 0 and (n_pages * page) % MOBA_BLOCK == 0 and t <= MOBA_BLOCK
    past_len = n_pages * page
    qb0, kb0, vb0 = q_col // width, k_col // width, v_col // width
    page_specs = [pl.BlockSpec((1, page, width), lambda s, pt, pg=pg: (pt[s, pg], 0, 0)) for pg in range(n_pages)]
    grid_spec = pltpu.PrefetchScalarGridSpec(
        num_scalar_prefetch=1,
        grid=(n_seq,),
        in_specs=[
            pl.BlockSpec(memory_space=pltpu.SMEM),
            pl.BlockSpec((1, t, width), lambda s, pt: (s, 0, qb0)),
            pl.BlockSpec((1, t, width), lambda s, pt: (s, 0, kb0)),
            pl.BlockSpec((1, t, width), lambda s, pt: (s, 0, vb0)),
            *page_specs, *page_specs,
        ],
        out_specs=pl.BlockSpec((t, width), lambda s, pt: (s, 0)),
    )
    return pl.pallas_call(
        functools.partial(_sample_attn_kernel, past_len=past_len, n_pages=n_pages),
        grid_spec=grid_spec,
        out_shape=jax.ShapeDtypeStruct((n_seq * t, width), F32),
        compiler_params=_params("parallel"),
        name="sample_attn",
    )(page_table, slopes, proj3, proj3, proj3, *([cache_k] * n_pages), *([cache_v] * n_pages))


def _merge_kernel(a_ref, b_ref, wp_ref, wa_ref, ga_ref, gb_ref, o_ref):
    pa = _dot(a_ref[...].astype(BF16), wp_ref[...])
    pb = _dot(b_ref[...].astype(BF16), wa_ref[...])
    merged = jax.nn.sigmoid(ga_ref[...]) * pa + jax.nn.sigmoid(gb_ref[...]) * pb
    o_ref[...] = merged.astype(o_ref.dtype)


def _merge(a_out, b_out, w_pool_up, w_att_up, proj, *, ga_col, gb_col, tm, tn):
    rows, wa = a_out.shape
    wb = b_out.shape[1]
    n = w_pool_up.shape[1]
    ga0, gb0 = ga_col // tn, gb_col // tn
    return pl.pallas_call(
        _merge_kernel,
        grid=(rows // tm, n // tn),
        in_specs=[
            pl.BlockSpec((tm, wa), lambda i, j: (i, 0)),
            pl.BlockSpec((tm, wb), lambda i, j: (i, 0)),
            pl.BlockSpec((wa, tn), lambda i, j: (0, j)),
            pl.BlockSpec((wb, tn), lambda i, j: (0, j)),
            pl.BlockSpec((tm, tn), lambda i, j: (i, ga0 + j)),
            pl.BlockSpec((tm, tn), lambda i, j: (i, gb0 + j)),
        ],
        out_specs=pl.BlockSpec((tm, tn), lambda i, j: (i, j)),
        out_shape=jax.ShapeDtypeStruct((rows, n), BF16),
        compiler_params=_params("parallel", "arbitrary"),
        name="merge",
    )(a_out, b_out, w_pool_up, w_att_up, proj, proj)


def _matmul_residual_kernel(a_ref, w_ref, r_ref, o_ref):
    o_ref[...] = r_ref[...] + _dot(a_ref[...], w_ref[...])


def _matmul_residual(a, w, res, *, tm, tn):
    rows, k = a.shape
    n = w.shape[1]
    return pl.pallas_call(
        _matmul_residual_kernel,
        grid=(rows // tm, n // tn),
        in_specs=[
            pl.BlockSpec((tm, k), lambda i, j: (i, 0)),
            pl.BlockSpec((k, tn), lambda i, j: (0, j)),
            pl.BlockSpec((tm, tn), lambda i, j: (i, j)),
        ],
        out_specs=pl.BlockSpec((tm, tn), lambda i, j: (i, j)),
        out_shape=jax.ShapeDtypeStruct((rows, n), F32),
        compiler_params=_params("parallel", "arbitrary"),
        name="out_proj",
    )(a, w, res)


def _gelu_exact(x):
    return 0.5 * x * (1.0 + lax.erf(x * (2.0 ** -0.5)))


def _ffn_kernel(h_ref, g2_ref, wa_ref, wg_ref, cw_ref, cb_ref, wd_ref, gf_ref, *rest,
                seq_rows, tiles_per_seq, per_seq_hist):
    hist_ref = rest[0] if per_seq_hist else None
    y_ref, tail_ref, hn_ref, acc_ref, carry_ref = rest[-5:]
    i = pl.program_id(0)
    j = pl.program_id(1)
    tm = h_ref.shape[0]

    @pl.when(j == 0)
    def _():
        hn_ref[...] = _rmsnorm(h_ref[...], g2_ref[...]).astype(BF16)
        acc_ref[...] = jnp.zeros(acc_ref.shape, F32)

    hn = hn_ref[...]
    a = _dot(hn, wa_ref[...])
    gv = _dot(hn, wg_ref[...])
    tn = a.shape[1]
    row = lax.broadcasted_iota(jnp.int32, (tm, tn), 0)
    if hist_ref is not None:
        n_seq = tm // seq_rows
        hist = hist_ref[...]
        prev = [jnp.broadcast_to(hist[:, r:r + 1, :], (n_seq, seq_rows, tn)).reshape(tm, tn)
                for r in range(CONV_W - 1)]
        t_in_seq = row % seq_rows
        tail_ref[...] = a
    else:
        @pl.when(i % tiles_per_seq == 0)
        def _():
            carry_ref[j] = jnp.zeros((SUBLANES, tn), F32)
        carry = carry_ref[j]
        prev = [jnp.broadcast_to(carry[SUBLANES - (CONV_W - 1) + r:SUBLANES - (CONV_W - 1) + r + 1, :], (tm, tn))
                for r in range(CONV_W - 1)]
        t_in_seq = row
        tail = a[tm - SUBLANES:, :]
        carry_ref[j] = tail
        tail_ref[0] = tail
    ac = cb_ref[...] + cw_ref[CONV_W - 1:CONV_W, :] * a
    for back in range(1, CONV_W):
        shifted = pltpu.roll(a, back, axis=0)
        for r in range(back):
            shifted = jnp.where(t_in_seq == r, prev[CONV_W - 1 - back + r], shifted)
        ac = ac + cw_ref[CONV_W - 1 - back:CONV_W - back, :] * shifted
    act = (_gelu_exact(ac) * gv).astype(BF16)
    acc_ref[...] += _dot(act, wd_ref[...])

    @pl.when(j == pl.num_programs(1) - 1)
    def _():
        y = h_ref[...] + acc_ref[...]
        y_ref[...] = _rmsnorm(y, gf_ref[...])


def _ffn(h, g2, w_up, conv_w, conv_b, w_down, gf, hist, *, seq_rows, tm, tn):
    rows, d = h.shape
    d_ff = w_down.shape[0]
    n_col = d_ff // tn
    n_row = rows // tm
    per_seq_hist = hist is not None
    if per_seq_hist:
        assert tm % seq_rows == 0
        tiles_per_seq = 1
        tail_shape = jax.ShapeDtypeStruct((rows, d_ff), F32)
        tail_spec = pl.BlockSpec((tm, tn), lambda i, j: (i, j))
        hist_args = (hist,)
        hist_specs = [pl.BlockSpec((tm // seq_rows, CONV_W - 1, tn), lambda i, j: (i, 0, j))]
    else:
        assert seq_rows % tm == 0
        tiles_per_seq = seq_rows // tm
        tail_shape = jax.ShapeDtypeStruct((n_row, SUBLANES, d_ff), F32)
        tail_spec = pl.BlockSpec((1, SUBLANES, tn), lambda i, j: (i, 0, j))
        hist_args = ()
        hist_specs = []
    y, tail = pl.pallas_call(
        functools.partial(_ffn_kernel, seq_rows=seq_rows, tiles_per_seq=tiles_per_seq, per_seq_hist=per_seq_hist),
        grid=(n_row, n_col),
        in_specs=[
            pl.BlockSpec((tm, d), lambda i, j: (i, 0)),
            pl.BlockSpec((1, d), lambda i, j: (0, 0)),
            pl.BlockSpec((d, tn), lambda i, j: (0, j)),
            pl.BlockSpec((d, tn), lambda i, j: (0, n_col + j)),
            pl.BlockSpec((CONV_W, tn), lambda i, j: (0, j)),
            pl.BlockSpec((1, tn), lambda i, j: (0, j)),
            pl.BlockSpec((tn, d), lambda i, j: (j, 0)),
            pl.BlockSpec((1, d), lambda i, j: (0, 0)),
            *hist_specs,
        ],
        out_specs=[pl.BlockSpec((tm, d), lambda i, j: (i, 0)), tail_spec],
        out_shape=[jax.ShapeDtypeStruct((rows, d), F32), tail_shape],
        scratch_shapes=[
            pltpu.VMEM((tm, d), BF16),
            pltpu.VMEM((tm, d), F32),
            pltpu.VMEM((n_col, SUBLANES, tn), F32),
        ],
        compiler_params=_params("arbitrary", "arbitrary"),
        name="ffn",
    )(h, g2, w_up, w_up, conv_w, conv_b, w_down, gf, *hist_args)
    return y, tail


def _layer(x3, pool_hist, conv_hist, attend, pos0, w, *, tm):
    n_seq, seq_len, d = x3.shape
    rows = n_seq * seq_len
    x = x3.reshape(rows, d)
    w_pool = w["pool_scale"].shape[1]
    w_att = w["w_att_up"].shape[0]
    proj = _norm_matmul(x, w["g1"], w["w_in"], tm=tm, tn=1024)
    proj3 = proj.reshape(n_seq, seq_len, -1)
    q_col, k_col, v_col = w_pool, w_pool + w_att, w_pool + 2 * w_att
    ga_col, gb_col = w_pool + 3 * w_att, w_pool + 3 * w_att + d
    if pool_hist is None:
        a_out = _pool_mix(proj3, None, w["pool_maps"], w["pool_scale"], s_blk=1, l_blk=tm, pos0=pos0,
                          out_dtype=BF16)
    else:
        a_out = _pool_mix(proj3, pool_hist, w["pool_maps"], w["pool_scale"], s_blk=32, l_blk=seq_len, pos0=pos0,
                          out_dtype=F32)
    b_out = attend(proj3, q_col, k_col, v_col)
    merged = _merge(a_out, b_out, w["w_pool_up"], w["w_att_up"], proj, ga_col=ga_col, gb_col=gb_col, tm=tm, tn=1024)
    h = _matmul_residual(merged, w["w_out"], x, tm=tm, tn=1024)
    y, tail = _ffn(h, w["g2"], w["w_up"], w["conv_w"], w["conv_b"], w["w_down"], w["gf"], conv_hist,
                   seq_rows=seq_len, tm=tm, tn=512)
    return y.reshape(n_seq, seq_len, d), proj3, tail


def kernel(x_prompt, x_sample, cache_k, cache_v, state_pool, state_conv, page_table, norm1_g, w_in, pool_maps,
           pool_scale, w_pool_up, w_att_up, w_out, norm2_g, w_up, conv_w, conv_b, w_down, final_norm_g):
    depth = w_in.shape[0]
    assert depth == 1, "single-layer trunk"
    bsz, seq, d = x_prompt.shape
    n_dec, dec_seq, _ = x_sample.shape
    _, n_pool, page, n_heads, dh = cache_k.shape
    assert dh == HEAD_DIM
    w_att = n_heads * dh
    w_pool = pool_scale.shape[1]
    d_ff = w_down.shape[1]
    past_len = page_table.shape[1] * page
    slopes = 2.0 ** (-8.0 * jnp.arange(1, n_heads + 1, dtype=F32) / n_heads)

    l = 0
    w = dict(
        g1=norm1_g[l][None, :], w_in=w_in[l].astype(BF16), pool_maps=pool_maps[l].astype(BF16),
        pool_scale=pool_scale[l][None, :], w_pool_up=w_pool_up[l].astype(BF16), w_att_up=w_att_up[l].astype(BF16),
        w_out=w_out[l].astype(BF16), g2=norm2_g[l][None, :], w_up=w_up[l].astype(BF16), conv_w=conv_w[l],
        conv_b=conv_b[l][None, :], w_down=w_down[l].astype(BF16), gf=final_norm_g[None, :],
    )
    tm = 512

    attend_p = lambda proj3, qc, kc, vc: _prompt_attention(proj3, slopes, n_heads=n_heads, q_col=qc, k_col=kc, v_col=vc)
    y_p, proj_p, tail_p = _layer(x_prompt, None, None, attend_p, 0, w, tm=tm)

    ck = cache_k.reshape(depth * n_pool, page, w_att)
    cv = cache_v.reshape(depth * n_pool, page, w_att)
    pages = page_table + l * n_pool
    attend_s = lambda proj3, qc, kc, vc: _sample_attention(proj3, ck, cv, pages, slopes, q_col=qc, k_col=kc, v_col=vc)
    y_s, proj_s, tail_s = _layer(x_sample, state_pool[l], state_conv[l], attend_s, past_len, w, tm=tm)

    heads = lambda p3, col: p3[:, :, col:col + w_att].reshape(p3.shape[0], p3.shape[1], n_heads, dh)[None]
    k_p, v_p = heads(proj_p, w_pool + w_att), heads(proj_p, w_pool + 2 * w_att)
    k_s, v_s = heads(proj_s, w_pool + w_att), heads(proj_s, w_pool + 2 * w_att)
    pool_p = proj_p[:, seq - POOL_HIST:, :w_pool][None]
    pool_s = jnp.concatenate([state_pool[l], proj_s[:, :, :w_pool]], axis=1)[:, -POOL_HIST:][None]
    tiles_per_seq = seq // tm
    conv_p = tail_p.reshape(bsz, tiles_per_seq, SUBLANES, d_ff)[:, -1, SUBLANES - (CONV_W - 1):][None]
    conv_s = jnp.concatenate([state_conv[l], tail_s.reshape(n_dec, dec_seq, d_ff)], axis=1)[:, -(CONV_W - 1):][None]
    return (y_p, y_s, k_p, v_p, k_s, v_s, pool_p, pool_s, conv_p, conv_s)
```

```python
import functools

import jax
import jax.numpy as jnp
from jax import lax
from jax.experimental import pallas as pl
from jax.experimental.pallas import tpu as pltpu

POOL_WINDOWS = (2, 4, 8, 16)
POOL_HIST = max(POOL_WINDOWS) - 1
HEAD_DIM = 128
MOBA_BLOCK = 256
MOBA_TOPK = 3
CONV_W = 3
RMS_EPS = 1e-6
NEG_INF = -1e30

SUBLANES = 8
HALO_ROWS = 16
ROW_TILE = 1024
POOL_TILE_ROWS = 512
V7X_VMEM_BYTES = 64 * 1024 * 1024
VMEM_LIMIT_BYTES = 56 * 1024 * 1024

BF16 = jnp.bfloat16
F32 = jnp.float32


def _params(*semantics):
    return pltpu.CompilerParams(dimension_semantics=semantics, vmem_limit_bytes=VMEM_LIMIT_BYTES)


def _rmsnorm(x, g):
    return x * lax.rsqrt(jnp.mean(x * x, axis=-1, keepdims=True) + RMS_EPS) * g


def _dot(a, b):
    return jnp.dot(a, b, preferred_element_type=F32)


def _dot_nt(a, b):
    return lax.dot_general(a, b, (((1,), (1,)), ((), ())), preferred_element_type=F32)


def _split_bf16(x):
    hi = x.astype(BF16)
    lo = (x - hi.astype(F32)).astype(BF16)
    return hi, lo


def _dot_nt_3pass(a, b):
    a_hi, a_lo = _split_bf16(a)
    b_hi, b_lo = _split_bf16(b)
    return _dot_nt(a_hi, b_hi) + (_dot_nt(a_hi, b_lo) + _dot_nt(a_lo, b_hi))


def _norm_matmul_kernel(x_ref, g_ref, w_ref, o_ref, xn_ref):
    @pl.when(pl.program_id(1) == 0)
    def _():
        xn_ref[...] = _rmsnorm(x_ref[...], g_ref[...]).astype(BF16)

    o_ref[...] = _dot(xn_ref[...], w_ref[...])


def _norm_matmul(x, g, w, *, tm, tn):
    rows, d = x.shape
    n = w.shape[1]
    return pl.pallas_call(
        _norm_matmul_kernel,
        grid=(rows // tm, n // tn),
        in_specs=[
            pl.BlockSpec((tm, d), lambda i, j: (i, 0)),
            pl.BlockSpec((1, d), lambda i, j: (0, 0)),
            pl.BlockSpec((d, tn), lambda i, j: (0, j)),
        ],
        out_specs=pl.BlockSpec((tm, tn), lambda i, j: (i, j)),
        out_shape=jax.ShapeDtypeStruct((rows, n), F32),
        scratch_shapes=[pltpu.VMEM((tm, d), BF16)],
        compiler_params=_params("parallel", "arbitrary"),
        name="in_proj",
    )(x, g, w)


def _pool_kernel(u_ref, hist_ref, maps_ref, scale_ref, o_ref, ext_ref, *, pos0, zero_first_hist):
    s_blk, l_blk, width = u_ref.shape
    group = width // len(POOL_WINDOWS)
    tile = pl.program_id(1)
    u = u_ref[...]
    ext_ref[:, HALO_ROWS:, :] = u
    hist_rows = hist_ref.shape[1]
    ext_ref[:, HALO_ROWS - hist_rows:HALO_ROWS, :] = hist_ref[...]
    if zero_first_hist:
        @pl.when(tile == 0)
        def _():
            ext_ref[:, 0:HALO_ROWS, :] = jnp.zeros((s_blk, HALO_ROWS, width), F32)

    pos = pos0 + tile * l_blk + lax.broadcasted_iota(jnp.int32, (s_blk, l_blk, group), 1)
    outs = []
    for gi, w in enumerate(POOL_WINDOWS):
        cols = slice(gi * group, (gi + 1) * group)
        win = u[:, :, cols]
        for k in range(1, w):
            win = win + ext_ref[:, HALO_ROWS - k:HALO_ROWS - k + l_blk, cols]
        cnt = jnp.minimum(w, pos + 1).astype(F32)
        d = win / cnt - u[:, :, cols]
        d2 = d.reshape(s_blk * l_blk, group).astype(BF16)
        outs.append(_dot(d2, maps_ref[gi]))
    m = jnp.concatenate(outs, axis=-1) * scale_ref[...]
    o_ref[...] = m.astype(o_ref.dtype)


def _pool_mix(proj3, hist, maps, scale, *, s_blk, l_blk, pos0, out_dtype):
    n_seq, seq_len, _ = proj3.shape
    n_groups, group, _ = maps.shape
    width = n_groups * group
    tiles = seq_len // l_blk
    if hist is None:
        halo_blocks = l_blk // HALO_ROWS
        hist_arr = proj3
        hist_spec = pl.BlockSpec((s_blk, HALO_ROWS, width),
                                 lambda s, t: (s, jnp.maximum(t * halo_blocks - 1, 0), 0))
    else:
        assert tiles == 1
        hist_arr = hist
        hist_spec = pl.BlockSpec((s_blk, hist.shape[1], width), lambda s, t: (s, 0, 0))
    kern = functools.partial(_pool_kernel, pos0=pos0, zero_first_hist=hist is None)
    return pl.pallas_call(
        kern,
        grid=(n_seq // s_blk, tiles),
        in_specs=[
            pl.BlockSpec((s_blk, l_blk, width), lambda s, t: (s, t, 0)),
            hist_spec,
            pl.BlockSpec((n_groups, group, group), lambda s, t: (0, 0, 0)),
            pl.BlockSpec((1, width), lambda s, t: (0, 0)),
        ],
        out_specs=pl.BlockSpec((s_blk * l_blk, width), lambda s, t: (s * tiles + t, 0)),
        out_shape=jax.ShapeDtypeStruct((n_seq * seq_len, width), out_dtype),
        scratch_shapes=[pltpu.VMEM((s_blk, HALO_ROWS + l_blk, width), F32)],
        compiler_params=_params("parallel", "arbitrary"),
        name="pool_mix",
    )(proj3, hist_arr, maps, scale)


def _topk_select(route, n_valid):
    nb, n = route.shape
    blk = lax.broadcasted_iota(jnp.int32, (nb, n), 0)
    rank = jnp.zeros((nb, n), jnp.int32)
    for other in range(nb):
        c = route[other:other + 1, :]
        beats = jnp.where((c > route) | ((c == route) & (other < blk)), 1, 0)
        rank = rank + beats * (other < n_valid).astype(jnp.int32)
    return (blk < n_valid) & (rank < MOBA_TOPK)


def _topk_select_list(scores):
    sel = []
    for j, s_j in enumerate(scores):
        rank = jnp.zeros(s_j.shape, jnp.int32)
        for other, s_o in enumerate(scores):
            if other < j:
                rank = rank + jnp.where(s_o >= s_j, 1, 0)
            elif other > j:
                rank = rank + jnp.where(s_o > s_j, 1, 0)
        sel.append(rank < MOBA_TOPK)
    return sel


KV_TILE = 2 * MOBA_BLOCK
NEVER = 1 << 30
PROMPT_HEADS_PER_STEP = 4


def _prompt_attn_kernel(slopes_ref, q_ref, k_ref, v_ref, o_ref,
                        kb_ref, vt_ref, kmean_ref, alibi_ref, m_ref, l_ref, acc_ref):
    hgroup = pl.program_id(1)
    qt = pl.program_id(2)
    n_hb, seq_len, dh = kb_ref.shape
    tq = q_ref.shape[1]
    nb = seq_len // MOBA_BLOCK
    blocks_per_tile = KV_TILE // MOBA_BLOCK
    slopes = [slopes_ref[hgroup * n_hb + hh] for hh in range(n_hb)]
    q_minus_k = (lax.broadcasted_iota(jnp.int32, (KV_TILE, tq), 1)
                 - lax.broadcasted_iota(jnp.int32, (KV_TILE, tq), 0))

    @pl.when(qt == 0)
    def _():
        for hh in range(n_hb):
            cols = slice(hh * dh, (hh + 1) * dh)
            k = k_ref[0, :, cols]
            kb_ref[hh] = k.astype(BF16)
            kmean_ref[hh] = jnp.mean(k.reshape(nb, MOBA_BLOCK, dh), axis=1)
            for c in range(seq_len // KV_TILE):
                vt_ref[hh, c] = v_ref[0, c * KV_TILE:(c + 1) * KV_TILE, cols].T.astype(BF16)
            alibi_ref[hh] = slopes[hh] * q_minus_k.astype(F32)

    blk = lax.broadcasted_iota(jnp.int32, (nb, tq), 0)
    qbs, sel_bits = [], []
    for hh in range(n_hb):
        q = q_ref[0, :, hh * dh:(hh + 1) * dh]
        qbs.append((q * dh ** -0.5).astype(BF16))
        route = _dot_nt_3pass(kmean_ref[hh], q)
        sel = _topk_select(route, qt) | (blk == qt)
        bits = jnp.sum(jnp.where(sel, jnp.left_shift(1, blk).astype(F32), 0.0), axis=0, keepdims=True)
        sel_bits.append(bits.astype(jnp.int32))

    m_ref[...] = jnp.full(m_ref.shape, NEG_INF, F32)
    l_ref[...] = jnp.zeros(l_ref.shape, F32)
    acc_ref[...] = jnp.zeros(acc_ref.shape, F32)

    def kv_step(p, carry):
        start = pl.multiple_of(p * KV_TILE, KV_TILE)
        offset = qt * tq - start
        scores = [_dot_nt(kb_ref[hh, pl.ds(start, KV_TILE), :], qbs[hh]) for hh in range(n_hb)]
        for hh in range(n_hb):
            halves = []
            for half in range(blocks_per_tile):
                bit = jnp.right_shift(sel_bits[hh], p * blocks_per_tile + half) & 1
                thr = jnp.where(bit == 1, 0, NEVER) - offset
                halves.append(jnp.broadcast_to(thr, (MOBA_BLOCK, tq)))
            keep = q_minus_k >= jnp.concatenate(halves, axis=0)
            s = scores[hh] - alibi_ref[hh] - slopes[hh] * offset.astype(F32)
            s = jnp.where(keep, s, NEG_INF)
            m_old = m_ref[hh]
            m_new = jnp.maximum(m_old, jnp.max(s, axis=0, keepdims=True))
            alpha = jnp.exp(m_old - m_new)
            pr = jnp.where(keep, jnp.exp(s - m_new), 0.0)
            l_ref[hh] = alpha * l_ref[hh] + jnp.sum(pr, axis=0, keepdims=True)
            acc_ref[hh] = alpha * acc_ref[hh] + _dot(vt_ref[hh, p], pr.astype(BF16))
            m_ref[hh] = m_new
        return carry

    lax.fori_loop(0, qt // blocks_per_tile + 1, kv_step, 0)
    for hh in range(n_hb):
        o_ref[:, hh * dh:(hh + 1) * dh] = (acc_ref[hh] / l_ref[hh]).T.astype(o_ref.dtype)


def _prompt_attention(proj3, slopes, *, n_heads, q_col, k_col, v_col):
    bsz, seq_len, _ = proj3.shape
    tq = MOBA_BLOCK
    nq = seq_len // tq
    nb = seq_len // MOBA_BLOCK
    hb = PROMPT_HEADS_PER_STEP
    gw = hb * HEAD_DIM
    assert n_heads % hb == 0 and seq_len % KV_TILE == 0
    qb0, kb0, vb0 = q_col // gw, k_col // gw, v_col // gw
    return pl.pallas_call(
        _prompt_attn_kernel,
        grid=(bsz, n_heads // hb, nq),
        in_specs=[
            pl.BlockSpec(memory_space=pltpu.SMEM),
            pl.BlockSpec((1, tq, gw), lambda b, h, i: (b, i, qb0 + h)),
            pl.BlockSpec((1, seq_len, gw), lambda b, h, i: (b, 0, kb0 + h)),
            pl.BlockSpec((1, seq_len, gw), lambda b, h, i: (b, 0, vb0 + h)),
        ],
        out_specs=pl.BlockSpec((tq, gw), lambda b, h, i: (b * nq + i, h)),
        out_shape=jax.ShapeDtypeStruct((bsz * seq_len, n_heads * HEAD_DIM), BF16),
        scratch_shapes=[
            pltpu.VMEM((hb, seq_len, HEAD_DIM), BF16),
            pltpu.VMEM((hb, seq_len // KV_TILE, HEAD_DIM, KV_TILE), BF16),
            pltpu.VMEM((hb, nb, HEAD_DIM), F32),
            pltpu.VMEM((hb, KV_TILE, tq), F32),
            pltpu.VMEM((hb, 1, tq), F32),
            pltpu.VMEM((hb, 1, tq), F32),
            pltpu.VMEM((hb, HEAD_DIM, tq), F32),
        ],
        compiler_params=_params("parallel", "parallel", "arbitrary"),
        name="prompt_attn",
    )(slopes, proj3, proj3, proj3)


def _heads_to_rows(x, n_heads, dh):
    return jnp.concatenate([x[:, h * dh:(h + 1) * dh] for h in range(n_heads)], axis=0)


def _sample_attn_kernel(pt_ref, slopes_ref, q_ref, kn_ref, vn_ref, *rest, past_len, n_pages):
    del pt_ref
    k_refs, v_refs, o_ref = rest[:n_pages], rest[n_pages:2 * n_pages], rest[2 * n_pages]
    t = q_ref.shape[1]
    page, n_heads, dh = k_refs[0].shape
    pages_per_block = MOBA_BLOCK // page
    nb = n_pages // pages_per_block
    rows = n_heads * t
    page_keys = page * n_heads

    q2 = _heads_to_rows(q_ref[0], n_heads, dh)
    q2b = (q2 * dh ** -0.5).astype(BF16)
    row_head = lax.broadcasted_iota(jnp.int32, (rows, 1), 0) // t
    row_query = lax.broadcasted_iota(jnp.int32, (rows, 1), 0) % t
    slope_rows = jnp.zeros((rows, 1), F32)
    for h in range(n_heads):
        slope_rows = jnp.where(row_head == h, slopes_ref[h], slope_rows)
    lane = lax.broadcasted_iota(jnp.int32, (rows, page_keys), 1)
    page_bias = jnp.where(lane % n_heads == row_head,
                          slope_rows * (row_query - lane // n_heads).astype(F32), -NEG_INF)

    route, scores = [], []
    for b in range(nb):
        ksum = jnp.zeros((n_heads, dh), F32)
        s_pages = []
        for pg in range(b * pages_per_block, (b + 1) * pages_per_block):
            kp = k_refs[pg][...]
            ksum = ksum + jnp.sum(kp, axis=0)
            s_pg = _dot_nt(q2b, kp.reshape(page_keys, dh).astype(BF16))
            s_pages.append(s_pg - page_bias - slope_rows * float(past_len - pg * page))
        kmean = ksum / MOBA_BLOCK
        kmean_rows = jnp.concatenate([jnp.broadcast_to(kmean[h:h + 1, :], (t, dh)) for h in range(n_heads)], axis=0)
        route.append(jnp.sum(q2 * kmean_rows, axis=-1, keepdims=True))
        scores.append(jnp.concatenate(s_pages, axis=-1))
    m_blk, l_blk, acc_blk = [], [], []
    for b in range(nb):
        pages = range(b * pages_per_block, (b + 1) * pages_per_block)
        vblk = jnp.concatenate([v_refs[pg][...].reshape(page_keys, dh) for pg in pages], axis=0)
        m_b = jnp.max(scores[b], axis=-1, keepdims=True)
        p = jnp.exp(scores[b] - m_b)
        m_blk.append(m_b)
        l_blk.append(jnp.sum(p, axis=-1, keepdims=True))
        acc_blk.append(_dot(p.astype(BF16), vblk.astype(BF16)))

    sel = _topk_select_list(route)
    kn2 = _heads_to_rows(kn_ref[0], n_heads, dh)
    vn2 = _heads_to_rows(vn_ref[0], n_heads, dh)
    col = lax.broadcasted_iota(jnp.int32, (rows, rows), 1)
    dist = row_query - col % t
    keep = (col // t == row_head) & (dist >= 0)
    s_own = _dot_nt(q2b, kn2.astype(BF16)) - slope_rows * dist.astype(F32)
    s_own = jnp.where(keep, s_own, NEG_INF)
    m_all = jnp.max(s_own, axis=-1, keepdims=True)
    for b in range(nb):
        m_all = jnp.maximum(m_all, jnp.where(sel[b], m_blk[b], NEG_INF))
    p_own = jnp.where(keep, jnp.exp(s_own - m_all), 0.0)
    den = jnp.sum(p_own, axis=-1, keepdims=True)
    num = _dot(p_own.astype(BF16), vn2.astype(BF16))
    for b in range(nb):
        w_b = jnp.where(sel[b], jnp.exp(m_blk[b] - m_all), 0.0)
        den = den + w_b * l_blk[b]
        num = num + w_b * acc_blk[b]
    out = num / den
    o_ref[...] = jnp.concatenate([out[h * t:(h + 1) * t, :] for h in range(n_heads)], axis=-1)


def _sample_attention(proj3, cache_k, cache_v, layer, page_table, slopes, *, q_col, k_col, v_col):
    n_seq, t, _ = proj3.shape
    _, _, page, n_heads, dh = cache_k.shape
    width = n_heads * dh
    n_pages = page_table.shape[1]
    assert MOBA_BLOCK % page == 0 and (n_pages * page) % MOBA_BLOCK == 0 and t <= MOBA_BLOCK
    past_len = n_pages * page
    qb0, kb0, vb0 = q_col // width, k_col // width, v_col // width
    page_specs = [pl.BlockSpec((None, None, page, n_heads, dh), lambda s, pt, pg=pg: (layer, pt[s, pg], 0, 0, 0))
                  for pg in range(n_pages)]
    grid_spec = pltpu.PrefetchScalarGridSpec(
        num_scalar_prefetch=1,
        grid=(n_seq,),
        in_specs=[
            pl.BlockSpec(memory_space=pltpu.SMEM),
            pl.BlockSpec((1, t, width), lambda s, pt: (s, 0, qb0)),
            pl.BlockSpec((1, t, width), lambda s, pt: (s, 0, kb0)),
            pl.BlockSpec((1, t, width), lambda s, pt: (s, 0, vb0)),
            *page_specs, *page_specs,
        ],
        out_specs=pl.BlockSpec((t, width), lambda s, pt: (s, 0)),
    )
    return pl.pallas_call(
        functools.partial(_sample_attn_kernel, past_len=past_len, n_pages=n_pages),
        grid_spec=grid_spec,
        out_shape=jax.ShapeDtypeStruct((n_seq * t, width), F32),
        compiler_params=_params("parallel"),
        name="sample_attn",
    )(page_table, slopes, proj3, proj3, proj3, *([cache_k] * n_pages), *([cache_v] * n_pages))


def _merge_kernel(a_ref, b_ref, wp_ref, wa_ref, ga_ref, gb_ref, o_ref):
    pa = _dot(a_ref[...].astype(BF16), wp_ref[...])
    pb = _dot(b_ref[...].astype(BF16), wa_ref[...])
    merged = jax.nn.sigmoid(ga_ref[...]) * pa + jax.nn.sigmoid(gb_ref[...]) * pb
    o_ref[...] = merged.astype(o_ref.dtype)


def _merge(a_out, b_out, w_pool_up, w_att_up, proj, *, ga_col, gb_col, tm, tn):
    rows, wa = a_out.shape
    wb = b_out.shape[1]
    n = w_pool_up.shape[1]
    ga0, gb0 = ga_col // tn, gb_col // tn
    return pl.pallas_call(
        _merge_kernel,
        grid=(rows // tm, n // tn),
        in_specs=[
            pl.BlockSpec((tm, wa), lambda i, j: (i, 0)),
            pl.BlockSpec((tm, wb), lambda i, j: (i, 0)),
            pl.BlockSpec((wa, tn), lambda i, j: (0, j)),
            pl.BlockSpec((wb, tn), lambda i, j: (0, j)),
            pl.BlockSpec((tm, tn), lambda i, j: (i, ga0 + j)),
            pl.BlockSpec((tm, tn), lambda i, j: (i, gb0 + j)),
        ],
        out_specs=pl.BlockSpec((tm, tn), lambda i, j: (i, j)),
        out_shape=jax.ShapeDtypeStruct((rows, n), BF16),
        compiler_params=_params("parallel", "arbitrary"),
        name="merge",
    )(a_out, b_out, w_pool_up, w_att_up, proj, proj)


def _matmul_residual_kernel(a_ref, w_ref, r_ref, o_ref):
    o_ref[...] = r_ref[...] + _dot(a_ref[...], w_ref[...])


def _matmul_residual(a, w, res, *, tm, tn):
    rows, k = a.shape
    n = w.shape[1]
    return pl.pallas_call(
        _matmul_residual_kernel,
        grid=(rows // tm, n // tn),
        in_specs=[
            pl.BlockSpec((tm, k), lambda i, j: (i, 0)),
            pl.BlockSpec((k, tn), lambda i, j: (0, j)),
            pl.BlockSpec((tm, tn), lambda i, j: (i, j)),
        ],
        out_specs=pl.BlockSpec((tm, tn), lambda i, j: (i, j)),
        out_shape=jax.ShapeDtypeStruct((rows, n), F32),
        compiler_params=_params("parallel", "arbitrary"),
        name="out_proj",
    )(a, w, res)


FFN_COL_CHUNKS = 2


def _gelu_exact(x):
    return 0.5 * x * (1.0 + lax.erf(x * (2.0 ** -0.5)))


def _ffn_kernel(h_ref, g2_ref, wa_ref, wg_ref, cw_ref, cb_ref, wd_ref, gf_ref, *rest,
                seq_rows, tiles_per_seq, per_seq_hist):
    hist_ref = rest[0] if per_seq_hist else None
    y_ref, tail_ref, hn_ref, carry_ref = rest[-4:]
    i = pl.program_id(0)
    j = pl.program_id(1)
    tm = h_ref.shape[0]
    tn = wa_ref.shape[1]
    sub = tn // FFN_COL_CHUNKS

    @pl.when(j == 0)
    def _():
        h = h_ref[...]
        hn_ref[...] = _rmsnorm(h, g2_ref[...]).astype(BF16)
        y_ref[...] = h

    if not per_seq_hist:
        @pl.when(i % tiles_per_seq == 0)
        def _():
            carry_ref[j] = jnp.zeros((SUBLANES, tn), F32)

    hn = hn_ref[...]
    chunks = [slice(c * sub, (c + 1) * sub) for c in range(FFN_COL_CHUNKS)]
    a_parts = [_dot(hn, wa_ref[:, cols]) for cols in chunks]
    g_parts = [_dot(hn, wg_ref[:, cols]) for cols in chunks]
    row = lax.broadcasted_iota(jnp.int32, (tm, sub), 0)
    down = None
    for cols, a, gv in zip(chunks, a_parts, g_parts):
        if per_seq_hist:
            n_seq = tm // seq_rows
            hist = hist_ref[:, :, cols]
            prev = [jnp.broadcast_to(hist[:, r:r + 1, :], (n_seq, seq_rows, sub)).reshape(tm, sub)
                    for r in range(CONV_W - 1)]
            t_in_seq = row % seq_rows
            tail_ref[:, cols] = a
        else:
            carry = carry_ref[j, :, cols]
            first = SUBLANES - (CONV_W - 1)
            prev = [jnp.broadcast_to(carry[first + r:first + r + 1, :], (tm, sub)) for r in range(CONV_W - 1)]
            t_in_seq = row
            tail = a[tm - SUBLANES:, :]
            carry_ref[j, :, cols] = tail
            tail_ref[0, :, cols] = tail
        ac = cb_ref[:, cols] + cw_ref[CONV_W - 1:CONV_W, cols] * a
        for back in range(1, CONV_W):
            shifted = pltpu.roll(a, back, axis=0)
            for r in range(back):
                shifted = jnp.where(t_in_seq == r, prev[CONV_W - 1 - back + r], shifted)
            ac = ac + cw_ref[CONV_W - 1 - back:CONV_W - back, cols] * shifted
        act = (_gelu_exact(ac) * gv).astype(BF16)
        part = _dot(act, wd_ref[cols, :])
        down = part if down is None else down + part
    y_ref[...] += down

    @pl.when(j == pl.num_programs(1) - 1)
    def _():
        y_ref[...] = _rmsnorm(y_ref[...], gf_ref[...])


def _ffn(h, g2, w_up, conv_w, conv_b, w_down, gf, hist, *, seq_rows, tm, tn):
    rows, d = h.shape
    d_ff = w_down.shape[0]
    n_col = d_ff // tn
    n_row = rows // tm
    per_seq_hist = hist is not None
    if per_seq_hist:
        assert tm % seq_rows == 0
        tiles_per_seq = 1
        tail_shape = jax.ShapeDtypeStruct((rows, d_ff), F32)
        tail_spec = pl.BlockSpec((tm, tn), lambda i, j: (i, j))
        hist_args = (hist,)
        hist_specs = [pl.BlockSpec((tm // seq_rows, CONV_W - 1, tn), lambda i, j: (i, 0, j))]
    else:
        assert seq_rows % tm == 0
        tiles_per_seq = seq_rows // tm
        tail_shape = jax.ShapeDtypeStruct((n_row, SUBLANES, d_ff), F32)
        tail_spec = pl.BlockSpec((1, SUBLANES, tn), lambda i, j: (i, 0, j))
        hist_args = ()
        hist_specs = []
    y, tail = pl.pallas_call(
        functools.partial(_ffn_kernel, seq_rows=seq_rows, tiles_per_seq=tiles_per_seq, per_seq_hist=per_seq_hist),
        grid=(n_row, n_col),
        in_specs=[
            pl.BlockSpec((tm, d), lambda i, j: (i, 0), pipeline_mode=pl.Buffered(1)),
            pl.BlockSpec((1, d), lambda i, j: (0, 0)),
            pl.BlockSpec((d, tn), lambda i, j: (0, j)),
            pl.BlockSpec((d, tn), lambda i, j: (0, n_col + j)),
            pl.BlockSpec((CONV_W, tn), lambda i, j: (0, j)),
            pl.BlockSpec((1, tn), lambda i, j: (0, j)),
            pl.BlockSpec((tn, d), lambda i, j: (j, 0)),
            pl.BlockSpec((1, d), lambda i, j: (0, 0)),
            *hist_specs,
        ],
        out_specs=[pl.BlockSpec((tm, d), lambda i, j: (i, 0)), tail_spec],
        out_shape=[jax.ShapeDtypeStruct((rows, d), F32), tail_shape],
        scratch_shapes=[
            pltpu.VMEM((tm, d), BF16),
            pltpu.VMEM((n_col, SUBLANES, tn), F32),
        ],
        compiler_params=_params("arbitrary", "arbitrary"),
        name="ffn",
    )(h, g2, w_up, w_up, conv_w, conv_b, w_down, gf, *hist_args)
    return y, tail


def _layer(x3, pool_hist, conv_hist, attend, pos0, w, *, tm):
    n_seq, seq_len, d = x3.shape
    rows = n_seq * seq_len
    x = x3.reshape(rows, d)
    w_pool = w["pool_scale"].shape[1]
    w_att = w["w_att_up"].shape[0]
    proj = _norm_matmul(x, w["g1"], w["w_in"], tm=tm, tn=1024)
    proj3 = proj.reshape(n_seq, seq_len, -1)
    q_col, k_col, v_col = w_pool, w_pool + w_att, w_pool + 2 * w_att
    ga_col, gb_col = w_pool + 3 * w_att, w_pool + 3 * w_att + d
    if pool_hist is None:
        a_out = _pool_mix(proj3, None, w["pool_maps"], w["pool_scale"], s_blk=1, l_blk=POOL_TILE_ROWS, pos0=pos0,
                          out_dtype=BF16)
    else:
        a_out = _pool_mix(proj3, pool_hist, w["pool_maps"], w["pool_scale"], s_blk=32, l_blk=seq_len, pos0=pos0,
                          out_dtype=F32)
    b_out = attend(proj3, q_col, k_col, v_col)
    merged = _merge(a_out, b_out, w["w_pool_up"], w["w_att_up"], proj, ga_col=ga_col, gb_col=gb_col, tm=tm, tn=1024)
    h = _matmul_residual(merged, w["w_out"], x, tm=tm, tn=1024)
    y, tail = _ffn(h, w["g2"], w["w_up"], w["conv_w"], w["conv_b"], w["w_down"], w["gf"], conv_hist,
                   seq_rows=seq_len, tm=tm, tn=512)
    return y.reshape(n_seq, seq_len, d), proj3, tail


def kernel(x_prompt, x_sample, cache_k, cache_v, state_pool, state_conv, page_table, norm1_g, w_in, pool_maps,
           pool_scale, w_pool_up, w_att_up, w_out, norm2_g, w_up, conv_w, conv_b, w_down, final_norm_g):
    depth = w_in.shape[0]
    assert depth == 1, "single-layer trunk"
    bsz, seq, d = x_prompt.shape
    n_dec, dec_seq, _ = x_sample.shape
    _, n_pool, page, n_heads, dh = cache_k.shape
    assert dh == HEAD_DIM
    w_att = n_heads * dh
    w_pool = pool_scale.shape[1]
    d_ff = w_down.shape[1]
    past_len = page_table.shape[1] * page
    slopes = 2.0 ** (-8.0 * jnp.arange(1, n_heads + 1, dtype=F32) / n_heads)

    l = 0
    w = dict(
        g1=norm1_g[l][None, :], w_in=w_in[l].astype(BF16), pool_maps=pool_maps[l].astype(BF16),
        pool_scale=pool_scale[l][None, :], w_pool_up=w_pool_up[l].astype(BF16), w_att_up=w_att_up[l].astype(BF16),
        w_out=w_out[l].astype(BF16), g2=norm2_g[l][None, :], w_up=w_up[l].astype(BF16), conv_w=conv_w[l],
        conv_b=conv_b[l][None, :], w_down=w_down[l].astype(BF16), gf=final_norm_g[None, :],
    )
    tm = ROW_TILE

    attend_p = lambda proj3, qc, kc, vc: _prompt_attention(proj3, slopes, n_heads=n_heads, q_col=qc, k_col=kc, v_col=vc)
    y_p, proj_p, tail_p = _layer(x_prompt, None, None, attend_p, 0, w, tm=tm)

    attend_s = lambda proj3, qc, kc, vc: _sample_attention(proj3, cache_k, cache_v, l, page_table, slopes,
                                                           q_col=qc, k_col=kc, v_col=vc)
    y_s, proj_s, tail_s = _layer(x_sample, state_pool[l], state_conv[l], attend_s, past_len, w, tm=tm)

    heads = lambda p3, col: p3[:, :, col:col + w_att].reshape(p3.shape[0], p3.shape[1], n_heads, dh)[None]
    k_p, v_p = heads(proj_p, w_pool + w_att), heads(proj_p, w_pool + 2 * w_att)
    k_s, v_s = heads(proj_s, w_pool + w_att), heads(proj_s, w_pool + 2 * w_att)
    pool_p = proj_p[:, seq - POOL_HIST:, :w_pool][None]
    pool_s = jnp.concatenate([state_pool[l], proj_s[:, :, :w_pool]], axis=1)[:, -POOL_HIST:][None]
    tiles_per_seq = seq // tm
    conv_p = tail_p.reshape(bsz, tiles_per_seq, SUBLANES, d_ff)[:, -1, SUBLANES - (CONV_W - 1):][None]
    conv_s = jnp.concatenate([state_conv[l], tail_s.reshape(n_dec, dec_seq, d_ff)], axis=1)[:, -(CONV_W - 1):][None]
    return (y_p, y_s, k_p, v_p, k_s, v_s, pool_p, pool_s, conv_p, conv_s)
```

```python
import functools

import jax
import jax.numpy as jnp
from jax import lax
from jax.experimental import pallas as pl
from jax.experimental.pallas import tpu as pltpu

POOL_WINDOWS = (2, 4, 8, 16)
POOL_HIST = max(POOL_WINDOWS) - 1
HEAD_DIM = 128
MOBA_BLOCK = 256
MOBA_TOPK = 3
CONV_W = 3
RMS_EPS = 1e-6
NEG_INF = -1e30

SUBLANES = 8
HALO_ROWS = 16
ROW_TILE = 1024
POOL_TILE_ROWS = 512
V7X_VMEM_BYTES = 64 * 1024 * 1024
VMEM_LIMIT_BYTES = 56 * 1024 * 1024

BF16 = jnp.bfloat16
F32 = jnp.float32


def _params(*semantics):
    return pltpu.CompilerParams(dimension_semantics=semantics, vmem_limit_bytes=VMEM_LIMIT_BYTES)


def _rmsnorm(x, g):
    return x * lax.rsqrt(jnp.mean(x * x, axis=-1, keepdims=True) + RMS_EPS) * g


def _dot(a, b):
    return jnp.dot(a, b, preferred_element_type=F32)


def _dot_nt(a, b):
    return lax.dot_general(a, b, (((1,), (1,)), ((), ())), preferred_element_type=F32)


def _split_bf16(x):
    hi = x.astype(BF16)
    lo = (x - hi.astype(F32)).astype(BF16)
    return hi, lo


def _dot_nt_3pass(a, b):
    a_hi, a_lo = _split_bf16(a)
    b_hi, b_lo = _split_bf16(b)
    return _dot_nt(a_hi, b_hi) + (_dot_nt(a_hi, b_lo) + _dot_nt(a_lo, b_hi))


def _norm_matmul_kernel(x_ref, g_ref, w_ref, o_ref, xn_ref):
    @pl.when(pl.program_id(1) == 0)
    def _():
        xn_ref[...] = _rmsnorm(x_ref[...], g_ref[...]).astype(BF16)

    o_ref[...] = _dot(xn_ref[...], w_ref[...])


def _norm_matmul(x, g, w, *, tm, tn):
    rows, d = x.shape
    n = w.shape[1]
    return pl.pallas_call(
        _norm_matmul_kernel,
        grid=(rows // tm, n // tn),
        in_specs=[
            pl.BlockSpec((tm, d), lambda i, j: (i, 0)),
            pl.BlockSpec((1, d), lambda i, j: (0, 0)),
            pl.BlockSpec((d, tn), lambda i, j: (0, j)),
        ],
        out_specs=pl.BlockSpec((tm, tn), lambda i, j: (i, j)),
        out_shape=jax.ShapeDtypeStruct((rows, n), F32),
        scratch_shapes=[pltpu.VMEM((tm, d), BF16)],
        compiler_params=_params("parallel", "arbitrary"),
        name="in_proj",
    )(x, g, w)


def _pool_kernel(u_ref, hist_ref, maps_ref, scale_ref, o_ref, ext_ref, *, pos0, zero_first_hist):
    s_blk, l_blk, width = u_ref.shape
    group = width // len(POOL_WINDOWS)
    tile = pl.program_id(1)
    u = u_ref[...]
    ext_ref[:, HALO_ROWS:, :] = u
    hist_rows = hist_ref.shape[1]
    ext_ref[:, HALO_ROWS - hist_rows:HALO_ROWS, :] = hist_ref[...]
    if zero_first_hist:
        @pl.when(tile == 0)
        def _():
            ext_ref[:, 0:HALO_ROWS, :] = jnp.zeros((s_blk, HALO_ROWS, width), F32)

    pos = pos0 + tile * l_blk + lax.broadcasted_iota(jnp.int32, (s_blk, l_blk, group), 1)
    outs = []
    for gi, w in enumerate(POOL_WINDOWS):
        cols = slice(gi * group, (gi + 1) * group)
        win = u[:, :, cols]
        for k in range(1, w):
            win = win + ext_ref[:, HALO_ROWS - k:HALO_ROWS - k + l_blk, cols]
        cnt = jnp.minimum(w, pos + 1).astype(F32)
        d = win / cnt - u[:, :, cols]
        d2 = d.reshape(s_blk * l_blk, group).astype(BF16)
        outs.append(_dot(d2, maps_ref[gi]))
    m = jnp.concatenate(outs, axis=-1) * scale_ref[...]
    o_ref[...] = m.astype(o_ref.dtype)


def _pool_mix(proj3, hist, maps, scale, *, s_blk, l_blk, pos0, out_dtype):
    n_seq, seq_len, _ = proj3.shape
    n_groups, group, _ = maps.shape
    width = n_groups * group
    tiles = seq_len // l_blk
    if hist is None:
        halo_blocks = l_blk // HALO_ROWS
        hist_arr = proj3
        hist_spec = pl.BlockSpec((s_blk, HALO_ROWS, width),
                                 lambda s, t: (s, jnp.maximum(t * halo_blocks - 1, 0), 0))
    else:
        assert tiles == 1
        hist_arr = hist
        hist_spec = pl.BlockSpec((s_blk, hist.shape[1], width), lambda s, t: (s, 0, 0))
    kern = functools.partial(_pool_kernel, pos0=pos0, zero_first_hist=hist is None)
    return pl.pallas_call(
        kern,
        grid=(n_seq // s_blk, tiles),
        in_specs=[
            pl.BlockSpec((s_blk, l_blk, width), lambda s, t: (s, t, 0)),
            hist_spec,
            pl.BlockSpec((n_groups, group, group), lambda s, t: (0, 0, 0)),
            pl.BlockSpec((1, width), lambda s, t: (0, 0)),
        ],
        out_specs=pl.BlockSpec((s_blk * l_blk, width), lambda s, t: (s * tiles + t, 0)),
        out_shape=jax.ShapeDtypeStruct((n_seq * seq_len, width), out_dtype),
        scratch_shapes=[pltpu.VMEM((s_blk, HALO_ROWS + l_blk, width), F32)],
        compiler_params=_params("parallel", "arbitrary"),
        name="pool_mix",
    )(proj3, hist_arr, maps, scale)


def _topk_select(route, n_valid):
    nb, n = route.shape
    blk = lax.broadcasted_iota(jnp.int32, (nb, n), 0)
    rank = jnp.zeros((nb, n), jnp.int32)
    for other in range(nb):
        c = route[other:other + 1, :]
        beats = jnp.where((c > route) | ((c == route) & (other < blk)), 1, 0)
        rank = rank + beats * (other < n_valid).astype(jnp.int32)
    return (blk < n_valid) & (rank < MOBA_TOPK)


def _topk_select_list(scores):
    sel = []
    for j, s_j in enumerate(scores):
        rank = jnp.zeros(s_j.shape, jnp.int32)
        for other, s_o in enumerate(scores):
            if other < j:
                rank = rank + jnp.where(s_o >= s_j, 1, 0)
            elif other > j:
                rank = rank + jnp.where(s_o > s_j, 1, 0)
        sel.append(rank < MOBA_TOPK)
    return sel


KV_TILE = 2 * MOBA_BLOCK
assert KV_TILE == 2 * MOBA_BLOCK
M_FLOOR = -1e20
PROMPT_HEADS_PER_STEP = 4


def _prompt_attn_kernel(slopes_ref, q_ref, k_ref, v_ref, o_ref,
                        kb_ref, vt_ref, kmean_ref, alibi_ref, m_ref, l_ref, acc_ref):
    hgroup = pl.program_id(1)
    qt = pl.program_id(2)
    n_hb, seq_len, dh = kb_ref.shape
    tq = q_ref.shape[1]
    nb = seq_len // MOBA_BLOCK
    blocks_per_tile = KV_TILE // MOBA_BLOCK
    slopes = [slopes_ref[hgroup * n_hb + hh] for hh in range(n_hb)]
    q_minus_k = (lax.broadcasted_iota(jnp.int32, (KV_TILE, tq), 1)
                 - lax.broadcasted_iota(jnp.int32, (KV_TILE, tq), 0))

    @pl.when(qt == 0)
    def _():
        for hh in range(n_hb):
            cols = slice(hh * dh, (hh + 1) * dh)
            k = k_ref[0, :, cols]
            kb_ref[hh] = k.astype(BF16)
            kmean_ref[hh] = jnp.mean(k.reshape(nb, MOBA_BLOCK, dh), axis=1)
            for c in range(nb):
                vt_ref[hh, c] = v_ref[0, c * MOBA_BLOCK:(c + 1) * MOBA_BLOCK, cols].T.astype(BF16)
            alibi_ref[hh] = slopes[hh] * q_minus_k.astype(F32)

    blk = lax.broadcasted_iota(jnp.int32, (nb, tq), 0)
    qbs, sel_bits = [], []
    for hh in range(n_hb):
        q = q_ref[0, :, hh * dh:(hh + 1) * dh]
        qbs.append((q * dh ** -0.5).astype(BF16))
        route = _dot_nt_3pass(kmean_ref[hh], q)
        sel = _topk_select(route, qt)
        bits = jnp.sum(jnp.where(sel, jnp.left_shift(1, blk).astype(F32), 0.0), axis=0, keepdims=True)
        sel_bits.append(bits.astype(jnp.int32))

    m_ref[...] = jnp.full(m_ref.shape, M_FLOOR, F32)
    l_ref[...] = jnp.zeros(l_ref.shape, F32)
    acc_ref[...] = jnp.zeros(acc_ref.shape, F32)

    def block_bias(hh, j):
        bit = jnp.right_shift(sel_bits[hh], j) & 1
        return jnp.broadcast_to(jnp.where(bit == 1, 0.0, NEG_INF), (MOBA_BLOCK, tq))

    def attend(first_blk, n_blk, bias_fn, offset):
        keys = n_blk * MOBA_BLOCK
        start = pl.multiple_of(first_blk * MOBA_BLOCK, MOBA_BLOCK)
        scores = [_dot_nt(kb_ref[hh, pl.ds(start, keys), :], qbs[hh]) for hh in range(n_hb)]
        for hh in range(n_hb):
            z = scores[hh] - alibi_ref[hh, 0:keys, :] + bias_fn(hh)
            shift = slopes[hh] * jnp.asarray(offset).astype(F32)
            m_old = m_ref[hh]
            m_new = jnp.maximum(m_old, jnp.max(z, axis=0, keepdims=True) - shift)
            alpha = jnp.exp(m_old - m_new)
            pr = jnp.exp(z - (m_new + shift))
            l_ref[hh] = alpha * l_ref[hh] + jnp.sum(pr, axis=0, keepdims=True)
            pv = None
            for b in range(n_blk):
                part = _dot(vt_ref[hh, first_blk + b], pr[b * MOBA_BLOCK:(b + 1) * MOBA_BLOCK].astype(BF16))
                pv = part if pv is None else pv + part
            acc_ref[hh] = alpha * acc_ref[hh] + pv
            m_ref[hh] = m_new

    def past_pair(p, carry):
        first = p * blocks_per_tile
        attend(first, blocks_per_tile,
               lambda hh: jnp.concatenate([block_bias(hh, first + b) for b in range(blocks_per_tile)], axis=0),
               (qt - first) * MOBA_BLOCK)
        return carry

    lax.fori_loop(0, qt // blocks_per_tile, past_pair, 0)

    @pl.when(qt % blocks_per_tile == 1)
    def _():
        attend(qt - 1, 1, lambda hh: block_bias(hh, qt - 1), MOBA_BLOCK)

    causal = jnp.where(q_minus_k[0:MOBA_BLOCK, :] >= 0, 0.0, NEG_INF)
    attend(qt, 1, lambda hh: causal, 0)
    for hh in range(n_hb):
        o_ref[:, hh * dh:(hh + 1) * dh] = (acc_ref[hh] / l_ref[hh]).T.astype(o_ref.dtype)


def _prompt_attention(proj3, slopes, *, n_heads, q_col, k_col, v_col):
    bsz, seq_len, _ = proj3.shape
    tq = MOBA_BLOCK
    nq = seq_len // tq
    nb = seq_len // MOBA_BLOCK
    hb = PROMPT_HEADS_PER_STEP
    gw = hb * HEAD_DIM
    assert n_heads % hb == 0 and seq_len % KV_TILE == 0
    qb0, kb0, vb0 = q_col // gw, k_col // gw, v_col // gw
    return pl.pallas_call(
        _prompt_attn_kernel,
        grid=(bsz, n_heads // hb, nq),
        in_specs=[
            pl.BlockSpec(memory_space=pltpu.SMEM),
            pl.BlockSpec((1, tq, gw), lambda b, h, i: (b, i, qb0 + h)),
            pl.BlockSpec((1, seq_len, gw), lambda b, h, i: (b, 0, kb0 + h)),
            pl.BlockSpec((1, seq_len, gw), lambda b, h, i: (b, 0, vb0 + h)),
        ],
        out_specs=pl.BlockSpec((tq, gw), lambda b, h, i: (b * nq + i, h)),
        out_shape=jax.ShapeDtypeStruct((bsz * seq_len, n_heads * HEAD_DIM), BF16),
        scratch_shapes=[
            pltpu.VMEM((hb, seq_len, HEAD_DIM), BF16),
            pltpu.VMEM((hb, nb, HEAD_DIM, MOBA_BLOCK), BF16),
            pltpu.VMEM((hb, nb, HEAD_DIM), F32),
            pltpu.VMEM((hb, KV_TILE, tq), F32),
            pltpu.VMEM((hb, 1, tq), F32),
            pltpu.VMEM((hb, 1, tq), F32),
            pltpu.VMEM((hb, HEAD_DIM, tq), F32),
        ],
        compiler_params=_params("parallel", "parallel", "arbitrary"),
        name="prompt_attn",
    )(slopes, proj3, proj3, proj3)


def _heads_to_rows(x, n_heads, dh):
    return jnp.concatenate([x[:, h * dh:(h + 1) * dh] for h in range(n_heads)], axis=0)


def _sample_attn_kernel(pt_ref, slopes_ref, q_ref, kn_ref, vn_ref, *rest, past_len, n_pages):
    del pt_ref
    k_refs, v_refs, o_ref = rest[:n_pages], rest[n_pages:2 * n_pages], rest[2 * n_pages]
    t = q_ref.shape[1]
    page, n_heads, dh = k_refs[0].shape
    pages_per_block = MOBA_BLOCK // page
    nb = n_pages // pages_per_block
    rows = n_heads * t
    page_keys = page * n_heads

    q2 = _heads_to_rows(q_ref[0], n_heads, dh)
    q2b = (q2 * dh ** -0.5).astype(BF16)
    row_head = lax.broadcasted_iota(jnp.int32, (rows, 1), 0) // t
    row_query = lax.broadcasted_iota(jnp.int32, (rows, 1), 0) % t
    slope_rows = jnp.zeros((rows, 1), F32)
    for h in range(n_heads):
        slope_rows = jnp.where(row_head == h, slopes_ref[h], slope_rows)
    lane = lax.broadcasted_iota(jnp.int32, (rows, page_keys), 1)
    page_bias = jnp.where(lane % n_heads == row_head,
                          slope_rows * (row_query - lane // n_heads).astype(F32), -NEG_INF)

    route, scores = [], []
    for b in range(nb):
        ksum = jnp.zeros((n_heads, dh), F32)
        s_pages = []
        for pg in range(b * pages_per_block, (b + 1) * pages_per_block):
            kp = k_refs[pg][...]
            ksum = ksum + jnp.sum(kp, axis=0)
            s_pg = _dot_nt(q2b, kp.reshape(page_keys, dh).astype(BF16))
            s_pages.append(s_pg - page_bias - slope_rows * float(past_len - pg * page))
        kmean = ksum / MOBA_BLOCK
        kmean_rows = jnp.concatenate([jnp.broadcast_to(kmean[h:h + 1, :], (t, dh)) for h in range(n_heads)], axis=0)
        route.append(jnp.sum(q2 * kmean_rows, axis=-1, keepdims=True))
        scores.append(jnp.concatenate(s_pages, axis=-1))
    m_blk, l_blk, acc_blk = [], [], []
    for b in range(nb):
        pages = range(b * pages_per_block, (b + 1) * pages_per_block)
        vblk = jnp.concatenate([v_refs[pg][...].reshape(page_keys, dh) for pg in pages], axis=0)
        m_b = jnp.max(scores[b], axis=-1, keepdims=True)
        p = jnp.exp(scores[b] - m_b)
        m_blk.append(m_b)
        l_blk.append(jnp.sum(p, axis=-1, keepdims=True))
        acc_blk.append(_dot(p.astype(BF16), vblk.astype(BF16)))

    sel = _topk_select_list(route)
    kn2 = _heads_to_rows(kn_ref[0], n_heads, dh)
    vn2 = _heads_to_rows(vn_ref[0], n_heads, dh)
    col = lax.broadcasted_iota(jnp.int32, (rows, rows), 1)
    dist = row_query - col % t
    keep = (col // t == row_head) & (dist >= 0)
    s_own = _dot_nt(q2b, kn2.astype(BF16)) - slope_rows * dist.astype(F32)
    s_own = jnp.where(keep, s_own, NEG_INF)
    m_all = jnp.max(s_own, axis=-1, keepdims=True)
    for b in range(nb):
        m_all = jnp.maximum(m_all, jnp.where(sel[b], m_blk[b], NEG_INF))
    p_own = jnp.where(keep, jnp.exp(s_own - m_all), 0.0)
    den = jnp.sum(p_own, axis=-1, keepdims=True)
    num = _dot(p_own.astype(BF16), vn2.astype(BF16))
    for b in range(nb):
        w_b = jnp.where(sel[b], jnp.exp(m_blk[b] - m_all), 0.0)
        den = den + w_b * l_blk[b]
        num = num + w_b * acc_blk[b]
    out = num / den
    o_ref[...] = jnp.concatenate([out[h * t:(h + 1) * t, :] for h in range(n_heads)], axis=-1)


def _sample_attention(proj3, cache_k, cache_v, layer, page_table, slopes, *, q_col, k_col, v_col):
    n_seq, t, _ = proj3.shape
    _, _, page, n_heads, dh = cache_k.shape
    width = n_heads * dh
    n_pages = page_table.shape[1]
    assert MOBA_BLOCK % page == 0 and (n_pages * page) % MOBA_BLOCK == 0 and t <= MOBA_BLOCK
    past_len = n_pages * page
    qb0, kb0, vb0 = q_col // width, k_col // width, v_col // width
    page_specs = [pl.BlockSpec((None, None, page, n_heads, dh), lambda s, pt, pg=pg: (layer, pt[s, pg], 0, 0, 0))
                  for pg in range(n_pages)]
    grid_spec = pltpu.PrefetchScalarGridSpec(
        num_scalar_prefetch=1,
        grid=(n_seq,),
        in_specs=[
            pl.BlockSpec(memory_space=pltpu.SMEM),
            pl.BlockSpec((1, t, width), lambda s, pt: (s, 0, qb0)),
            pl.BlockSpec((1, t, width), lambda s, pt: (s, 0, kb0)),
            pl.BlockSpec((1, t, width), lambda s, pt: (s, 0, vb0)),
            *page_specs, *page_specs,
        ],
        out_specs=pl.BlockSpec((t, width), lambda s, pt: (s, 0)),
    )
    return pl.pallas_call(
        functools.partial(_sample_attn_kernel, past_len=past_len, n_pages=n_pages),
        grid_spec=grid_spec,
        out_shape=jax.ShapeDtypeStruct((n_seq * t, width), F32),
        compiler_params=_params("parallel"),
        name="sample_attn",
    )(page_table, slopes, proj3, proj3, proj3, *([cache_k] * n_pages), *([cache_v] * n_pages))


def _merge_kernel(a_ref, b_ref, wp_ref, wa_ref, ga_ref, gb_ref, o_ref):
    pa = _dot(a_ref[...].astype(BF16), wp_ref[...])
    pb = _dot(b_ref[...].astype(BF16), wa_ref[...])
    merged = jax.nn.sigmoid(ga_ref[...]) * pa + jax.nn.sigmoid(gb_ref[...]) * pb
    o_ref[...] = merged.astype(o_ref.dtype)


def _merge(a_out, b_out, w_pool_up, w_att_up, proj, *, ga_col, gb_col, tm, tn):
    rows, wa = a_out.shape
    wb = b_out.shape[1]
    n = w_pool_up.shape[1]
    ga0, gb0 = ga_col // tn, gb_col // tn
    return pl.pallas_call(
        _merge_kernel,
        grid=(rows // tm, n // tn),
        in_specs=[
            pl.BlockSpec((tm, wa), lambda i, j: (i, 0)),
            pl.BlockSpec((tm, wb), lambda i, j: (i, 0)),
            pl.BlockSpec((wa, tn), lambda i, j: (0, j)),
            pl.BlockSpec((wb, tn), lambda i, j: (0, j)),
            pl.BlockSpec((tm, tn), lambda i, j: (i, ga0 + j)),
            pl.BlockSpec((tm, tn), lambda i, j: (i, gb0 + j)),
        ],
        out_specs=pl.BlockSpec((tm, tn), lambda i, j: (i, j)),
        out_shape=jax.ShapeDtypeStruct((rows, n), BF16),
        compiler_params=_params("parallel", "arbitrary"),
        name="merge",
    )(a_out, b_out, w_pool_up, w_att_up, proj, proj)


def _matmul_residual_kernel(a_ref, w_ref, r_ref, o_ref):
    o_ref[...] = r_ref[...] + _dot(a_ref[...], w_ref[...])


def _matmul_residual(a, w, res, *, tm, tn):
    rows, k = a.shape
    n = w.shape[1]
    return pl.pallas_call(
        _matmul_residual_kernel,
        grid=(rows // tm, n // tn),
        in_specs=[
            pl.BlockSpec((tm, k), lambda i, j: (i, 0)),
            pl.BlockSpec((k, tn), lambda i, j: (0, j)),
            pl.BlockSpec((tm, tn), lambda i, j: (i, j)),
        ],
        out_specs=pl.BlockSpec((tm, tn), lambda i, j: (i, j)),
        out_shape=jax.ShapeDtypeStruct((rows, n), F32),
        compiler_params=_params("parallel", "arbitrary"),
        name="out_proj",
    )(a, w, res)


FFN_COL_CHUNKS = 2


def _gelu_exact(x):
    return 0.5 * x * (1.0 + lax.erf(x * (2.0 ** -0.5)))


def _ffn_kernel(h_ref, g2_ref, wa_ref, wg_ref, cw_ref, cb_ref, wd_ref, gf_ref, *rest,
                seq_rows, tiles_per_seq, per_seq_hist):
    hist_ref = rest[0] if per_seq_hist else None
    y_ref, tail_ref, hn_ref, carry_ref = rest[-4:]
    i = pl.program_id(0)
    j = pl.program_id(1)
    tm = h_ref.shape[0]
    tn = wa_ref.shape[1]
    sub = tn // FFN_COL_CHUNKS

    @pl.when(j == 0)
    def _():
        h = h_ref[...]
        hn_ref[...] = _rmsnorm(h, g2_ref[...]).astype(BF16)
        y_ref[...] = h

    if not per_seq_hist:
        @pl.when(i % tiles_per_seq == 0)
        def _():
            carry_ref[j] = jnp.zeros((SUBLANES, tn), F32)

    hn = hn_ref[...]
    chunks = [slice(c * sub, (c + 1) * sub) for c in range(FFN_COL_CHUNKS)]
    a_parts = [_dot(hn, wa_ref[:, cols]) for cols in chunks]
    g_parts = [_dot(hn, wg_ref[:, cols]) for cols in chunks]
    row = lax.broadcasted_iota(jnp.int32, (tm, sub), 0)
    down = None
    for cols, a, gv in zip(chunks, a_parts, g_parts):
        if per_seq_hist:
            n_seq = tm // seq_rows
            hist = hist_ref[:, :, cols]
            prev = [jnp.broadcast_to(hist[:, r:r + 1, :], (n_seq, seq_rows, sub)).reshape(tm, sub)
                    for r in range(CONV_W - 1)]
            t_in_seq = row % seq_rows
            tail_ref[:, cols] = a
        else:
            carry = carry_ref[j, :, cols]
            first = SUBLANES - (CONV_W - 1)
            prev = [jnp.broadcast_to(carry[first + r:first + r + 1, :], (tm, sub)) for r in range(CONV_W - 1)]
            t_in_seq = row
            tail = a[tm - SUBLANES:, :]
            carry_ref[j, :, cols] = tail
            tail_ref[0, :, cols] = tail
        ac = cb_ref[:, cols] + cw_ref[CONV_W - 1:CONV_W, cols] * a
        for back in range(1, CONV_W):
            shifted = pltpu.roll(a, back, axis=0)
            for r in range(back):
                shifted = jnp.where(t_in_seq == r, prev[CONV_W - 1 - back + r], shifted)
            ac = ac + cw_ref[CONV_W - 1 - back:CONV_W - back, cols] * shifted
        act = (_gelu_exact(ac) * gv).astype(BF16)
        part = _dot(act, wd_ref[cols, :])
        down = part if down is None else down + part
    y_ref[...] += down

    @pl.when(j == pl.num_programs(1) - 1)
    def _():
        y_ref[...] = _rmsnorm(y_ref[...], gf_ref[...])


def _ffn(h, g2, w_up, conv_w, conv_b, w_down, gf, hist, *, seq_rows, tm, tn):
    rows, d = h.shape
    d_ff = w_down.shape[0]
    n_col = d_ff // tn
    n_row = rows // tm
    per_seq_hist = hist is not None
    if per_seq_hist:
        assert tm % seq_rows == 0
        tiles_per_seq = 1
        tail_shape = jax.ShapeDtypeStruct((rows, d_ff), F32)
        tail_spec = pl.BlockSpec((tm, tn), lambda i, j: (i, j))
        hist_args = (hist,)
        hist_specs = [pl.BlockSpec((tm // seq_rows, CONV_W - 1, tn), lambda i, j: (i, 0, j))]
    else:
        assert seq_rows % tm == 0
        tiles_per_seq = seq_rows // tm
        tail_shape = jax.ShapeDtypeStruct((n_row, SUBLANES, d_ff), F32)
        tail_spec = pl.BlockSpec((1, SUBLANES, tn), lambda i, j: (i, 0, j))
        hist_args = ()
        hist_specs = []
    y, tail = pl.pallas_call(
        functools.partial(_ffn_kernel, seq_rows=seq_rows, tiles_per_seq=tiles_per_seq, per_seq_hist=per_seq_hist),
        grid=(n_row, n_col),
        in_specs=[
            pl.BlockSpec((tm, d), lambda i, j: (i, 0), pipeline_mode=pl.Buffered(1)),
            pl.BlockSpec((1, d), lambda i, j: (0, 0)),
            pl.BlockSpec((d, tn), lambda i, j: (0, j)),
            pl.BlockSpec((d, tn), lambda i, j: (0, n_col + j)),
            pl.BlockSpec((CONV_W, tn), lambda i, j: (0, j)),
            pl.BlockSpec((1, tn), lambda i, j: (0, j)),
            pl.BlockSpec((tn, d), lambda i, j: (j, 0)),
            pl.BlockSpec((1, d), lambda i, j: (0, 0)),
            *hist_specs,
        ],
        out_specs=[pl.BlockSpec((tm, d), lambda i, j: (i, 0)), tail_spec],
        out_shape=[jax.ShapeDtypeStruct((rows, d), F32), tail_shape],
        scratch_shapes=[
            pltpu.VMEM((tm, d), BF16),
            pltpu.VMEM((n_col, SUBLANES, tn), F32),
        ],
        compiler_params=_params("arbitrary", "arbitrary"),
        name="ffn",
    )(h, g2, w_up, w_up, conv_w, conv_b, w_down, gf, *hist_args)
    return y, tail


def _layer(x3, pool_hist, conv_hist, attend, pos0, w, *, tm):
    n_seq, seq_len, d = x3.shape
    rows = n_seq * seq_len
    x = x3.reshape(rows, d)
    w_pool = w["pool_scale"].shape[1]
    w_att = w["w_att_up"].shape[0]
    proj = _norm_matmul(x, w["g1"], w["w_in"], tm=tm, tn=1024)
    proj3 = proj.reshape(n_seq, seq_len, -1)
    q_col, k_col, v_col = w_pool, w_pool + w_att, w_pool + 2 * w_att
    ga_col, gb_col = w_pool + 3 * w_att, w_pool + 3 * w_att + d
    if pool_hist is None:
        a_out = _pool_mix(proj3, None, w["pool_maps"], w["pool_scale"], s_blk=1, l_blk=POOL_TILE_ROWS, pos0=pos0,
                          out_dtype=BF16)
    else:
        a_out = _pool_mix(proj3, pool_hist, w["pool_maps"], w["pool_scale"], s_blk=32, l_blk=seq_len, pos0=pos0,
                          out_dtype=F32)
    b_out = attend(proj3, q_col, k_col, v_col)
    merged = _merge(a_out, b_out, w["w_pool_up"], w["w_att_up"], proj, ga_col=ga_col, gb_col=gb_col, tm=tm, tn=1024)
    h = _matmul_residual(merged, w["w_out"], x, tm=tm, tn=1024)
    y, tail = _ffn(h, w["g2"], w["w_up"], w["conv_w"], w["conv_b"], w["w_down"], w["gf"], conv_hist,
                   seq_rows=seq_len, tm=tm, tn=512)
    return y.reshape(n_seq, seq_len, d), proj3, tail


def kernel(x_prompt, x_sample, cache_k, cache_v, state_pool, state_conv, page_table, norm1_g, w_in, pool_maps,
           pool_scale, w_pool_up, w_att_up, w_out, norm2_g, w_up, conv_w, conv_b, w_down, final_norm_g):
    depth = w_in.shape[0]
    assert depth == 1, "single-layer trunk"
    bsz, seq, d = x_prompt.shape
    n_dec, dec_seq, _ = x_sample.shape
    _, n_pool, page, n_heads, dh = cache_k.shape
    assert dh == HEAD_DIM
    w_att = n_heads * dh
    w_pool = pool_scale.shape[1]
    d_ff = w_down.shape[1]
    past_len = page_table.shape[1] * page
    slopes = 2.0 ** (-8.0 * jnp.arange(1, n_heads + 1, dtype=F32) / n_heads)

    l = 0
    w = dict(
        g1=norm1_g[l][None, :], w_in=w_in[l].astype(BF16), pool_maps=pool_maps[l].astype(BF16),
        pool_scale=pool_scale[l][None, :], w_pool_up=w_pool_up[l].astype(BF16), w_att_up=w_att_up[l].astype(BF16),
        w_out=w_out[l].astype(BF16), g2=norm2_g[l][None, :], w_up=w_up[l].astype(BF16), conv_w=conv_w[l],
        conv_b=conv_b[l][None, :], w_down=w_down[l].astype(BF16), gf=final_norm_g[None, :],
    )
    tm = ROW_TILE

    attend_p = lambda proj3, qc, kc, vc: _prompt_attention(proj3, slopes, n_heads=n_heads, q_col=qc, k_col=kc, v_col=vc)
    y_p, proj_p, tail_p = _layer(x_prompt, None, None, attend_p, 0, w, tm=tm)

    attend_s = lambda proj3, qc, kc, vc: _sample_attention(proj3, cache_k, cache_v, l, page_table, slopes,
                                                           q_col=qc, k_col=kc, v_col=vc)
    y_s, proj_s, tail_s = _layer(x_sample, state_pool[l], state_conv[l], attend_s, past_len, w, tm=tm)

    heads = lambda p3, col: p3[:, :, col:col + w_att].reshape(p3.shape[0], p3.shape[1], n_heads, dh)[None]
    k_p, v_p = heads(proj_p, w_pool + w_att), heads(proj_p, w_pool + 2 * w_att)
    k_s, v_s = heads(proj_s, w_pool + w_att), heads(proj_s, w_pool + 2 * w_att)
    pool_p = proj_p[:, seq - POOL_HIST:, :w_pool][None]
    pool_s = jnp.concatenate([state_pool[l], proj_s[:, :, :w_pool]], axis=1)[:, -POOL_HIST:][None]
    tiles_per_seq = seq // tm
    conv_p = tail_p.reshape(bsz, tiles_per_seq, SUBLANES, d_ff)[:, -1, SUBLANES - (CONV_W - 1):][None]
    conv_s = jnp.concatenate([state_conv[l], tail_s.reshape(n_dec, dec_seq, d_ff)], axis=1)[:, -(CONV_W - 1):][None]
    return (y_p, y_s, k_p, v_p, k_s, v_s, pool_p, pool_s, conv_p, conv_s)
```

```python
import functools

import jax
import jax.numpy as jnp
from jax import lax
from jax.experimental import pallas as pl
from jax.experimental.pallas import tpu as pltpu

POOL_WINDOWS = (2, 4, 8, 16)
POOL_HIST = max(POOL_WINDOWS) - 1
HEAD_DIM = 128
MOBA_BLOCK = 256
MOBA_TOPK = 3
CONV_W = 3
RMS_EPS = 1e-6
NEG_INF = -1e30

SUBLANES = 8
HALO_ROWS = 16
ROW_TILE = 1024
POOL_TILE_ROWS = 512
V7X_VMEM_BYTES = 64 * 1024 * 1024
VMEM_LIMIT_BYTES = 56 * 1024 * 1024

BF16 = jnp.bfloat16
F32 = jnp.float32


def _params(*semantics):
    return pltpu.CompilerParams(dimension_semantics=semantics, vmem_limit_bytes=VMEM_LIMIT_BYTES)


def _rmsnorm(x, g):
    return x * lax.rsqrt(jnp.mean(x * x, axis=-1, keepdims=True) + RMS_EPS) * g


def _dot(a, b):
    return jnp.dot(a, b, preferred_element_type=F32)


def _dot_nt(a, b):
    return lax.dot_general(a, b, (((1,), (1,)), ((), ())), preferred_element_type=F32)


def _split_bf16(x):
    hi = x.astype(BF16)
    lo = (x - hi.astype(F32)).astype(BF16)
    return hi, lo


def _dot_nt_3pass(a, b):
    a_hi, a_lo = _split_bf16(a)
    b_hi, b_lo = _split_bf16(b)
    return _dot_nt(a_hi, b_hi) + (_dot_nt(a_hi, b_lo) + _dot_nt(a_lo, b_hi))


def _in_proj_kernel(x_ref, g_ref, w_ref, rest_ref, k_ref, v_ref, *more, k_tiles, v_tiles, export_w):
    xn_ref = more[-1]
    j = pl.program_id(1)

    @pl.when(j == 0)
    def _():
        xn_ref[...] = _rmsnorm(x_ref[...], g_ref[...]).astype(BF16)

    wb = w_ref[...].astype(BF16)
    if export_w:
        more[0][...] = wb
    res = _dot(xn_ref[...], wb)
    is_k = (j >= k_tiles[0]) & (j < k_tiles[1])
    is_v = (j >= v_tiles[0]) & (j < v_tiles[1])

    @pl.when(is_k)
    def _():
        k_ref[...] = res

    @pl.when(is_v)
    def _():
        v_ref[...] = res

    @pl.when(jnp.logical_not(is_k | is_v))
    def _():
        rest_ref[...] = res


def _in_proj(x, g, w, *, k_col, kv_width, tm, tn):
    rows, d = x.shape
    n = w.shape[1]
    export_w = w.dtype != BF16
    assert k_col % tn == 0 and kv_width % tn == 0
    k0, nkv = k_col // tn, kv_width // tn
    v0, after = k0 + nkv, k0 + 2 * nkv
    rest_map = lambda i, j: (i, jnp.where(j < k0, j, jnp.maximum(j - 2 * nkv, k0 - 1)))
    k_map = lambda i, j: (i, jnp.clip(j - k0, 0, nkv - 1))
    v_map = lambda i, j: (i, jnp.clip(j - v0, 0, nkv - 1))
    out_specs = [pl.BlockSpec((tm, tn), rest_map),
                 pl.BlockSpec((tm, tn), k_map, pipeline_mode=pl.Buffered(1)),
                 pl.BlockSpec((tm, tn), v_map, pipeline_mode=pl.Buffered(1))]
    out_shape = [jax.ShapeDtypeStruct((rows, n - 2 * kv_width), F32),
                 jax.ShapeDtypeStruct((rows, kv_width), F32), jax.ShapeDtypeStruct((rows, kv_width), F32)]
    if export_w:
        out_specs.append(pl.BlockSpec((d, tn), lambda i, j: (0, j)))
        out_shape.append(jax.ShapeDtypeStruct((d, n), BF16))
    outs = pl.pallas_call(
        functools.partial(_in_proj_kernel, k_tiles=(k0, v0), v_tiles=(v0, after), export_w=export_w),
        grid=(rows // tm, n // tn),
        in_specs=[
            pl.BlockSpec((tm, d), lambda i, j: (i, 0)),
            pl.BlockSpec((1, d), lambda i, j: (0, 0)),
            pl.BlockSpec((d, tn), lambda i, j: (0, j)),
        ],
        out_specs=out_specs,
        out_shape=out_shape,
        scratch_shapes=[pltpu.VMEM((tm, d), BF16)],
        compiler_params=_params("arbitrary", "arbitrary"),
        name="in_proj",
    )(x, g, w)
    return (*outs, None) if not export_w else tuple(outs)


def _pool_kernel(u_ref, hist_ref, maps_ref, scale_ref, o_ref, ext_ref, *, pos0, zero_first_hist):
    s_blk, l_blk, width = u_ref.shape
    group = width // len(POOL_WINDOWS)
    tile = pl.program_id(1)
    u = u_ref[...]
    ext_ref[:, HALO_ROWS:, :] = u
    hist_rows = hist_ref.shape[1]
    ext_ref[:, HALO_ROWS - hist_rows:HALO_ROWS, :] = hist_ref[...]
    if zero_first_hist:
        @pl.when(tile == 0)
        def _():
            ext_ref[:, 0:HALO_ROWS, :] = jnp.zeros((s_blk, HALO_ROWS, width), F32)

    pos = pos0 + tile * l_blk + lax.broadcasted_iota(jnp.int32, (s_blk, l_blk, group), 1)
    outs = []
    for gi, w in enumerate(POOL_WINDOWS):
        cols = slice(gi * group, (gi + 1) * group)
        win = u[:, :, cols]
        for k in range(1, w):
            win = win + ext_ref[:, HALO_ROWS - k:HALO_ROWS - k + l_blk, cols]
        cnt = jnp.minimum(w, pos + 1).astype(F32)
        d = win / cnt - u[:, :, cols]
        d2 = d.reshape(s_blk * l_blk, group).astype(BF16)
        outs.append(_dot(d2, maps_ref[gi]))
    m = jnp.concatenate(outs, axis=-1) * scale_ref[...]
    o_ref[...] = m.astype(o_ref.dtype)


def _pool_mix(proj3, hist, maps, scale, *, s_blk, l_blk, pos0, out_dtype):
    n_seq, seq_len, _ = proj3.shape
    n_groups, group, _ = maps.shape
    width = n_groups * group
    tiles = seq_len // l_blk
    if hist is None:
        halo_blocks = l_blk // HALO_ROWS
        hist_arr = proj3
        hist_spec = pl.BlockSpec((s_blk, HALO_ROWS, width),
                                 lambda s, t: (s, jnp.maximum(t * halo_blocks - 1, 0), 0))
    else:
        assert tiles == 1
        hist_arr = hist
        hist_spec = pl.BlockSpec((s_blk, hist.shape[1], width), lambda s, t: (s, 0, 0))
    kern = functools.partial(_pool_kernel, pos0=pos0, zero_first_hist=hist is None)
    return pl.pallas_call(
        kern,
        grid=(n_seq // s_blk, tiles),
        in_specs=[
            pl.BlockSpec((s_blk, l_blk, width), lambda s, t: (s, t, 0)),
            hist_spec,
            pl.BlockSpec((n_groups, group, group), lambda s, t: (0, 0, 0)),
            pl.BlockSpec((1, width), lambda s, t: (0, 0)),
        ],
        out_specs=pl.BlockSpec((s_blk * l_blk, width), lambda s, t: (s * tiles + t, 0)),
        out_shape=jax.ShapeDtypeStruct((n_seq * seq_len, width), out_dtype),
        scratch_shapes=[pltpu.VMEM((s_blk, HALO_ROWS + l_blk, width), F32)],
        compiler_params=_params("parallel", "arbitrary"),
        name="pool_mix",
    )(proj3, hist_arr, maps, scale)


def _topk_select(route, n_valid):
    nb, n = route.shape
    blk = lax.broadcasted_iota(jnp.int32, (nb, n), 0)
    rank = jnp.zeros((nb, n), jnp.int32)
    for other in range(nb):
        c = route[other:other + 1, :]
        beats = jnp.where((c > route) | ((c == route) & (other < blk)), 1, 0)
        rank = rank + beats * (other < n_valid).astype(jnp.int32)
    return (blk < n_valid) & (rank < MOBA_TOPK)


def _topk_select_list(scores):
    sel = []
    for j, s_j in enumerate(scores):
        rank = jnp.zeros(s_j.shape, jnp.int32)
        for other, s_o in enumerate(scores):
            if other < j:
                rank = rank + jnp.where(s_o >= s_j, 1, 0)
            elif other > j:
                rank = rank + jnp.where(s_o > s_j, 1, 0)
        sel.append(rank < MOBA_TOPK)
    return sel


KV_TILE = 2 * MOBA_BLOCK
assert KV_TILE == 2 * MOBA_BLOCK
M_FLOOR = -1e20
PROMPT_HEADS_PER_STEP = 4


def _prompt_attn_kernel(slopes_ref, q_ref, k_ref, v_ref, o_ref,
                        kb_ref, vt_ref, kmean_ref, alibi_ref, m_ref, l_ref, acc_ref):
    hgroup = pl.program_id(1)
    qt = pl.program_id(2)
    n_hb, seq_len, dh = kb_ref.shape
    tq = q_ref.shape[1]
    nb = seq_len // MOBA_BLOCK
    blocks_per_tile = KV_TILE // MOBA_BLOCK
    slopes = [slopes_ref[hgroup * n_hb + hh] for hh in range(n_hb)]
    q_minus_k = (lax.broadcasted_iota(jnp.int32, (KV_TILE, tq), 1)
                 - lax.broadcasted_iota(jnp.int32, (KV_TILE, tq), 0))

    @pl.when(qt == 0)
    def _():
        for hh in range(n_hb):
            cols = slice(hh * dh, (hh + 1) * dh)
            k = k_ref[0, :, cols]
            kb_ref[hh] = k.astype(BF16)
            kmean_ref[hh] = jnp.mean(k.reshape(nb, MOBA_BLOCK, dh), axis=1)
            for c in range(nb):
                vt_ref[hh, c] = v_ref[0, c * MOBA_BLOCK:(c + 1) * MOBA_BLOCK, cols].T.astype(BF16)
            alibi_ref[hh] = slopes[hh] * q_minus_k.astype(F32)

    blk = lax.broadcasted_iota(jnp.int32, (nb, tq), 0)
    qbs, sel_bits = [], []
    for hh in range(n_hb):
        q = q_ref[0, :, hh * dh:(hh + 1) * dh]
        qbs.append((q * dh ** -0.5).astype(BF16))
        route = _dot_nt_3pass(kmean_ref[hh], q)
        sel = _topk_select(route, qt)
        bits = jnp.sum(jnp.where(sel, jnp.left_shift(1, blk).astype(F32), 0.0), axis=0, keepdims=True)
        sel_bits.append(bits.astype(jnp.int32))

    m_ref[...] = jnp.full(m_ref.shape, M_FLOOR, F32)
    l_ref[...] = jnp.zeros(l_ref.shape, F32)
    acc_ref[...] = jnp.zeros(acc_ref.shape, F32)

    def block_bias(hh, j):
        bit = jnp.right_shift(sel_bits[hh], j) & 1
        return jnp.broadcast_to(jnp.where(bit == 1, 0.0, NEG_INF), (MOBA_BLOCK, tq))

    def attend(first_blk, n_blk, bias_fn, offset):
        keys = n_blk * MOBA_BLOCK
        start = pl.multiple_of(first_blk * MOBA_BLOCK, MOBA_BLOCK)
        scores = [_dot_nt(kb_ref[hh, pl.ds(start, keys), :], qbs[hh]) for hh in range(n_hb)]
        for hh in range(n_hb):
            z = scores[hh] - alibi_ref[hh, 0:keys, :] + bias_fn(hh)
            shift = slopes[hh] * jnp.asarray(offset).astype(F32)
            m_old = m_ref[hh]
            m_new = jnp.maximum(m_old, jnp.max(z, axis=0, keepdims=True) - shift)
            alpha = jnp.exp(m_old - m_new)
            pr = jnp.exp(z - (m_new + shift))
            l_ref[hh] = alpha * l_ref[hh] + jnp.sum(pr, axis=0, keepdims=True)
            pv = None
            for b in range(n_blk):
                part = _dot(vt_ref[hh, first_blk + b], pr[b * MOBA_BLOCK:(b + 1) * MOBA_BLOCK].astype(BF16))
                pv = part if pv is None else pv + part
            acc_ref[hh] = alpha * acc_ref[hh] + pv
            m_ref[hh] = m_new

    def past_pair(p, carry):
        first = p * blocks_per_tile
        attend(first, blocks_per_tile,
               lambda hh: jnp.concatenate([block_bias(hh, first + b) for b in range(blocks_per_tile)], axis=0),
               (qt - first) * MOBA_BLOCK)
        return carry

    lax.fori_loop(0, qt // blocks_per_tile, past_pair, 0)

    @pl.when(qt % blocks_per_tile == 1)
    def _():
        attend(qt - 1, 1, lambda hh: block_bias(hh, qt - 1), MOBA_BLOCK)

    causal = jnp.where(q_minus_k[0:MOBA_BLOCK, :] >= 0, 0.0, NEG_INF)
    attend(qt, 1, lambda hh: causal, 0)
    for hh in range(n_hb):
        o_ref[:, hh * dh:(hh + 1) * dh] = (acc_ref[hh] / l_ref[hh]).T.astype(o_ref.dtype)


def _prompt_attention(q_src3, k3, v3, slopes, *, q_col):
    bsz, seq_len, w_att = k3.shape
    n_heads = w_att // HEAD_DIM
    tq = MOBA_BLOCK
    nq = seq_len // tq
    nb = seq_len // MOBA_BLOCK
    hb = PROMPT_HEADS_PER_STEP
    gw = hb * HEAD_DIM
    assert n_heads % hb == 0 and seq_len % KV_TILE == 0 and q_col % gw == 0
    qb0 = q_col // gw
    return pl.pallas_call(
        _prompt_attn_kernel,
        grid=(bsz, n_heads // hb, nq),
        in_specs=[
            pl.BlockSpec(memory_space=pltpu.SMEM),
            pl.BlockSpec((1, tq, gw), lambda b, h, i: (b, i, qb0 + h)),
            pl.BlockSpec((1, seq_len, gw), lambda b, h, i: (b, 0, h)),
            pl.BlockSpec((1, seq_len, gw), lambda b, h, i: (b, 0, h)),
        ],
        out_specs=pl.BlockSpec((tq, gw), lambda b, h, i: (b * nq + i, h)),
        out_shape=jax.ShapeDtypeStruct((bsz * seq_len, n_heads * HEAD_DIM), BF16),
        scratch_shapes=[
            pltpu.VMEM((hb, seq_len, HEAD_DIM), BF16),
            pltpu.VMEM((hb, nb, HEAD_DIM, MOBA_BLOCK), BF16),
            pltpu.VMEM((hb, nb, HEAD_DIM), F32),
            pltpu.VMEM((hb, KV_TILE, tq), F32),
            pltpu.VMEM((hb, 1, tq), F32),
            pltpu.VMEM((hb, 1, tq), F32),
            pltpu.VMEM((hb, HEAD_DIM, tq), F32),
        ],
        compiler_params=_params("parallel", "parallel", "arbitrary"),
        name="prompt_attn",
    )(slopes, q_src3, k3, v3)


def _heads_to_rows(x, n_heads, dh):
    return jnp.concatenate([x[:, h * dh:(h + 1) * dh] for h in range(n_heads)], axis=0)


def _sample_attn_kernel(pt_ref, slopes_ref, q_ref, kn_ref, vn_ref, *rest, past_len, n_pages):
    del pt_ref
    k_refs, v_refs, o_ref = rest[:n_pages], rest[n_pages:2 * n_pages], rest[2 * n_pages]
    t = q_ref.shape[1]
    page, n_heads, dh = k_refs[0].shape
    pages_per_block = MOBA_BLOCK // page
    nb = n_pages // pages_per_block
    rows = n_heads * t
    page_keys = page * n_heads

    q2 = _heads_to_rows(q_ref[0], n_heads, dh)
    q2b = (q2 * dh ** -0.5).astype(BF16)
    row_head = lax.broadcasted_iota(jnp.int32, (rows, 1), 0) // t
    row_query = lax.broadcasted_iota(jnp.int32, (rows, 1), 0) % t
    slope_rows = jnp.zeros((rows, 1), F32)
    for h in range(n_heads):
        slope_rows = jnp.where(row_head == h, slopes_ref[h], slope_rows)
    lane = lax.broadcasted_iota(jnp.int32, (rows, page_keys), 1)
    page_bias = jnp.where(lane % n_heads == row_head,
                          slope_rows * (row_query - lane // n_heads).astype(F32), -NEG_INF)

    def block_scores(b):
        ksum = jnp.zeros((n_heads, dh), F32)
        s_pages = []
        for pg in range(b * pages_per_block, (b + 1) * pages_per_block):
            kp = k_refs[pg][...]
            ksum = ksum + jnp.sum(kp, axis=0)
            s_pg = _dot_nt(q2b, kp.reshape(page_keys, dh).astype(BF16))
            s_pages.append(s_pg - page_bias - slope_rows * float(past_len - pg * page))
        kmean = ksum / MOBA_BLOCK
        kmean_rows = jnp.concatenate([jnp.broadcast_to(kmean[h:h + 1, :], (t, dh)) for h in range(n_heads)], axis=0)
        return jnp.sum(q2 * kmean_rows, axis=-1, keepdims=True), jnp.concatenate(s_pages, axis=-1)

    route, m_blk, l_blk, acc_blk = [], [], [], []
    ahead = block_scores(0)
    for b in range(nb):
        route_b, s_b = ahead
        if b + 1 < nb:
            ahead = block_scores(b + 1)
        pages = range(b * pages_per_block, (b + 1) * pages_per_block)
        vblk = jnp.concatenate([v_refs[pg][...].reshape(page_keys, dh) for pg in pages], axis=0)
        m_b = jnp.max(s_b, axis=-1, keepdims=True)
        p = jnp.exp(s_b - m_b)
        route.append(route_b)
        m_blk.append(m_b)
        l_blk.append(jnp.sum(p, axis=-1, keepdims=True))
        acc_blk.append(_dot(p.astype(BF16), vblk.astype(BF16)))

    sel = _topk_select_list(route)
    kn2 = _heads_to_rows(kn_ref[0], n_heads, dh)
    vn2 = _heads_to_rows(vn_ref[0], n_heads, dh)
    col = lax.broadcasted_iota(jnp.int32, (rows, rows), 1)
    dist = row_query - col % t
    keep = (col // t == row_head) & (dist >= 0)
    s_own = _dot_nt(q2b, kn2.astype(BF16)) - slope_rows * dist.astype(F32)
    s_own = jnp.where(keep, s_own, NEG_INF)
    m_all = jnp.max(s_own, axis=-1, keepdims=True)
    for b in range(nb):
        m_all = jnp.maximum(m_all, jnp.where(sel[b], m_blk[b], NEG_INF))
    p_own = jnp.where(keep, jnp.exp(s_own - m_all), 0.0)
    den = jnp.sum(p_own, axis=-1, keepdims=True)
    num = _dot(p_own.astype(BF16), vn2.astype(BF16))
    for b in range(nb):
        w_b = jnp.where(sel[b], jnp.exp(m_blk[b] - m_all), 0.0)
        den = den + w_b * l_blk[b]
        num = num + w_b * acc_blk[b]
    out = num / den
    o_ref[...] = jnp.concatenate([out[h * t:(h + 1) * t, :] for h in range(n_heads)], axis=-1)


def _sample_attention(q_src3, k3, v3, cache_k, cache_v, layer, page_table, slopes, *, q_col):
    n_seq, t, _ = k3.shape
    _, _, page, n_heads, dh = cache_k.shape
    width = n_heads * dh
    n_pages = page_table.shape[1]
    assert MOBA_BLOCK % page == 0 and (n_pages * page) % MOBA_BLOCK == 0 and t <= MOBA_BLOCK
    assert q_col % width == 0
    past_len = n_pages * page
    qb0 = q_col // width
    page_specs = [pl.BlockSpec((None, None, page, n_heads, dh), lambda s, pt, pg=pg: (layer, pt[s, pg], 0, 0, 0))
                  for pg in range(n_pages)]
    grid_spec = pltpu.PrefetchScalarGridSpec(
        num_scalar_prefetch=1,
        grid=(n_seq,),
        in_specs=[
            pl.BlockSpec(memory_space=pltpu.SMEM),
            pl.BlockSpec((1, t, width), lambda s, pt: (s, 0, qb0)),
            pl.BlockSpec((1, t, width), lambda s, pt: (s, 0, 0)),
            pl.BlockSpec((1, t, width), lambda s, pt: (s, 0, 0)),
            *page_specs, *page_specs,
        ],
        out_specs=pl.BlockSpec((t, width), lambda s, pt: (s, 0)),
    )
    return pl.pallas_call(
        functools.partial(_sample_attn_kernel, past_len=past_len, n_pages=n_pages),
        grid_spec=grid_spec,
        out_shape=jax.ShapeDtypeStruct((n_seq * t, width), F32),
        compiler_params=_params("parallel"),
        name="sample_attn",
    )(page_table, slopes, q_src3, k3, v3, *([cache_k] * n_pages), *([cache_v] * n_pages))


def _merge_kernel(a_ref, b_ref, wp_ref, wa_ref, ga_ref, gb_ref, o_ref):
    pa = _dot(a_ref[...].astype(BF16), wp_ref[...])
    pb = _dot(b_ref[...].astype(BF16), wa_ref[...])
    merged = jax.nn.sigmoid(ga_ref[...]) * pa + jax.nn.sigmoid(gb_ref[...]) * pb
    o_ref[...] = merged.astype(o_ref.dtype)


def _merge(a_out, b_out, w_pool_up, w_att_up, proj, *, ga_col, gb_col, tm, tn):
    rows, wa = a_out.shape
    wb = b_out.shape[1]
    n = w_pool_up.shape[1]
    ga0, gb0 = ga_col // tn, gb_col // tn
    return pl.pallas_call(
        _merge_kernel,
        grid=(rows // tm, n // tn),
        in_specs=[
            pl.BlockSpec((tm, wa), lambda i, j: (i, 0)),
            pl.BlockSpec((tm, wb), lambda i, j: (i, 0)),
            pl.BlockSpec((wa, tn), lambda i, j: (0, j)),
            pl.BlockSpec((wb, tn), lambda i, j: (0, j)),
            pl.BlockSpec((tm, tn), lambda i, j: (i, ga0 + j)),
            pl.BlockSpec((tm, tn), lambda i, j: (i, gb0 + j)),
        ],
        out_specs=pl.BlockSpec((tm, tn), lambda i, j: (i, j)),
        out_shape=jax.ShapeDtypeStruct((rows, n), BF16),
        compiler_params=_params("parallel", "arbitrary"),
        name="merge",
    )(a_out, b_out, w_pool_up, w_att_up, proj, proj)


def _matmul_residual_kernel(a_ref, w_ref, r_ref, o_ref):
    o_ref[...] = r_ref[...] + _dot(a_ref[...], w_ref[...])


def _matmul_residual(a, w, res, *, tm, tn):
    rows, k = a.shape
    n = w.shape[1]
    return pl.pallas_call(
        _matmul_residual_kernel,
        grid=(rows // tm, n // tn),
        in_specs=[
            pl.BlockSpec((tm, k), lambda i, j: (i, 0)),
            pl.BlockSpec((k, tn), lambda i, j: (0, j)),
            pl.BlockSpec((tm, tn), lambda i, j: (i, j)),
        ],
        out_specs=pl.BlockSpec((tm, tn), lambda i, j: (i, j)),
        out_shape=jax.ShapeDtypeStruct((rows, n), F32),
        compiler_params=_params("parallel", "arbitrary"),
        name="out_proj",
    )(a, w, res)


FFN_CHUNK_COLS = 256


def _gelu_exact(x):
    return 0.5 * x * (1.0 + lax.erf(x * (2.0 ** -0.5)))


def _ffn_kernel(h_ref, g2_ref, wa_ref, wg_ref, cw_ref, cb_ref, wd_ref, gf_ref, *rest,
                seq_rows, tiles_per_seq, per_seq_hist, export_w):
    hist_ref = rest[0] if per_seq_hist else None
    n_in = 1 if per_seq_hist else 0
    y_ref, tail_ref = rest[n_in:n_in + 2]
    wa_out, wg_out, wd_out = rest[n_in + 2:n_in + 5] if export_w else (None, None, None)
    hn_ref, carry_ref = rest[-2:]
    i = pl.program_id(0)
    j = pl.program_id(1)
    tm = h_ref.shape[0]
    tn = wa_ref.shape[1]
    sub = FFN_CHUNK_COLS
    n_chunks = tn // sub

    @pl.when(j == 0)
    def _():
        h = h_ref[...]
        hn_ref[...] = _rmsnorm(h, g2_ref[...]).astype(BF16)
        y_ref[...] = h

    if not per_seq_hist:
        @pl.when(i % tiles_per_seq == 0)
        def _():
            carry_ref[j] = jnp.zeros((SUBLANES, tn), F32)

    hn = hn_ref[...]
    chunks = [slice(c * sub, (c + 1) * sub) for c in range(n_chunks)]
    wa_parts = [wa_ref[:, cols].astype(BF16) for cols in chunks]
    wg_parts = [wg_ref[:, cols].astype(BF16) for cols in chunks]
    wd_parts = [wd_ref[cols, :].astype(BF16) for cols in chunks]
    if export_w:
        for cols, wa_c, wg_c, wd_c in zip(chunks, wa_parts, wg_parts, wd_parts):
            wa_out[:, cols] = wa_c
            wg_out[:, cols] = wg_c
            wd_out[cols, :] = wd_c
    a_parts = [_dot(hn, wa_c) for wa_c in wa_parts]
    g_parts = [_dot(hn, wg_c) for wg_c in wg_parts]
    row = lax.broadcasted_iota(jnp.int32, (tm, sub), 0)
    down = None
    for cols, a, gv, wd_c in zip(chunks, a_parts, g_parts, wd_parts):
        if per_seq_hist:
            n_seq = tm // seq_rows
            hist = hist_ref[:, :, cols]
            prev = [jnp.broadcast_to(hist[:, r:r + 1, :], (n_seq, seq_rows, sub)).reshape(tm, sub)
                    for r in range(CONV_W - 1)]
            t_in_seq = row % seq_rows
            tail_ref[:, :, cols] = a.reshape(n_seq, seq_rows, sub)[:, seq_rows - (CONV_W - 1):, :]
        else:
            carry = carry_ref[j, :, cols]
            first = SUBLANES - (CONV_W - 1)
            prev = [jnp.broadcast_to(carry[first + r:first + r + 1, :], (tm, sub)) for r in range(CONV_W - 1)]
            t_in_seq = row
            tail = a[tm - SUBLANES:, :]
            carry_ref[j, :, cols] = tail
            tail_ref[0, :, cols] = tail
        ac = cb_ref[:, cols] + cw_ref[CONV_W - 1:CONV_W, cols] * a
        for back in range(1, CONV_W):
            shifted = pltpu.roll(a, back, axis=0)
            for r in range(back):
                shifted = jnp.where(t_in_seq == r, prev[CONV_W - 1 - back + r], shifted)
            ac = ac + cw_ref[CONV_W - 1 - back:CONV_W - back, cols] * shifted
        act = (_gelu_exact(ac) * gv).astype(BF16)
        part = _dot(act, wd_c)
        down = part if down is None else down + part
    y_ref[...] += down

    @pl.when(j == pl.num_programs(1) - 1)
    def _():
        y_ref[...] = _rmsnorm(y_ref[...], gf_ref[...])


def _ffn(h, g2, w_up, conv_w, conv_b, w_down, gf, hist, *, seq_rows, tm, tn):
    rows, d = h.shape
    d_ff = w_down.shape[0]
    n_col = d_ff // tn
    n_row = rows // tm
    per_seq_hist = hist is not None
    export_w = not isinstance(w_up, tuple)
    if export_w:
        w_a, w_g, g_off = w_up, w_up, n_col
        assert w_up.dtype == F32 and w_down.dtype == F32
    else:
        (w_a, w_g), g_off = w_up, 0
    if per_seq_hist:
        assert tm % seq_rows == 0 and seq_rows >= CONV_W - 1
        tiles_per_seq = 1
        tail_shape = jax.ShapeDtypeStruct((rows // seq_rows, CONV_W - 1, d_ff), F32)
        tail_spec = pl.BlockSpec((tm // seq_rows, CONV_W - 1, tn), lambda i, j: (i, 0, j))
        hist_args = (hist,)
        hist_specs = [pl.BlockSpec((tm // seq_rows, CONV_W - 1, tn), lambda i, j: (i, 0, j))]
    else:
        assert seq_rows % tm == 0
        tiles_per_seq = seq_rows // tm
        tail_shape = jax.ShapeDtypeStruct((n_row, SUBLANES, d_ff), F32)
        tail_spec = pl.BlockSpec((1, SUBLANES, tn), lambda i, j: (i, 0, j))
        hist_args = ()
        hist_specs = []
    out_specs = [pl.BlockSpec((tm, d), lambda i, j: (i, 0)), tail_spec]
    out_shape = [jax.ShapeDtypeStruct((rows, d), F32), tail_shape]
    if export_w:
        out_specs += [pl.BlockSpec((d, tn), lambda i, j: (0, j)), pl.BlockSpec((d, tn), lambda i, j: (0, j)),
                      pl.BlockSpec((tn, d), lambda i, j: (j, 0))]
        out_shape += [jax.ShapeDtypeStruct((d, d_ff), BF16), jax.ShapeDtypeStruct((d, d_ff), BF16),
                      jax.ShapeDtypeStruct((d_ff, d), BF16)]
    outs = pl.pallas_call(
        functools.partial(_ffn_kernel, seq_rows=seq_rows, tiles_per_seq=tiles_per_seq, per_seq_hist=per_seq_hist,
                          export_w=export_w),
        grid=(n_row, n_col),
        in_specs=[
            pl.BlockSpec((tm, d), lambda i, j: (i, 0), pipeline_mode=pl.Buffered(1)),
            pl.BlockSpec((1, d), lambda i, j: (0, 0)),
            pl.BlockSpec((d, tn), lambda i, j: (0, j)),
            pl.BlockSpec((d, tn), lambda i, j: (0, g_off + j)),
            pl.BlockSpec((CONV_W, tn), lambda i, j: (0, j)),
            pl.BlockSpec((1, tn), lambda i, j: (0, j)),
            pl.BlockSpec((tn, d), lambda i, j: (j, 0)),
            pl.BlockSpec((1, d), lambda i, j: (0, 0)),
            *hist_specs,
        ],
        out_specs=out_specs,
        out_shape=out_shape,
        scratch_shapes=[
            pltpu.VMEM((tm, d), BF16),
            pltpu.VMEM((n_col, SUBLANES, tn), F32),
        ],
        compiler_params=_params("arbitrary", "arbitrary"),
        name="ffn",
    )(h, g2, w_a, w_g, conv_w, conv_b, w_down, gf, *hist_args)
    return outs[0], outs[1], (tuple(outs[2:]) if export_w else None)


def _layer(x3, pool_hist, conv_hist, attend, pos0, w, *, tm, in_tn, ffn_tn):
    n_seq, seq_len, d = x3.shape
    rows = n_seq * seq_len
    x = x3.reshape(rows, d)
    w_pool = w["pool_scale"].shape[1]
    w_att = w["w_att_up"].shape[0]
    rest, k, v, w_in_b = _in_proj(x, w["g1"], w["w_in"], k_col=w_pool + w_att, kv_width=w_att, tm=tm, tn=in_tn)
    rest3 = rest.reshape(n_seq, seq_len, -1)
    k3, v3 = k.reshape(n_seq, seq_len, w_att), v.reshape(n_seq, seq_len, w_att)
    q_col, ga_col, gb_col = w_pool, w_pool + w_att, w_pool + w_att + d
    if pool_hist is None:
        a_out = _pool_mix(rest3, None, w["pool_maps"], w["pool_scale"], s_blk=1, l_blk=POOL_TILE_ROWS, pos0=pos0,
                          out_dtype=BF16)
    else:
        a_out = _pool_mix(rest3, pool_hist, w["pool_maps"], w["pool_scale"], s_blk=32, l_blk=seq_len, pos0=pos0,
                          out_dtype=F32)
    b_out = attend(rest3, k3, v3, q_col)
    merged = _merge(a_out, b_out, w["w_pool_up"], w["w_att_up"], rest, ga_col=ga_col, gb_col=gb_col, tm=tm, tn=1024)
    h = _matmul_residual(merged, w["w_out"], x, tm=tm, tn=1024)
    y, tail, ffn_w = _ffn(h, w["g2"], w["w_up"], w["conv_w"], w["conv_b"], w["w_down"], w["gf"], conv_hist,
                          seq_rows=seq_len, tm=tm, tn=ffn_tn)
    return y.reshape(n_seq, seq_len, d), rest3, k3, v3, tail, (w_in_b, ffn_w)


def kernel(x_prompt, x_sample, cache_k, cache_v, state_pool, state_conv, page_table, norm1_g, w_in, pool_maps,
           pool_scale, w_pool_up, w_att_up, w_out, norm2_g, w_up, conv_w, conv_b, w_down, final_norm_g):
    depth = w_in.shape[0]
    assert depth == 1, "single-layer trunk"
    bsz, seq, d = x_prompt.shape
    n_dec, dec_seq, _ = x_sample.shape
    _, n_pool, page, n_heads, dh = cache_k.shape
    assert dh == HEAD_DIM
    w_att = n_heads * dh
    w_pool = pool_scale.shape[1]
    d_ff = w_down.shape[1]
    past_len = page_table.shape[1] * page
    slopes = 2.0 ** (-8.0 * jnp.arange(1, n_heads + 1, dtype=F32) / n_heads)

    l = 0
    w = dict(
        g1=norm1_g[l][None, :], w_in=w_in[l], pool_maps=pool_maps[l].astype(BF16),
        pool_scale=pool_scale[l][None, :], w_pool_up=w_pool_up[l].astype(BF16), w_att_up=w_att_up[l].astype(BF16),
        w_out=w_out[l].astype(BF16), g2=norm2_g[l][None, :], w_up=w_up[l], conv_w=conv_w[l],
        conv_b=conv_b[l][None, :], w_down=w_down[l], gf=final_norm_g[None, :],
    )
    tm = ROW_TILE

    attend_s = lambda q_src3, k3, v3, qc: _sample_attention(q_src3, k3, v3, cache_k, cache_v, l, page_table, slopes,
                                                            q_col=qc)
    y_s, rest_s, k_s, v_s, tail_s, (w_in_b, (w_ua_b, w_ug_b, w_down_b)) = _layer(
        x_sample, state_pool[l], state_conv[l], attend_s, past_len, w, tm=tm, in_tn=512, ffn_tn=256)

    w_p = dict(w, w_in=w_in_b, w_up=(w_ua_b, w_ug_b), w_down=w_down_b)
    attend_p = lambda q_src3, k3, v3, qc: _prompt_attention(q_src3, k3, v3, slopes, q_col=qc)
    y_p, rest_p, k_p, v_p, tail_p, _ = _layer(x_prompt, None, None, attend_p, 0, w_p, tm=tm, in_tn=1024, ffn_tn=512)

    heads = lambda a3: a3.reshape(a3.shape[0], a3.shape[1], n_heads, dh)[None]
    pool_p = rest_p[:, seq - POOL_HIST:, :w_pool][None]
    pool_s = jnp.concatenate([state_pool[l], rest_s[:, :, :w_pool]], axis=1)[:, -POOL_HIST:][None]
    tiles_per_seq = seq // tm
    conv_p = tail_p.reshape(bsz, tiles_per_seq, SUBLANES, d_ff)[:, -1, SUBLANES - (CONV_W - 1):][None]
    conv_s = tail_s[None]
    return (y_p, y_s, heads(k_p), heads(v_p), heads(k_s), heads(v_s), pool_p, pool_s, conv_p, conv_s)
```

```python
import functools

import jax
import jax.numpy as jnp
from jax import lax
from jax.experimental import pallas as pl
from jax.experimental.pallas import tpu as pltpu

POOL_WINDOWS = (2, 4, 8, 16)
POOL_HIST = max(POOL_WINDOWS) - 1
HEAD_DIM = 128
MOBA_BLOCK = 256
MOBA_TOPK = 3
CONV_W = 3
RMS_EPS = 1e-6
NEG_INF = -1e30

SUBLANES = 8
HALO_ROWS = 16
ROW_TILE = 1024
POOL_TILE_ROWS = 512
V7X_VMEM_BYTES = 64 * 1024 * 1024
VMEM_LIMIT_BYTES = 58 * 1024 * 1024

BF16 = jnp.bfloat16
F32 = jnp.float32


def _params(*semantics):
    return pltpu.CompilerParams(dimension_semantics=semantics, vmem_limit_bytes=VMEM_LIMIT_BYTES)


def _rmsnorm(x, g):
    return x * lax.rsqrt(jnp.mean(x * x, axis=-1, keepdims=True) + RMS_EPS) * g


def _dot(a, b):
    return jnp.dot(a, b, preferred_element_type=F32)


def _dot_nt(a, b):
    return lax.dot_general(a, b, (((1,), (1,)), ((), ())), preferred_element_type=F32)


def _split_bf16(x):
    hi = x.astype(BF16)
    lo = (x - hi.astype(F32)).astype(BF16)
    return hi, lo


def _dot_nt_3pass(a, b):
    a_hi, a_lo = _split_bf16(a)
    b_hi, b_lo = _split_bf16(b)
    return _dot_nt(a_hi, b_hi) + (_dot_nt(a_hi, b_lo) + _dot_nt(a_lo, b_hi))


def _in_proj_kernel(x_ref, g_ref, w_ref, rest_ref, k_ref, v_ref, *more, k_tiles, v_tiles, export_w):
    xn_ref = more[-1]
    j = pl.program_id(1)

    @pl.when(j == 0)
    def _():
        xn_ref[...] = _rmsnorm(x_ref[...], g_ref[...]).astype(BF16)

    def project_into(o_ref):
        wb = w_ref[...].astype(BF16)
        if export_w:
            more[0][...] = wb
        o_ref[...] = _dot(xn_ref[...], wb)

    is_k = (j >= k_tiles[0]) & (j < k_tiles[1])
    is_v = (j >= v_tiles[0]) & (j < v_tiles[1])
    pl.when(is_k)(lambda: project_into(k_ref))
    pl.when(is_v)(lambda: project_into(v_ref))
    pl.when(jnp.logical_not(is_k | is_v))(lambda: project_into(rest_ref))


def _in_proj(x, g, w, *, k_col, kv_width, tm, tn):
    rows, d = x.shape
    n = w.shape[1]
    export_w = w.dtype != BF16
    assert k_col % tn == 0 and kv_width % tn == 0
    k0, nkv = k_col // tn, kv_width // tn
    v0, after = k0 + nkv, k0 + 2 * nkv
    rest_map = lambda i, j: (i, jnp.where(j < k0, j, jnp.maximum(j - 2 * nkv, k0 - 1)))
    k_map = lambda i, j: (i, jnp.clip(j - k0, 0, nkv - 1))
    v_map = lambda i, j: (i, jnp.clip(j - v0, 0, nkv - 1))
    out_specs = [pl.BlockSpec((tm, tn), rest_map),
                 pl.BlockSpec((tm, tn), k_map, pipeline_mode=pl.Buffered(1)),
                 pl.BlockSpec((tm, tn), v_map, pipeline_mode=pl.Buffered(1))]
    out_shape = [jax.ShapeDtypeStruct((rows, n - 2 * kv_width), F32),
                 jax.ShapeDtypeStruct((rows, kv_width), F32), jax.ShapeDtypeStruct((rows, kv_width), F32)]
    if export_w:
        out_specs.append(pl.BlockSpec((d, tn), lambda i, j: (0, j)))
        out_shape.append(jax.ShapeDtypeStruct((d, n), BF16))
    outs = pl.pallas_call(
        functools.partial(_in_proj_kernel, k_tiles=(k0, v0), v_tiles=(v0, after), export_w=export_w),
        grid=(rows // tm, n // tn),
        in_specs=[
            pl.BlockSpec((tm, d), lambda i, j: (i, 0)),
            pl.BlockSpec((1, d), lambda i, j: (0, 0)),
            pl.BlockSpec((d, tn), lambda i, j: (0, j)),
        ],
        out_specs=out_specs,
        out_shape=out_shape,
        scratch_shapes=[pltpu.VMEM((tm, d), BF16)],
        compiler_params=_params("arbitrary", "arbitrary"),
        name="in_proj",
    )(x, g, w)
    return (*outs, None) if not export_w else tuple(outs)


def _pool_kernel(u_ref, hist_ref, maps_ref, scale_ref, o_ref, ext_ref, *, pos0, zero_first_hist):
    s_blk, l_blk, width = u_ref.shape
    group = width // len(POOL_WINDOWS)
    tile = pl.program_id(1)
    u = u_ref[...]
    ext_ref[:, HALO_ROWS:, :] = u
    hist_rows = hist_ref.shape[1]
    ext_ref[:, HALO_ROWS - hist_rows:HALO_ROWS, :] = hist_ref[...]
    if zero_first_hist:
        @pl.when(tile == 0)
        def _():
            ext_ref[:, 0:HALO_ROWS, :] = jnp.zeros((s_blk, HALO_ROWS, width), F32)

    pos = pos0 + tile * l_blk + lax.broadcasted_iota(jnp.int32, (s_blk, l_blk, group), 1)
    outs = []
    for gi, w in enumerate(POOL_WINDOWS):
        cols = slice(gi * group, (gi + 1) * group)
        win = u[:, :, cols]
        for k in range(1, w):
            win = win + ext_ref[:, HALO_ROWS - k:HALO_ROWS - k + l_blk, cols]
        cnt = jnp.minimum(w, pos + 1).astype(F32)
        d = win / cnt - u[:, :, cols]
        d2 = d.reshape(s_blk * l_blk, group).astype(BF16)
        outs.append(_dot(d2, maps_ref[gi]))
    m = jnp.concatenate(outs, axis=-1) * scale_ref[...]
    o_ref[...] = m.astype(o_ref.dtype)


def _pool_mix(proj3, hist, maps, scale, *, s_blk, l_blk, pos0, out_dtype):
    n_seq, seq_len, _ = proj3.shape
    n_groups, group, _ = maps.shape
    width = n_groups * group
    tiles = seq_len // l_blk
    if hist is None:
        halo_blocks = l_blk // HALO_ROWS
        hist_arr = proj3
        hist_spec = pl.BlockSpec((s_blk, HALO_ROWS, width),
                                 lambda s, t: (s, jnp.maximum(t * halo_blocks - 1, 0), 0))
    else:
        assert tiles == 1
        hist_arr = hist
        hist_spec = pl.BlockSpec((s_blk, hist.shape[1], width), lambda s, t: (s, 0, 0))
    kern = functools.partial(_pool_kernel, pos0=pos0, zero_first_hist=hist is None)
    return pl.pallas_call(
        kern,
        grid=(n_seq // s_blk, tiles),
        in_specs=[
            pl.BlockSpec((s_blk, l_blk, width), lambda s, t: (s, t, 0)),
            hist_spec,
            pl.BlockSpec((n_groups, group, group), lambda s, t: (0, 0, 0)),
            pl.BlockSpec((1, width), lambda s, t: (0, 0)),
        ],
        out_specs=pl.BlockSpec((s_blk * l_blk, width), lambda s, t: (s * tiles + t, 0)),
        out_shape=jax.ShapeDtypeStruct((n_seq * seq_len, width), out_dtype),
        scratch_shapes=[pltpu.VMEM((s_blk, HALO_ROWS + l_blk, width), F32)],
        compiler_params=_params("parallel", "arbitrary"),
        name="pool_mix",
    )(proj3, hist_arr, maps, scale)


def _topk_select(route, n_valid):
    nb, n = route.shape
    blk = lax.broadcasted_iota(jnp.int32, (nb, n), 0)
    rank = jnp.zeros((nb, n), jnp.int32)
    for other in range(nb):
        c = route[other:other + 1, :]
        beats = jnp.where((c > route) | ((c == route) & (other < blk)), 1, 0)
        rank = rank + beats * (other < n_valid).astype(jnp.int32)
    return (blk < n_valid) & (rank < MOBA_TOPK)


def _topk_select_list(scores):
    sel = []
    for j, s_j in enumerate(scores):
        rank = jnp.zeros(s_j.shape, jnp.int32)
        for other, s_o in enumerate(scores):
            if other < j:
                rank = rank + jnp.where(s_o >= s_j, 1, 0)
            elif other > j:
                rank = rank + jnp.where(s_o > s_j, 1, 0)
        sel.append(rank < MOBA_TOPK)
    return sel


KV_TILE = 2 * MOBA_BLOCK
assert KV_TILE == 2 * MOBA_BLOCK
M_FLOOR = -1e20
PROMPT_HEADS_PER_STEP = 4
SCORE_LOOKAHEAD = 4


def _prompt_attn_kernel(slopes_ref, q_ref, k_ref, v_ref, o_ref,
                        kb_ref, vt_ref, kmean_ref, alibi_ref, m_ref, l_ref, acc_ref):
    hgroup = pl.program_id(1)
    qt = pl.program_id(2)
    n_hb, seq_len, dh = kb_ref.shape
    tq = q_ref.shape[1]
    nb = seq_len // MOBA_BLOCK
    blocks_per_tile = KV_TILE // MOBA_BLOCK
    slopes = [slopes_ref[hgroup * n_hb + hh] for hh in range(n_hb)]
    q_minus_k = (lax.broadcasted_iota(jnp.int32, (KV_TILE, tq), 1)
                 - lax.broadcasted_iota(jnp.int32, (KV_TILE, tq), 0))

    @pl.when(qt == 0)
    def _():
        for hh in range(n_hb):
            cols = slice(hh * dh, (hh + 1) * dh)
            k = k_ref[0, :, cols]
            kb_ref[hh] = k.astype(BF16)
            kmean_ref[hh] = jnp.mean(k.reshape(nb, MOBA_BLOCK, dh), axis=1)
            for c in range(nb):
                vt_ref[hh, c] = v_ref[0, c * MOBA_BLOCK:(c + 1) * MOBA_BLOCK, cols].T.astype(BF16)
            alibi_ref[hh] = slopes[hh] * q_minus_k.astype(F32)

    blk = lax.broadcasted_iota(jnp.int32, (nb, tq), 0)
    qbs, sel_bits = [], []
    for hh in range(n_hb):
        q = q_ref[0, :, hh * dh:(hh + 1) * dh]
        qbs.append((q * dh ** -0.5).astype(BF16))
        route = _dot_nt_3pass(kmean_ref[hh], q)
        sel = _topk_select(route, qt)
        bits = jnp.sum(jnp.where(sel, jnp.left_shift(1, blk).astype(F32), 0.0), axis=0, keepdims=True)
        sel_bits.append(bits.astype(jnp.int32))

    m_ref[...] = jnp.full(m_ref.shape, M_FLOOR, F32)
    l_ref[...] = jnp.zeros(l_ref.shape, F32)
    acc_ref[...] = jnp.zeros(acc_ref.shape, F32)

    def block_bias(hh, j):
        bit = jnp.right_shift(sel_bits[hh], j) & 1
        return jnp.broadcast_to(jnp.where(bit == 1, 0.0, NEG_INF), (MOBA_BLOCK, tq))

    def attend(first_blk, n_blk, bias_fn, offset):
        keys = n_blk * MOBA_BLOCK
        start = pl.multiple_of(first_blk * MOBA_BLOCK, MOBA_BLOCK)
        score = lambda hh: _dot_nt(kb_ref[hh, pl.ds(start, keys), :], qbs[hh])
        scores = {hh: score(hh) for hh in range(min(SCORE_LOOKAHEAD, n_hb))}
        for hh in range(n_hb):
            if hh + SCORE_LOOKAHEAD < n_hb:
                scores[hh + SCORE_LOOKAHEAD] = score(hh + SCORE_LOOKAHEAD)
            z = scores.pop(hh) - alibi_ref[hh, 0:keys, :] + bias_fn(hh)
            shift = slopes[hh] * jnp.asarray(offset).astype(F32)
            m_old = m_ref[hh]
            m_new = jnp.maximum(m_old, jnp.max(z, axis=0, keepdims=True) - shift)
            alpha = jnp.exp(m_old - m_new)
            pr = jnp.exp(z - (m_new + shift))
            l_ref[hh] = alpha * l_ref[hh] + jnp.sum(pr, axis=0, keepdims=True)
            pv = None
            for b in range(n_blk):
                part = _dot(vt_ref[hh, first_blk + b], pr[b * MOBA_BLOCK:(b + 1) * MOBA_BLOCK].astype(BF16))
                pv = part if pv is None else pv + part
            acc_ref[hh] = alpha * acc_ref[hh] + pv
            m_ref[hh] = m_new

    def past_pair(p, carry):
        first = p * blocks_per_tile
        attend(first, blocks_per_tile,
               lambda hh: jnp.concatenate([block_bias(hh, first + b) for b in range(blocks_per_tile)], axis=0),
               (qt - first) * MOBA_BLOCK)
        return carry

    lax.fori_loop(0, qt // blocks_per_tile, past_pair, 0)

    @pl.when(qt % blocks_per_tile == 1)
    def _():
        attend(qt - 1, 1, lambda hh: block_bias(hh, qt - 1), MOBA_BLOCK)

    causal = jnp.where(q_minus_k[0:MOBA_BLOCK, :] >= 0, 0.0, NEG_INF)
    attend(qt, 1, lambda hh: causal, 0)
    for hh in range(n_hb):
        o_ref[:, hh * dh:(hh + 1) * dh] = (acc_ref[hh] / l_ref[hh]).T.astype(o_ref.dtype)


def _prompt_attention(q_src3, k3, v3, slopes, *, q_col):
    bsz, seq_len, w_att = k3.shape
    n_heads = w_att // HEAD_DIM
    tq = MOBA_BLOCK
    nq = seq_len // tq
    nb = seq_len // MOBA_BLOCK
    hb = PROMPT_HEADS_PER_STEP
    gw = hb * HEAD_DIM
    assert n_heads % hb == 0 and seq_len % KV_TILE == 0 and q_col % gw == 0
    qb0 = q_col // gw
    return pl.pallas_call(
        _prompt_attn_kernel,
        grid=(bsz, n_heads // hb, nq),
        in_specs=[
            pl.BlockSpec(memory_space=pltpu.SMEM),
            pl.BlockSpec((1, tq, gw), lambda b, h, i: (b, i, qb0 + h)),
            pl.BlockSpec((1, seq_len, gw), lambda b, h, i: (b, 0, h)),
            pl.BlockSpec((1, seq_len, gw), lambda b, h, i: (b, 0, h)),
        ],
        out_specs=pl.BlockSpec((tq, gw), lambda b, h, i: (b * nq + i, h)),
        out_shape=jax.ShapeDtypeStruct((bsz * seq_len, n_heads * HEAD_DIM), BF16),
        scratch_shapes=[
            pltpu.VMEM((hb, seq_len, HEAD_DIM), BF16),
            pltpu.VMEM((hb, nb, HEAD_DIM, MOBA_BLOCK), BF16),
            pltpu.VMEM((hb, nb, HEAD_DIM), F32),
            pltpu.VMEM((hb, KV_TILE, tq), F32),
            pltpu.VMEM((hb, 1, tq), F32),
            pltpu.VMEM((hb, 1, tq), F32),
            pltpu.VMEM((hb, HEAD_DIM, tq), F32),
        ],
        compiler_params=_params("parallel", "parallel", "arbitrary"),
        name="prompt_attn",
    )(slopes, q_src3, k3, v3)


def _heads_to_rows(x, n_heads, dh):
    return jnp.concatenate([x[:, h * dh:(h + 1) * dh] for h in range(n_heads)], axis=0)


def _sample_attn_kernel(pt_ref, slopes_ref, q_ref, kn_ref, vn_ref, *rest, past_len, n_pages):
    del pt_ref
    k_refs, v_refs, o_ref = rest[:n_pages], rest[n_pages:2 * n_pages], rest[2 * n_pages]
    t = q_ref.shape[1]
    page, n_heads, dh = k_refs[0].shape
    pages_per_block = MOBA_BLOCK // page
    nb = n_pages // pages_per_block
    rows = n_heads * t
    page_keys = page * n_heads

    q2 = _heads_to_rows(q_ref[0], n_heads, dh)
    q2b = (q2 * dh ** -0.5).astype(BF16)
    row_head = lax.broadcasted_iota(jnp.int32, (rows, 1), 0) // t
    row_query = lax.broadcasted_iota(jnp.int32, (rows, 1), 0) % t
    slope_rows = jnp.zeros((rows, 1), F32)
    for h in range(n_heads):
        slope_rows = jnp.where(row_head == h, slopes_ref[h], slope_rows)
    lane = lax.broadcasted_iota(jnp.int32, (rows, page_keys), 1)
    page_bias = jnp.where(lane % n_heads == row_head,
                          slope_rows * (row_query - lane // n_heads).astype(F32), -NEG_INF)

    def block_scores(b):
        ksum = jnp.zeros((n_heads, dh), F32)
        s_pages = []
        for pg in range(b * pages_per_block, (b + 1) * pages_per_block):
            kp = k_refs[pg][...]
            ksum = ksum + jnp.sum(kp, axis=0)
            s_pg = _dot_nt(q2b, kp.reshape(page_keys, dh).astype(BF16))
            s_pages.append(s_pg - page_bias - slope_rows * float(past_len - pg * page))
        kmean = ksum / MOBA_BLOCK
        kmean_rows = jnp.concatenate([jnp.broadcast_to(kmean[h:h + 1, :], (t, dh)) for h in range(n_heads)], axis=0)
        return jnp.sum(q2 * kmean_rows, axis=-1, keepdims=True), jnp.concatenate(s_pages, axis=-1)

    route, m_blk, l_blk, acc_blk = [], [], [], []
    ahead = block_scores(0)
    for b in range(nb):
        route_b, s_b = ahead
        if b + 1 < nb:
            ahead = block_scores(b + 1)
        pages = range(b * pages_per_block, (b + 1) * pages_per_block)
        vblk = jnp.concatenate([v_refs[pg][...].reshape(page_keys, dh) for pg in pages], axis=0)
        m_b = jnp.max(s_b, axis=-1, keepdims=True)
        p = jnp.exp(s_b - m_b)
        route.append(route_b)
        m_blk.append(m_b)
        l_blk.append(jnp.sum(p, axis=-1, keepdims=True))
        acc_blk.append(_dot(p.astype(BF16), vblk.astype(BF16)))

    sel = _topk_select_list(route)
    kn2 = _heads_to_rows(kn_ref[0], n_heads, dh)
    vn2 = _heads_to_rows(vn_ref[0], n_heads, dh)
    col = lax.broadcasted_iota(jnp.int32, (rows, rows), 1)
    dist = row_query - col % t
    keep = (col // t == row_head) & (dist >= 0)
    s_own = _dot_nt(q2b, kn2.astype(BF16)) - slope_rows * dist.astype(F32)
    s_own = jnp.where(keep, s_own, NEG_INF)
    m_all = jnp.max(s_own, axis=-1, keepdims=True)
    for b in range(nb):
        m_all = jnp.maximum(m_all, jnp.where(sel[b], m_blk[b], NEG_INF))
    p_own = jnp.where(keep, jnp.exp(s_own - m_all), 0.0)
    den = jnp.sum(p_own, axis=-1, keepdims=True)
    num = _dot(p_own.astype(BF16), vn2.astype(BF16))
    for b in range(nb):
        w_b = jnp.where(sel[b], jnp.exp(m_blk[b] - m_all), 0.0)
        den = den + w_b * l_blk[b]
        num = num + w_b * acc_blk[b]
    out = num / den
    o_ref[...] = jnp.concatenate([out[h * t:(h + 1) * t, :] for h in range(n_heads)], axis=-1)


def _sample_attention(q_src3, k3, v3, cache_k, cache_v, layer, page_table, slopes, *, q_col):
    n_seq, t, _ = k3.shape
    _, _, page, n_heads, dh = cache_k.shape
    width = n_heads * dh
    n_pages = page_table.shape[1]
    assert MOBA_BLOCK % page == 0 and (n_pages * page) % MOBA_BLOCK == 0 and t <= MOBA_BLOCK
    assert q_col % width == 0
    past_len = n_pages * page
    qb0 = q_col // width
    page_specs = [pl.BlockSpec((None, None, page, n_heads, dh), lambda s, pt, pg=pg: (layer, pt[s, pg], 0, 0, 0))
                  for pg in range(n_pages)]
    grid_spec = pltpu.PrefetchScalarGridSpec(
        num_scalar_prefetch=1,
        grid=(n_seq,),
        in_specs=[
            pl.BlockSpec(memory_space=pltpu.SMEM),
            pl.BlockSpec((1, t, width), lambda s, pt: (s, 0, qb0)),
            pl.BlockSpec((1, t, width), lambda s, pt: (s, 0, 0)),
            pl.BlockSpec((1, t, width), lambda s, pt: (s, 0, 0)),
            *page_specs, *page_specs,
        ],
        out_specs=pl.BlockSpec((t, width), lambda s, pt: (s, 0)),
    )
    return pl.pallas_call(
        functools.partial(_sample_attn_kernel, past_len=past_len, n_pages=n_pages),
        grid_spec=grid_spec,
        out_shape=jax.ShapeDtypeStruct((n_seq * t, width), F32),
        compiler_params=_params("parallel"),
        name="sample_attn",
    )(page_table, slopes, q_src3, k3, v3, *([cache_k] * n_pages), *([cache_v] * n_pages))


def _merge_kernel(a_ref, b_ref, wp_ref, wa_ref, ga_ref, gb_ref, o_ref):
    pa = _dot(a_ref[...].astype(BF16), wp_ref[...])
    pb = _dot(b_ref[...].astype(BF16), wa_ref[...])
    merged = jax.nn.sigmoid(ga_ref[...]) * pa + jax.nn.sigmoid(gb_ref[...]) * pb
    o_ref[...] = merged.astype(o_ref.dtype)


def _merge(a_out, b_out, w_pool_up, w_att_up, proj, *, ga_col, gb_col, tm, tn):
    rows, wa = a_out.shape
    wb = b_out.shape[1]
    n = w_pool_up.shape[1]
    ga0, gb0 = ga_col // tn, gb_col // tn
    return pl.pallas_call(
        _merge_kernel,
        grid=(rows // tm, n // tn),
        in_specs=[
            pl.BlockSpec((tm, wa), lambda i, j: (i, 0)),
            pl.BlockSpec((tm, wb), lambda i, j: (i, 0)),
            pl.BlockSpec((wa, tn), lambda i, j: (0, j)),
            pl.BlockSpec((wb, tn), lambda i, j: (0, j)),
            pl.BlockSpec((tm, tn), lambda i, j: (i, ga0 + j)),
            pl.BlockSpec((tm, tn), lambda i, j: (i, gb0 + j)),
        ],
        out_specs=pl.BlockSpec((tm, tn), lambda i, j: (i, j)),
        out_shape=jax.ShapeDtypeStruct((rows, n), BF16),
        compiler_params=_params("parallel", "arbitrary"),
        name="merge",
    )(a_out, b_out, w_pool_up, w_att_up, proj, proj)


def _matmul_residual_kernel(a_ref, w_ref, r_ref, o_ref):
    o_ref[...] = r_ref[...] + _dot(a_ref[...], w_ref[...])


def _matmul_residual(a, w, res, *, tm, tn):
    rows, k = a.shape
    n = w.shape[1]
    return pl.pallas_call(
        _matmul_residual_kernel,
        grid=(rows // tm, n // tn),
        in_specs=[
            pl.BlockSpec((tm, k), lambda i, j: (i, 0)),
            pl.BlockSpec((k, tn), lambda i, j: (0, j)),
            pl.BlockSpec((tm, tn), lambda i, j: (i, j)),
        ],
        out_specs=pl.BlockSpec((tm, tn), lambda i, j: (i, j)),
        out_shape=jax.ShapeDtypeStruct((rows, n), F32),
        compiler_params=_params("parallel", "arbitrary"),
        name="out_proj",
    )(a, w, res)


FFN_CHUNK_COLS = 256


def _gelu_exact(x):
    return 0.5 * x * (1.0 + lax.erf(x * (2.0 ** -0.5)))


def _ffn_kernel(h_ref, g2_ref, wa_ref, wg_ref, cw_ref, cb_ref, wd_ref, gf_ref, *rest,
                seq_rows, tiles_per_seq, per_seq_hist, export_w):
    hist_ref = rest[0] if per_seq_hist else None
    n_in = 1 if per_seq_hist else 0
    y_ref, tail_ref = rest[n_in:n_in + 2]
    wa_out, wg_out, wd_out = rest[n_in + 2:n_in + 5] if export_w else (None, None, None)
    hn_ref, carry_ref = rest[-2:]
    i = pl.program_id(0)
    j = pl.program_id(1)
    tm = h_ref.shape[0]
    tn = wa_ref.shape[1]
    sub = FFN_CHUNK_COLS
    n_chunks = tn // sub

    @pl.when(j == 0)
    def _():
        h = h_ref[...]
        hn_ref[...] = _rmsnorm(h, g2_ref[...]).astype(BF16)
        y_ref[...] = h

    if not per_seq_hist:
        @pl.when(i % tiles_per_seq == 0)
        def _():
            carry_ref[j] = jnp.zeros((SUBLANES, tn), F32)

    hn = hn_ref[...]
    chunks = [slice(c * sub, (c + 1) * sub) for c in range(n_chunks)]
    wa_parts = [wa_ref[:, cols].astype(BF16) for cols in chunks]
    wg_parts = [wg_ref[:, cols].astype(BF16) for cols in chunks]
    wd_parts = [wd_ref[cols, :].astype(BF16) for cols in chunks]
    if export_w:
        for cols, wa_c, wg_c, wd_c in zip(chunks, wa_parts, wg_parts, wd_parts):
            wa_out[:, cols] = wa_c
            wg_out[:, cols] = wg_c
            wd_out[cols, :] = wd_c
    a_parts = [_dot(hn, wa_c) for wa_c in wa_parts]
    g_parts = [_dot(hn, wg_c) for wg_c in wg_parts]
    row = lax.broadcasted_iota(jnp.int32, (tm, sub), 0)
    down = None
    for cols, a, gv, wd_c in zip(chunks, a_parts, g_parts, wd_parts):
        if per_seq_hist:
            n_seq = tm // seq_rows
            hist = hist_ref[:, :, cols]
            prev = [jnp.broadcast_to(hist[:, r:r + 1, :], (n_seq, seq_rows, sub)).reshape(tm, sub)
                    for r in range(CONV_W - 1)]
            t_in_seq = row % seq_rows
            tail_ref[:, :, cols] = a.reshape(n_seq, seq_rows, sub)[:, seq_rows - (CONV_W - 1):, :]
        else:
            carry = carry_ref[j, :, cols]
            first = SUBLANES - (CONV_W - 1)
            prev = [jnp.broadcast_to(carry[first + r:first + r + 1, :], (tm, sub)) for r in range(CONV_W - 1)]
            t_in_seq = row
            tail = a[tm - SUBLANES:, :]
            carry_ref[j, :, cols] = tail
            tail_ref[0, :, cols] = tail
        ac = cb_ref[:, cols] + cw_ref[CONV_W - 1:CONV_W, cols] * a
        for back in range(1, CONV_W):
            shifted = pltpu.roll(a, back, axis=0)
            for r in range(back):
                shifted = jnp.where(t_in_seq == r, prev[CONV_W - 1 - back + r], shifted)
            ac = ac + cw_ref[CONV_W - 1 - back:CONV_W - back, cols] * shifted
        act = (_gelu_exact(ac) * gv).astype(BF16)
        part = _dot(act, wd_c)
        down = part if down is None else down + part
    y_ref[...] += down

    @pl.when(j == pl.num_programs(1) - 1)
    def _():
        y_ref[...] = _rmsnorm(y_ref[...], gf_ref[...])


def _ffn(h, g2, w_up, conv_w, conv_b, w_down, gf, hist, *, seq_rows, tm, tn):
    rows, d = h.shape
    d_ff = w_down.shape[0]
    n_col = d_ff // tn
    n_row = rows // tm
    per_seq_hist = hist is not None
    export_w = not isinstance(w_up, tuple)
    if export_w:
        w_a, w_g, g_off = w_up, w_up, n_col
        assert w_up.dtype == F32 and w_down.dtype == F32
    else:
        (w_a, w_g), g_off = w_up, 0
    if per_seq_hist:
        assert tm % seq_rows == 0 and seq_rows >= CONV_W - 1
        tiles_per_seq = 1
        tail_shape = jax.ShapeDtypeStruct((rows // seq_rows, CONV_W - 1, d_ff), F32)
        tail_spec = pl.BlockSpec((tm // seq_rows, CONV_W - 1, tn), lambda i, j: (i, 0, j))
        hist_args = (hist,)
        hist_specs = [pl.BlockSpec((tm // seq_rows, CONV_W - 1, tn), lambda i, j: (i, 0, j))]
    else:
        assert seq_rows % tm == 0
        tiles_per_seq = seq_rows // tm
        tail_shape = jax.ShapeDtypeStruct((n_row, SUBLANES, d_ff), F32)
        tail_spec = pl.BlockSpec((1, SUBLANES, tn), lambda i, j: (i, 0, j))
        hist_args = ()
        hist_specs = []
    out_specs = [pl.BlockSpec((tm, d), lambda i, j: (i, 0)), tail_spec]
    out_shape = [jax.ShapeDtypeStruct((rows, d), F32), tail_shape]
    if export_w:
        out_specs += [pl.BlockSpec((d, tn), lambda i, j: (0, j)), pl.BlockSpec((d, tn), lambda i, j: (0, j)),
                      pl.BlockSpec((tn, d), lambda i, j: (j, 0))]
        out_shape += [jax.ShapeDtypeStruct((d, d_ff), BF16), jax.ShapeDtypeStruct((d, d_ff), BF16),
                      jax.ShapeDtypeStruct((d_ff, d), BF16)]
    outs = pl.pallas_call(
        functools.partial(_ffn_kernel, seq_rows=seq_rows, tiles_per_seq=tiles_per_seq, per_seq_hist=per_seq_hist,
                          export_w=export_w),
        grid=(n_row, n_col),
        in_specs=[
            pl.BlockSpec((tm, d), lambda i, j: (i, 0), pipeline_mode=pl.Buffered(1 if export_w else 2)),
            pl.BlockSpec((1, d), lambda i, j: (0, 0)),
            pl.BlockSpec((d, tn), lambda i, j: (0, j)),
            pl.BlockSpec((d, tn), lambda i, j: (0, g_off + j)),
            pl.BlockSpec((CONV_W, tn), lambda i, j: (0, j)),
            pl.BlockSpec((1, tn), lambda i, j: (0, j)),
            pl.BlockSpec((tn, d), lambda i, j: (j, 0)),
            pl.BlockSpec((1, d), lambda i, j: (0, 0)),
            *hist_specs,
        ],
        out_specs=out_specs,
        out_shape=out_shape,
        scratch_shapes=[
            pltpu.VMEM((tm, d), BF16),
            pltpu.VMEM((n_col, SUBLANES, tn), F32),
        ],
        compiler_params=_params("arbitrary", "arbitrary"),
        name="ffn",
    )(h, g2, w_a, w_g, conv_w, conv_b, w_down, gf, *hist_args)
    return outs[0], outs[1], (tuple(outs[2:]) if export_w else None)


def _layer(x3, pool_hist, conv_hist, attend, pos0, w, *, tm, in_tn, ffn_tn):
    n_seq, seq_len, d = x3.shape
    rows = n_seq * seq_len
    x = x3.reshape(rows, d)
    w_pool = w["pool_scale"].shape[1]
    w_att = w["w_att_up"].shape[0]
    rest, k, v, w_in_b = _in_proj(x, w["g1"], w["w_in"], k_col=w_pool + w_att, kv_width=w_att, tm=tm, tn=in_tn)
    rest3 = rest.reshape(n_seq, seq_len, -1)
    k3, v3 = k.reshape(n_seq, seq_len, w_att), v.reshape(n_seq, seq_len, w_att)
    q_col, ga_col, gb_col = w_pool, w_pool + w_att, w_pool + w_att + d
    if pool_hist is None:
        a_out = _pool_mix(rest3, None, w["pool_maps"], w["pool_scale"], s_blk=1, l_blk=POOL_TILE_ROWS, pos0=pos0,
                          out_dtype=BF16)
    else:
        a_out = _pool_mix(rest3, pool_hist, w["pool_maps"], w["pool_scale"], s_blk=32, l_blk=seq_len, pos0=pos0,
                          out_dtype=F32)
    b_out = attend(rest3, k3, v3, q_col)
    merged = _merge(a_out, b_out, w["w_pool_up"], w["w_att_up"], rest, ga_col=ga_col, gb_col=gb_col, tm=tm, tn=1024)
    h = _matmul_residual(merged, w["w_out"], x, tm=tm, tn=1024)
    y, tail, ffn_w = _ffn(h, w["g2"], w["w_up"], w["conv_w"], w["conv_b"], w["w_down"], w["gf"], conv_hist,
                          seq_rows=seq_len, tm=tm, tn=ffn_tn)
    return y.reshape(n_seq, seq_len, d), rest3, k3, v3, tail, (w_in_b, ffn_w)


def kernel(x_prompt, x_sample, cache_k, cache_v, state_pool, state_conv, page_table, norm1_g, w_in, pool_maps,
           pool_scale, w_pool_up, w_att_up, w_out, norm2_g, w_up, conv_w, conv_b, w_down, final_norm_g):
    depth = w_in.shape[0]
    assert depth == 1, "single-layer trunk"
    bsz, seq, d = x_prompt.shape
    n_dec, dec_seq, _ = x_sample.shape
    _, n_pool, page, n_heads, dh = cache_k.shape
    assert dh == HEAD_DIM
    w_att = n_heads * dh
    w_pool = pool_scale.shape[1]
    d_ff = w_down.shape[1]
    past_len = page_table.shape[1] * page
    slopes = 2.0 ** (-8.0 * jnp.arange(1, n_heads + 1, dtype=F32) / n_heads)

    l = 0
    w = dict(
        g1=norm1_g[l][None, :], w_in=w_in[l], pool_maps=pool_maps[l].astype(BF16),
        pool_scale=pool_scale[l][None, :], w_pool_up=w_pool_up[l].astype(BF16), w_att_up=w_att_up[l].astype(BF16),
        w_out=w_out[l].astype(BF16), g2=norm2_g[l][None, :], w_up=w_up[l], conv_w=conv_w[l],
        conv_b=conv_b[l][None, :], w_down=w_down[l], gf=final_norm_g[None, :],
    )
    tm = ROW_TILE

    attend_s = lambda q_src3, k3, v3, qc: _sample_attention(q_src3, k3, v3, cache_k, cache_v, l, page_table, slopes,
                                                            q_col=qc)
    y_s, rest_s, k_s, v_s, tail_s, (w_in_b, (w_ua_b, w_ug_b, w_down_b)) = _layer(
        x_sample, state_pool[l], state_conv[l], attend_s, past_len, w, tm=tm, in_tn=512, ffn_tn=256)

    w_p = dict(w, w_in=w_in_b, w_up=(w_ua_b, w_ug_b), w_down=w_down_b)
    attend_p = lambda q_src3, k3, v3, qc: _prompt_attention(q_src3, k3, v3, slopes, q_col=qc)
    y_p, rest_p, k_p, v_p, tail_p, _ = _layer(x_prompt, None, None, attend_p, 0, w_p, tm=tm, in_tn=1024, ffn_tn=512)

    heads = lambda a3: a3.reshape(a3.shape[0], a3.shape[1], n_heads, dh)[None]
    pool_p = rest_p[:, seq - POOL_HIST:, :w_pool][None]
    pool_s = jnp.concatenate([state_pool[l], rest_s[:, :, :w_pool]], axis=1)[:, -POOL_HIST:][None]
    tiles_per_seq = seq // tm
    conv_p = tail_p.reshape(bsz, tiles_per_seq, SUBLANES, d_ff)[:, -1, SUBLANES - (CONV_W - 1):][None]
    conv_s = tail_s[None]
    return (y_p, y_s, heads(k_p), heads(v_p), heads(k_s), heads(v_s), pool_p, pool_s, conv_p, conv_s)
```

```python
import functools

import jax
import jax.numpy as jnp
from jax import lax
from jax.experimental import pallas as pl
from jax.experimental.pallas import tpu as pltpu

POOL_WINDOWS = (2, 4, 8, 16)
POOL_HIST = max(POOL_WINDOWS) - 1
HEAD_DIM = 128
MOBA_BLOCK = 256
MOBA_TOPK = 3
CONV_W = 3
RMS_EPS = 1e-6
NEG_INF = -1e30

SUBLANES = 8
HALO_ROWS = 16
ROW_TILE = 1024
POOL_TILE_ROWS = 512
V7X_VMEM_BYTES = 64 * 1024 * 1024
VMEM_LIMIT_BYTES = 58 * 1024 * 1024

BF16 = jnp.bfloat16
F32 = jnp.float32


def _params(*semantics):
    return pltpu.CompilerParams(dimension_semantics=semantics, vmem_limit_bytes=VMEM_LIMIT_BYTES)


def _rmsnorm(x, g):
    return x * lax.rsqrt(jnp.mean(x * x, axis=-1, keepdims=True) + RMS_EPS) * g


def _dot(a, b):
    return jnp.dot(a, b, preferred_element_type=F32)


def _dot_nt(a, b):
    return lax.dot_general(a, b, (((1,), (1,)), ((), ())), preferred_element_type=F32)


def _split_bf16(x):
    hi = x.astype(BF16)
    lo = (x - hi.astype(F32)).astype(BF16)
    return hi, lo


def _dot_nt_3pass(a, b):
    a_hi, a_lo = _split_bf16(a)
    b_hi, b_lo = _split_bf16(b)
    return _dot_nt(a_hi, b_hi) + (_dot_nt(a_hi, b_lo) + _dot_nt(a_lo, b_hi))


def _in_proj_kernel(x_ref, g_ref, w_ref, rest_ref, k_ref, v_ref, *more, k_tiles, v_tiles, export_w):
    xn_ref = more[-1]
    j = pl.program_id(1)

    @pl.when(j == 0)
    def _():
        xn_ref[...] = _rmsnorm(x_ref[...], g_ref[...]).astype(BF16)

    def project_into(o_ref):
        wb = w_ref[...].astype(BF16)
        if export_w:
            more[0][...] = wb
        o_ref[...] = _dot(xn_ref[...], wb)

    is_k = (j >= k_tiles[0]) & (j < k_tiles[1])
    is_v = (j >= v_tiles[0]) & (j < v_tiles[1])
    pl.when(is_k)(lambda: project_into(k_ref))
    pl.when(is_v)(lambda: project_into(v_ref))
    pl.when(jnp.logical_not(is_k | is_v))(lambda: project_into(rest_ref))


def _in_proj(x, g, w, *, k_col, kv_width, tm, tn):
    rows, d = x.shape
    n = w.shape[1]
    export_w = w.dtype != BF16
    assert k_col % tn == 0 and kv_width % tn == 0
    k0, nkv = k_col // tn, kv_width // tn
    v0, after = k0 + nkv, k0 + 2 * nkv
    rest_map = lambda i, j: (i, jnp.where(j < k0, j, jnp.maximum(j - 2 * nkv, k0 - 1)))
    k_map = lambda i, j: (i, jnp.clip(j - k0, 0, nkv - 1))
    v_map = lambda i, j: (i, jnp.clip(j - v0, 0, nkv - 1))
    out_specs = [pl.BlockSpec((tm, tn), rest_map), pl.BlockSpec((tm, tn), k_map), pl.BlockSpec((tm, tn), v_map)]
    out_shape = [jax.ShapeDtypeStruct((rows, n - 2 * kv_width), F32),
                 jax.ShapeDtypeStruct((rows, kv_width), F32), jax.ShapeDtypeStruct((rows, kv_width), F32)]
    if export_w:
        out_specs.append(pl.BlockSpec((d, tn), lambda i, j: (0, j)))
        out_shape.append(jax.ShapeDtypeStruct((d, n), BF16))
    outs = pl.pallas_call(
        functools.partial(_in_proj_kernel, k_tiles=(k0, v0), v_tiles=(v0, after), export_w=export_w),
        grid=(rows // tm, n // tn),
        in_specs=[
            pl.BlockSpec((tm, d), lambda i, j: (i, 0)),
            pl.BlockSpec((1, d), lambda i, j: (0, 0)),
            pl.BlockSpec((d, tn), lambda i, j: (0, j)),
        ],
        out_specs=out_specs,
        out_shape=out_shape,
        scratch_shapes=[pltpu.VMEM((tm, d), BF16)],
        compiler_params=_params("arbitrary", "arbitrary"),
        name="in_proj",
    )(x, g, w)
    return (*outs, None) if not export_w else tuple(outs)


def _pool_kernel(u_ref, hist_ref, maps_ref, scale_ref, o_ref, ext_ref, *, pos0, zero_first_hist):
    s_blk, l_blk, width = u_ref.shape
    group = width // len(POOL_WINDOWS)
    tile = pl.program_id(1)
    u = u_ref[...]
    ext_ref[:, HALO_ROWS:, :] = u
    hist_rows = hist_ref.shape[1]
    ext_ref[:, HALO_ROWS - hist_rows:HALO_ROWS, :] = hist_ref[...]
    if zero_first_hist:
        @pl.when(tile == 0)
        def _():
            ext_ref[:, 0:HALO_ROWS, :] = jnp.zeros((s_blk, HALO_ROWS, width), F32)

    pos = pos0 + tile * l_blk + lax.broadcasted_iota(jnp.int32, (s_blk, l_blk, group), 1)
    outs = []
    for gi, w in enumerate(POOL_WINDOWS):
        cols = slice(gi * group, (gi + 1) * group)
        win = u[:, :, cols]
        for k in range(1, w):
            win = win + ext_ref[:, HALO_ROWS - k:HALO_ROWS - k + l_blk, cols]
        cnt = jnp.minimum(w, pos + 1).astype(F32)
        d = win / cnt - u[:, :, cols]
        d2 = d.reshape(s_blk * l_blk, group).astype(BF16)
        outs.append(_dot(d2, maps_ref[gi]))
    m = jnp.concatenate(outs, axis=-1) * scale_ref[...]
    o_ref[...] = m.astype(o_ref.dtype)


def _pool_mix(proj3, hist, maps, scale, *, s_blk, l_blk, pos0, out_dtype):
    n_seq, seq_len, _ = proj3.shape
    n_groups, group, _ = maps.shape
    width = n_groups * group
    tiles = seq_len // l_blk
    if hist is None:
        halo_blocks = l_blk // HALO_ROWS
        hist_arr = proj3
        hist_spec = pl.BlockSpec((s_blk, HALO_ROWS, width),
                                 lambda s, t: (s, jnp.maximum(t * halo_blocks - 1, 0), 0))
    else:
        assert tiles == 1
        hist_arr = hist
        hist_spec = pl.BlockSpec((s_blk, hist.shape[1], width), lambda s, t: (s, 0, 0))
    kern = functools.partial(_pool_kernel, pos0=pos0, zero_first_hist=hist is None)
    return pl.pallas_call(
        kern,
        grid=(n_seq // s_blk, tiles),
        in_specs=[
            pl.BlockSpec((s_blk, l_blk, width), lambda s, t: (s, t, 0)),
            hist_spec,
            pl.BlockSpec((n_groups, group, group), lambda s, t: (0, 0, 0)),
            pl.BlockSpec((1, width), lambda s, t: (0, 0)),
        ],
        out_specs=pl.BlockSpec((s_blk * l_blk, width), lambda s, t: (s * tiles + t, 0)),
        out_shape=jax.ShapeDtypeStruct((n_seq * seq_len, width), out_dtype),
        scratch_shapes=[pltpu.VMEM((s_blk, HALO_ROWS + l_blk, width), F32)],
        compiler_params=_params("parallel", "arbitrary"),
        name="pool_mix",
    )(proj3, hist_arr, maps, scale)


def _topk_select(route, n_valid):
    nb, n = route.shape
    blk = lax.broadcasted_iota(jnp.int32, (nb, n), 0)
    rank = jnp.zeros((nb, n), jnp.int32)
    for other in range(nb):
        c = route[other:other + 1, :]
        beats = jnp.where((c > route) | ((c == route) & (other < blk)), 1, 0)
        rank = rank + beats * (other < n_valid).astype(jnp.int32)
    return (blk < n_valid) & (rank < MOBA_TOPK)


def _topk_select_list(scores):
    sel = []
    for j, s_j in enumerate(scores):
        rank = jnp.zeros(s_j.shape, jnp.int32)
        for other, s_o in enumerate(scores):
            if other < j:
                rank = rank + jnp.where(s_o >= s_j, 1, 0)
            elif other > j:
                rank = rank + jnp.where(s_o > s_j, 1, 0)
        sel.append(rank < MOBA_TOPK)
    return sel


KV_TILE = 2 * MOBA_BLOCK
assert KV_TILE == 2 * MOBA_BLOCK
ONES_ROWS = 16
M_FLOOR = -1e20
PROMPT_HEADS_PER_STEP = 4
SCORE_LOOKAHEAD = 4


def _prompt_attn_kernel(slopes_ref, q_ref, k_ref, v_ref, o_ref,
                        kb_ref, vt_ref, kmean_ref, alibi_ref, m_ref, acc_ref):
    hgroup = pl.program_id(1)
    qt = pl.program_id(2)
    n_hb, seq_len, dh = kb_ref.shape
    tq = q_ref.shape[1]
    nb = seq_len // MOBA_BLOCK
    blocks_per_tile = KV_TILE // MOBA_BLOCK
    slopes = [slopes_ref[hgroup * n_hb + hh] for hh in range(n_hb)]
    q_minus_k = (lax.broadcasted_iota(jnp.int32, (KV_TILE, tq), 1)
                 - lax.broadcasted_iota(jnp.int32, (KV_TILE, tq), 0))

    @pl.when(qt == 0)
    def _():
        for hh in range(n_hb):
            cols = slice(hh * dh, (hh + 1) * dh)
            k = k_ref[0, :, cols]
            kb_ref[hh] = k.astype(BF16)
            kmean_ref[hh] = jnp.mean(k.reshape(nb, MOBA_BLOCK, dh), axis=1)
            ones_rows = jnp.where(lax.broadcasted_iota(jnp.int32, (ONES_ROWS, MOBA_BLOCK), 0) == 0, 1.0, 0.0)
            for c in range(nb):
                vt_ref[hh, c, 0:dh, :] = v_ref[0, c * MOBA_BLOCK:(c + 1) * MOBA_BLOCK, cols].T.astype(BF16)
                vt_ref[hh, c, dh:, :] = ones_rows.astype(BF16)
            alibi_ref[hh] = slopes[hh] * q_minus_k.astype(F32)

    blk = lax.broadcasted_iota(jnp.int32, (nb, tq), 0)
    qbs, sel_bits = [], []
    for hh in range(n_hb):
        q = q_ref[0, :, hh * dh:(hh + 1) * dh]
        qbs.append((q * dh ** -0.5).astype(BF16))
        route = _dot_nt_3pass(kmean_ref[hh], q)
        sel = _topk_select(route, qt)
        bits = jnp.sum(jnp.where(sel, jnp.left_shift(1, blk).astype(F32), 0.0), axis=0, keepdims=True)
        sel_bits.append(bits.astype(jnp.int32))

    m_ref[...] = jnp.full(m_ref.shape, M_FLOOR, F32)
    acc_ref[...] = jnp.zeros(acc_ref.shape, F32)

    def block_bias(hh, j):
        bit = jnp.right_shift(sel_bits[hh], j) & 1
        return jnp.broadcast_to(jnp.where(bit == 1, 0.0, NEG_INF), (MOBA_BLOCK, tq))

    def attend(first_blk, n_blk, bias_fn, offset):
        keys = n_blk * MOBA_BLOCK
        start = pl.multiple_of(first_blk * MOBA_BLOCK, MOBA_BLOCK)
        score = lambda hh: _dot_nt(kb_ref[hh, pl.ds(start, keys), :], qbs[hh])
        scores = {hh: score(hh) for hh in range(min(SCORE_LOOKAHEAD, n_hb))}
        for hh in range(n_hb):
            if hh + SCORE_LOOKAHEAD < n_hb:
                scores[hh + SCORE_LOOKAHEAD] = score(hh + SCORE_LOOKAHEAD)
            z = scores.pop(hh) - alibi_ref[hh, 0:keys, :] + bias_fn(hh)
            shift = slopes[hh] * jnp.asarray(offset).astype(F32)
            m_old = m_ref[hh]
            m_new = jnp.maximum(m_old, jnp.max(z, axis=0, keepdims=True) - shift)
            alpha = jnp.exp(m_old - m_new)
            pr = jnp.exp(z - (m_new + shift))
            pv = None
            for b in range(n_blk):
                part = _dot(vt_ref[hh, first_blk + b], pr[b * MOBA_BLOCK:(b + 1) * MOBA_BLOCK].astype(BF16))
                pv = part if pv is None else pv + part
            acc_ref[hh] = alpha * acc_ref[hh] + pv
            m_ref[hh] = m_new

    def past_pair(p, carry):
        first = p * blocks_per_tile
        attend(first, blocks_per_tile,
               lambda hh: jnp.concatenate([block_bias(hh, first + b) for b in range(blocks_per_tile)], axis=0),
               (qt - first) * MOBA_BLOCK)
        return carry

    lax.fori_loop(0, qt // blocks_per_tile, past_pair, 0)

    @pl.when(qt % blocks_per_tile == 1)
    def _():
        attend(qt - 1, 1, lambda hh: block_bias(hh, qt - 1), MOBA_BLOCK)

    causal = jnp.where(q_minus_k[0:MOBA_BLOCK, :] >= 0, 0.0, NEG_INF)
    attend(qt, 1, lambda hh: causal, 0)
    for hh in range(n_hb):
        acc = acc_ref[hh]
        o_ref[:, hh * dh:(hh + 1) * dh] = (acc[0:dh, :] / acc[dh:dh + 1, :]).T.astype(o_ref.dtype)


def _prompt_attention(q_src3, k3, v3, slopes, *, q_col):
    bsz, seq_len, w_att = k3.shape
    n_heads = w_att // HEAD_DIM
    tq = MOBA_BLOCK
    nq = seq_len // tq
    nb = seq_len // MOBA_BLOCK
    hb = PROMPT_HEADS_PER_STEP
    gw = hb * HEAD_DIM
    assert n_heads % hb == 0 and seq_len % KV_TILE == 0 and q_col % gw == 0
    qb0 = q_col // gw
    return pl.pallas_call(
        _prompt_attn_kernel,
        grid=(bsz, n_heads // hb, nq),
        in_specs=[
            pl.BlockSpec(memory_space=pltpu.SMEM),
            pl.BlockSpec((1, tq, gw), lambda b, h, i: (b, i, qb0 + h)),
            pl.BlockSpec((1, seq_len, gw), lambda b, h, i: (b, 0, h)),
            pl.BlockSpec((1, seq_len, gw), lambda b, h, i: (b, 0, h)),
        ],
        out_specs=pl.BlockSpec((tq, gw), lambda b, h, i: (b * nq + i, h)),
        out_shape=jax.ShapeDtypeStruct((bsz * seq_len, n_heads * HEAD_DIM), BF16),
        scratch_shapes=[
            pltpu.VMEM((hb, seq_len, HEAD_DIM), BF16),
            pltpu.VMEM((hb, nb, HEAD_DIM + ONES_ROWS, MOBA_BLOCK), BF16),
            pltpu.VMEM((hb, nb, HEAD_DIM), F32),
            pltpu.VMEM((hb, KV_TILE, tq), F32),
            pltpu.VMEM((hb, 1, tq), F32),
            pltpu.VMEM((hb, HEAD_DIM + ONES_ROWS, tq), F32),
        ],
        compiler_params=_params("parallel", "parallel", "arbitrary"),
        name="prompt_attn",
    )(slopes, q_src3, k3, v3)


def _heads_to_rows(x, n_heads, dh):
    return jnp.concatenate([x[:, h * dh:(h + 1) * dh] for h in range(n_heads)], axis=0)


def _sample_attn_kernel(pt_ref, slopes_ref, q_ref, kn_ref, vn_ref, *rest, past_len, n_pages):
    del pt_ref
    k_refs, v_refs, o_ref = rest[:n_pages], rest[n_pages:2 * n_pages], rest[2 * n_pages]
    t = q_ref.shape[1]
    page, n_heads, dh = k_refs[0].shape
    pages_per_block = MOBA_BLOCK // page
    nb = n_pages // pages_per_block
    rows = n_heads * t
    page_keys = page * n_heads

    q2 = _heads_to_rows(q_ref[0], n_heads, dh)
    q2b = (q2 * dh ** -0.5).astype(BF16)
    row_head = lax.broadcasted_iota(jnp.int32, (rows, 1), 0) // t
    row_query = lax.broadcasted_iota(jnp.int32, (rows, 1), 0) % t
    slope_rows = jnp.zeros((rows, 1), F32)
    for h in range(n_heads):
        slope_rows = jnp.where(row_head == h, slopes_ref[h], slope_rows)
    lane = lax.broadcasted_iota(jnp.int32, (rows, page_keys), 1)
    page_bias = jnp.where(lane % n_heads == row_head,
                          slope_rows * (row_query - lane // n_heads).astype(F32), -NEG_INF)

    def block_scores(b):
        ksum = jnp.zeros((n_heads, dh), F32)
        s_pages = []
        for pg in range(b * pages_per_block, (b + 1) * pages_per_block):
            kp = k_refs[pg][...]
            ksum = ksum + jnp.sum(kp, axis=0)
            s_pg = _dot_nt(q2b, kp.reshape(page_keys, dh).astype(BF16))
            s_pages.append(s_pg - page_bias - slope_rows * float(past_len - pg * page))
        kmean = ksum / MOBA_BLOCK
        kmean_rows = jnp.concatenate([jnp.broadcast_to(kmean[h:h + 1, :], (t, dh)) for h in range(n_heads)], axis=0)
        return jnp.sum(q2 * kmean_rows, axis=-1, keepdims=True), jnp.concatenate(s_pages, axis=-1)

    route, m_blk, l_blk, acc_blk = [], [], [], []
    ahead = block_scores(0)
    for b in range(nb):
        route_b, s_b = ahead
        if b + 1 < nb:
            ahead = block_scores(b + 1)
        pages = range(b * pages_per_block, (b + 1) * pages_per_block)
        vblk = jnp.concatenate([v_refs[pg][...].reshape(page_keys, dh) for pg in pages], axis=0)
        m_b = jnp.max(s_b, axis=-1, keepdims=True)
        p = jnp.exp(s_b - m_b)
        route.append(route_b)
        m_blk.append(m_b)
        l_blk.append(jnp.sum(p, axis=-1, keepdims=True))
        acc_blk.append(_dot(p.astype(BF16), vblk.astype(BF16)))

    sel = _topk_select_list(route)
    kn2 = _heads_to_rows(kn_ref[0], n_heads, dh)
    vn2 = _heads_to_rows(vn_ref[0], n_heads, dh)
    col = lax.broadcasted_iota(jnp.int32, (rows, rows), 1)
    dist = row_query - col % t
    keep = (col // t == row_head) & (dist >= 0)
    s_own = _dot_nt(q2b, kn2.astype(BF16)) - slope_rows * dist.astype(F32)
    s_own = jnp.where(keep, s_own, NEG_INF)
    m_all = jnp.max(s_own, axis=-1, keepdims=True)
    for b in range(nb):
        m_all = jnp.maximum(m_all, jnp.where(sel[b], m_blk[b], NEG_INF))
    p_own = jnp.where(keep, jnp.exp(s_own - m_all), 0.0)
    den = jnp.sum(p_own, axis=-1, keepdims=True)
    num = _dot(p_own.astype(BF16), vn2.astype(BF16))
    for b in range(nb):
        w_b = jnp.where(sel[b], jnp.exp(m_blk[b] - m_all), 0.0)
        den = den + w_b * l_blk[b]
        num = num + w_b * acc_blk[b]
    out = num / den
    o_ref[...] = jnp.concatenate([out[h * t:(h + 1) * t, :] for h in range(n_heads)], axis=-1)


def _sample_attention(q_src3, k3, v3, cache_k, cache_v, layer, page_table, slopes, *, q_col):
    n_seq, t, _ = k3.shape
    _, _, page, n_heads, dh = cache_k.shape
    width = n_heads * dh
    n_pages = page_table.shape[1]
    assert MOBA_BLOCK % page == 0 and (n_pages * page) % MOBA_BLOCK == 0 and t <= MOBA_BLOCK
    assert q_col % width == 0
    past_len = n_pages * page
    qb0 = q_col // width
    page_specs = [pl.BlockSpec((None, None, page, n_heads, dh), lambda s, pt, pg=pg: (layer, pt[s, pg], 0, 0, 0))
                  for pg in range(n_pages)]
    grid_spec = pltpu.PrefetchScalarGridSpec(
        num_scalar_prefetch=1,
        grid=(n_seq,),
        in_specs=[
            pl.BlockSpec(memory_space=pltpu.SMEM),
            pl.BlockSpec((1, t, width), lambda s, pt: (s, 0, qb0)),
            pl.BlockSpec((1, t, width), lambda s, pt: (s, 0, 0)),
            pl.BlockSpec((1, t, width), lambda s, pt: (s, 0, 0)),
            *page_specs, *page_specs,
        ],
        out_specs=pl.BlockSpec((t, width), lambda s, pt: (s, 0)),
    )
    return pl.pallas_call(
        functools.partial(_sample_attn_kernel, past_len=past_len, n_pages=n_pages),
        grid_spec=grid_spec,
        out_shape=jax.ShapeDtypeStruct((n_seq * t, width), F32),
        compiler_params=_params("parallel"),
        name="sample_attn",
    )(page_table, slopes, q_src3, k3, v3, *([cache_k] * n_pages), *([cache_v] * n_pages))


def _merge_kernel(a_ref, b_ref, wp_ref, wa_ref, ga_ref, gb_ref, o_ref):
    pa = _dot(a_ref[...].astype(BF16), wp_ref[...])
    pb = _dot(b_ref[...].astype(BF16), wa_ref[...])
    merged = jax.nn.sigmoid(ga_ref[...]) * pa + jax.nn.sigmoid(gb_ref[...]) * pb
    o_ref[...] = merged.astype(o_ref.dtype)


def _merge(a_out, b_out, w_pool_up, w_att_up, proj, *, ga_col, gb_col, tm, tn):
    rows, wa = a_out.shape
    wb = b_out.shape[1]
    n = w_pool_up.shape[1]
    ga0, gb0 = ga_col // tn, gb_col // tn
    return pl.pallas_call(
        _merge_kernel,
        grid=(rows // tm, n // tn),
        in_specs=[
            pl.BlockSpec((tm, wa), lambda i, j: (i, 0)),
            pl.BlockSpec((tm, wb), lambda i, j: (i, 0)),
            pl.BlockSpec((wa, tn), lambda i, j: (0, j)),
            pl.BlockSpec((wb, tn), lambda i, j: (0, j)),
            pl.BlockSpec((tm, tn), lambda i, j: (i, ga0 + j)),
            pl.BlockSpec((tm, tn), lambda i, j: (i, gb0 + j)),
        ],
        out_specs=pl.BlockSpec((tm, tn), lambda i, j: (i, j)),
        out_shape=jax.ShapeDtypeStruct((rows, n), BF16),
        compiler_params=_params("parallel", "arbitrary"),
        name="merge",
    )(a_out, b_out, w_pool_up, w_att_up, proj, proj)


def _matmul_residual_kernel(a_ref, w_ref, r_ref, o_ref, *w_outs):
    wb = w_ref[...].astype(BF16)
    if w_outs:
        w_outs[0][...] = wb
    o_ref[...] = r_ref[...] + _dot(a_ref[...], wb)


def _matmul_residual(a, w, res, *, tm, tn):
    rows, k = a.shape
    n = w.shape[1]
    export_w = w.dtype != BF16
    out_specs = [pl.BlockSpec((tm, tn), lambda i, j: (i, j))]
    out_shape = [jax.ShapeDtypeStruct((rows, n), F32)]
    if export_w:
        out_specs.append(pl.BlockSpec((k, tn), lambda i, j: (0, j)))
        out_shape.append(jax.ShapeDtypeStruct((k, n), BF16))
    outs = pl.pallas_call(
        _matmul_residual_kernel,
        grid=(rows // tm, n // tn),
        in_specs=[
            pl.BlockSpec((tm, k), lambda i, j: (i, 0)),
            pl.BlockSpec((k, tn), lambda i, j: (0, j)),
            pl.BlockSpec((tm, tn), lambda i, j: (i, j)),
        ],
        out_specs=out_specs,
        out_shape=out_shape,
        compiler_params=_params("arbitrary", "arbitrary"),
        name="out_proj",
    )(a, w, res)
    return outs[0], (outs[1] if export_w else None)


FFN_CHUNK_COLS = 256


def _gelu_exact(x):
    return 0.5 * x * (1.0 + lax.erf(x * (2.0 ** -0.5)))


def _ffn_kernel(h_ref, g2_ref, wa_ref, wg_ref, cw_ref, cb_ref, wd_ref, gf_ref, *rest,
                seq_rows, tiles_per_seq, per_seq_hist, export_w):
    hist_ref = rest[0] if per_seq_hist else None
    n_in = 1 if per_seq_hist else 0
    y_ref, tail_ref = rest[n_in:n_in + 2]
    wa_out, wg_out, wd_out = rest[n_in + 2:n_in + 5] if export_w else (None, None, None)
    hn_ref, carry_ref = rest[-2:]
    i = pl.program_id(0)
    j = pl.program_id(1)
    tm = h_ref.shape[0]
    tn = wa_ref.shape[1]
    sub = FFN_CHUNK_COLS
    n_chunks = tn // sub

    @pl.when(j == 0)
    def _():
        h = h_ref[...]
        hn_ref[...] = _rmsnorm(h, g2_ref[...]).astype(BF16)
        y_ref[...] = h

    if not per_seq_hist:
        @pl.when(i % tiles_per_seq == 0)
        def _():
            carry_ref[j] = jnp.zeros((SUBLANES, tn), F32)

    hn = hn_ref[...]
    chunks = [slice(c * sub, (c + 1) * sub) for c in range(n_chunks)]
    wa_parts = [wa_ref[:, cols].astype(BF16) for cols in chunks]
    wg_parts = [wg_ref[:, cols].astype(BF16) for cols in chunks]
    wd_parts = [wd_ref[cols, :].astype(BF16) for cols in chunks]
    if export_w:
        for cols, wa_c, wg_c, wd_c in zip(chunks, wa_parts, wg_parts, wd_parts):
            wa_out[:, cols] = wa_c
            wg_out[:, cols] = wg_c
            wd_out[cols, :] = wd_c
    a_parts = [_dot(hn, wa_c) for wa_c in wa_parts]
    g_parts = [_dot(hn, wg_c) for wg_c in wg_parts]
    row = lax.broadcasted_iota(jnp.int32, (tm, sub), 0)
    down = None
    for cols, a, gv, wd_c in zip(chunks, a_parts, g_parts, wd_parts):
        if per_seq_hist:
            n_seq = tm // seq_rows
            hist = hist_ref[:, :, cols]
            prev = [jnp.broadcast_to(hist[:, r:r + 1, :], (n_seq, seq_rows, sub)).reshape(tm, sub)
                    for r in range(CONV_W - 1)]
            t_in_seq = row % seq_rows
            tail_ref[:, :, cols] = a.reshape(n_seq, seq_rows, sub)[:, seq_rows - (CONV_W - 1):, :]
        else:
            carry = carry_ref[j, :, cols]
            first = SUBLANES - (CONV_W - 1)
            prev = [jnp.broadcast_to(carry[first + r:first + r + 1, :], (tm, sub)) for r in range(CONV_W - 1)]
            t_in_seq = row
            tail = a[tm - SUBLANES:, :]
            carry_ref[j, :, cols] = tail
            tail_ref[0, :, cols] = tail
        ac = cb_ref[:, cols] + cw_ref[CONV_W - 1:CONV_W, cols] * a
        for back in range(1, CONV_W):
            shifted = pltpu.roll(a, back, axis=0)
            for r in range(back):
                shifted = jnp.where(t_in_seq == r, prev[CONV_W - 1 - back + r], shifted)
            ac = ac + cw_ref[CONV_W - 1 - back:CONV_W - back, cols] * shifted
        act = (_gelu_exact(ac) * gv).astype(BF16)
        part = _dot(act, wd_c)
        down = part if down is None else down + part
    y_ref[...] += down

    @pl.when(j == pl.num_programs(1) - 1)
    def _():
        y_ref[...] = _rmsnorm(y_ref[...], gf_ref[...])


def _ffn(h, g2, w_up, conv_w, conv_b, w_down, gf, hist, *, seq_rows, tm, tn):
    rows, d = h.shape
    d_ff = w_down.shape[0]
    n_col = d_ff // tn
    n_row = rows // tm
    per_seq_hist = hist is not None
    export_w = not isinstance(w_up, tuple)
    if export_w:
        w_a, w_g, g_off = w_up, w_up, n_col
        assert w_up.dtype == F32 and w_down.dtype == F32
    else:
        (w_a, w_g), g_off = w_up, 0
    if per_seq_hist:
        assert tm % seq_rows == 0 and seq_rows >= CONV_W - 1
        tiles_per_seq = 1
        tail_shape = jax.ShapeDtypeStruct((rows // seq_rows, CONV_W - 1, d_ff), F32)
        tail_spec = pl.BlockSpec((tm // seq_rows, CONV_W - 1, tn), lambda i, j: (i, 0, j))
        hist_args = (hist,)
        hist_specs = [pl.BlockSpec((tm // seq_rows, CONV_W - 1, tn), lambda i, j: (i, 0, j))]
    else:
        assert seq_rows % tm == 0
        tiles_per_seq = seq_rows // tm
        tail_shape = jax.ShapeDtypeStruct((n_row, SUBLANES, d_ff), F32)
        tail_spec = pl.BlockSpec((1, SUBLANES, tn), lambda i, j: (i, 0, j))
        hist_args = ()
        hist_specs = []
    out_specs = [pl.BlockSpec((tm, d), lambda i, j: (i, 0)), tail_spec]
    out_shape = [jax.ShapeDtypeStruct((rows, d), F32), tail_shape]
    if export_w:
        out_specs += [pl.BlockSpec((d, tn), lambda i, j: (0, j)), pl.BlockSpec((d, tn), lambda i, j: (0, j)),
                      pl.BlockSpec((tn, d), lambda i, j: (j, 0))]
        out_shape += [jax.ShapeDtypeStruct((d, d_ff), BF16), jax.ShapeDtypeStruct((d, d_ff), BF16),
                      jax.ShapeDtypeStruct((d_ff, d), BF16)]
    outs = pl.pallas_call(
        functools.partial(_ffn_kernel, seq_rows=seq_rows, tiles_per_seq=tiles_per_seq, per_seq_hist=per_seq_hist,
                          export_w=export_w),
        grid=(n_row, n_col),
        in_specs=[
            pl.BlockSpec((tm, d), lambda i, j: (i, 0), pipeline_mode=pl.Buffered(1 if export_w else 2)),
            pl.BlockSpec((1, d), lambda i, j: (0, 0)),
            pl.BlockSpec((d, tn), lambda i, j: (0, j)),
            pl.BlockSpec((d, tn), lambda i, j: (0, g_off + j)),
            pl.BlockSpec((CONV_W, tn), lambda i, j: (0, j)),
            pl.BlockSpec((1, tn), lambda i, j: (0, j)),
            pl.BlockSpec((tn, d), lambda i, j: (j, 0)),
            pl.BlockSpec((1, d), lambda i, j: (0, 0)),
            *hist_specs,
        ],
        out_specs=out_specs,
        out_shape=out_shape,
        scratch_shapes=[
            pltpu.VMEM((tm, d), BF16),
            pltpu.VMEM((n_col, SUBLANES, tn), F32),
        ],
        compiler_params=_params("arbitrary", "arbitrary"),
        name="ffn",
    )(h, g2, w_a, w_g, conv_w, conv_b, w_down, gf, *hist_args)
    return outs[0], outs[1], (tuple(outs[2:]) if export_w else None)


def _layer(x3, pool_hist, conv_hist, attend, pos0, w, *, tm, in_tn, ffn_tn):
    n_seq, seq_len, d = x3.shape
    rows = n_seq * seq_len
    x = x3.reshape(rows, d)
    w_pool = w["pool_scale"].shape[1]
    w_att = w["w_att_up"].shape[0]
    rest, k, v, w_in_b = _in_proj(x, w["g1"], w["w_in"], k_col=w_pool + w_att, kv_width=w_att, tm=tm, tn=in_tn)
    rest3 = rest.reshape(n_seq, seq_len, -1)
    k3, v3 = k.reshape(n_seq, seq_len, w_att), v.reshape(n_seq, seq_len, w_att)
    q_col, ga_col, gb_col = w_pool, w_pool + w_att, w_pool + w_att + d
    if pool_hist is None:
        a_out = _pool_mix(rest3, None, w["pool_maps"], w["pool_scale"], s_blk=1, l_blk=POOL_TILE_ROWS, pos0=pos0,
                          out_dtype=BF16)
    else:
        a_out = _pool_mix(rest3, pool_hist, w["pool_maps"], w["pool_scale"], s_blk=32, l_blk=seq_len, pos0=pos0,
                          out_dtype=F32)
    b_out = attend(rest3, k3, v3, q_col)
    merged = _merge(a_out, b_out, w["w_pool_up"], w["w_att_up"], rest, ga_col=ga_col, gb_col=gb_col, tm=tm, tn=1024)
    h, w_out_b = _matmul_residual(merged, w["w_out"], x, tm=tm, tn=1024)
    y, tail, ffn_w = _ffn(h, w["g2"], w["w_up"], w["conv_w"], w["conv_b"], w["w_down"], w["gf"], conv_hist,
                          seq_rows=seq_len, tm=tm, tn=ffn_tn)
    return y.reshape(n_seq, seq_len, d), rest3, k3, v3, tail, (w_in_b, w_out_b, ffn_w)


def kernel(x_prompt, x_sample, cache_k, cache_v, state_pool, state_conv, page_table, norm1_g, w_in, pool_maps,
           pool_scale, w_pool_up, w_att_up, w_out, norm2_g, w_up, conv_w, conv_b, w_down, final_norm_g):
    depth = w_in.shape[0]
    assert depth == 1, "single-layer trunk"
    bsz, seq, d = x_prompt.shape
    n_dec, dec_seq, _ = x_sample.shape
    _, n_pool, page, n_heads, dh = cache_k.shape
    assert dh == HEAD_DIM
    w_att = n_heads * dh
    w_pool = pool_scale.shape[1]
    d_ff = w_down.shape[1]
    past_len = page_table.shape[1] * page
    slopes = 2.0 ** (-8.0 * jnp.arange(1, n_heads + 1, dtype=F32) / n_heads)

    l = 0
    w = dict(
        g1=norm1_g[l][None, :], w_in=w_in[l], pool_maps=pool_maps[l].astype(BF16),
        pool_scale=pool_scale[l][None, :], w_pool_up=w_pool_up[l].astype(BF16), w_att_up=w_att_up[l].astype(BF16),
        w_out=w_out[l], g2=norm2_g[l][None, :], w_up=w_up[l], conv_w=conv_w[l],
        conv_b=conv_b[l][None, :], w_down=w_down[l], gf=final_norm_g[None, :],
    )
    tm = ROW_TILE

    attend_s = lambda q_src3, k3, v3, qc: _sample_attention(q_src3, k3, v3, cache_k, cache_v, l, page_table, slopes,
                                                            q_col=qc)
    y_s, rest_s, k_s, v_s, tail_s, (w_in_b, w_out_b, (w_ua_b, w_ug_b, w_down_b)) = _layer(
        x_sample, state_pool[l], state_conv[l], attend_s, past_len, w, tm=tm, in_tn=512, ffn_tn=256)

    w_p = dict(w, w_in=w_in_b, w_out=w_out_b, w_up=(w_ua_b, w_ug_b), w_down=w_down_b)
    attend_p = lambda q_src3, k3, v3, qc: _prompt_attention(q_src3, k3, v3, slopes, q_col=qc)
    y_p, rest_p, k_p, v_p, tail_p, _ = _layer(x_prompt, None, None, attend_p, 0, w_p, tm=tm, in_tn=1024, ffn_tn=512)

    heads = lambda a3: a3.reshape(a3.shape[0], a3.shape[1], n_heads, dh)[None]
    pool_p = rest_p[:, seq - POOL_HIST:, :w_pool][None]
    pool_s = jnp.concatenate([state_pool[l], rest_s[:, :, :w_pool]], axis=1)[:, -POOL_HIST:][None]
    tiles_per_seq = seq // tm
    conv_p = tail_p.reshape(bsz, tiles_per_seq, SUBLANES, d_ff)[:, -1, SUBLANES - (CONV_W - 1):][None]
    conv_s = tail_s[None]
    return (y_p, y_s, heads(k_p), heads(v_p), heads(k_s), heads(v_s), pool_p, pool_s, conv_p, conv_s)
```

```python
import functools

import jax
import jax.numpy as jnp
from jax import lax
from jax.experimental import pallas as pl
from jax.experimental.pallas import tpu as pltpu

POOL_WINDOWS = (2, 4, 8, 16)
assert all(w & (w - 1) == 0 for w in POOL_WINDOWS)
POOL_HIST = max(POOL_WINDOWS) - 1
HEAD_DIM = 128
MOBA_BLOCK = 256
MOBA_TOPK = 3
CONV_W = 3
RMS_EPS = 1e-6
NEG_INF = -1e30

SUBLANES = 8
HALO_ROWS = 16
ROW_TILE = 1024
POOL_TILE_ROWS = 512
V7X_VMEM_BYTES = 64 * 1024 * 1024
VMEM_LIMIT_BYTES = 58 * 1024 * 1024

BF16 = jnp.bfloat16
F32 = jnp.float32


def _params(*semantics):
    return pltpu.CompilerParams(dimension_semantics=semantics, vmem_limit_bytes=VMEM_LIMIT_BYTES)


def _rmsnorm(x, g):
    return x * lax.rsqrt(jnp.mean(x * x, axis=-1, keepdims=True) + RMS_EPS) * g


def _dot(a, b):
    return jnp.dot(a, b, preferred_element_type=F32)


def _dot_nt(a, b):
    return lax.dot_general(a, b, (((1,), (1,)), ((), ())), preferred_element_type=F32)


def _split_bf16(x):
    hi = x.astype(BF16)
    lo = (x - hi.astype(F32)).astype(BF16)
    return hi, lo


def _dot_nt_3pass(a, b):
    a_hi, a_lo = _split_bf16(a)
    b_hi, b_lo = _split_bf16(b)
    return _dot_nt(a_hi, b_hi) + (_dot_nt(a_hi, b_lo) + _dot_nt(a_lo, b_hi))


def _in_proj_kernel(x_ref, g_ref, w_ref, rest_ref, k_ref, v_ref, *more, k_tiles, v_tiles, export_w):
    xn_ref = more[-1]
    j = pl.program_id(1)

    @pl.when(j == 0)
    def _():
        xn_ref[...] = _rmsnorm(x_ref[...], g_ref[...]).astype(BF16)

    def project_into(o_ref):
        wb = w_ref[...].astype(BF16)
        if export_w:
            more[0][...] = wb
        o_ref[...] = _dot(xn_ref[...], wb)

    is_k = (j >= k_tiles[0]) & (j < k_tiles[1])
    is_v = (j >= v_tiles[0]) & (j < v_tiles[1])
    pl.when(is_k)(lambda: project_into(k_ref))
    pl.when(is_v)(lambda: project_into(v_ref))
    pl.when(jnp.logical_not(is_k | is_v))(lambda: project_into(rest_ref))


def _in_proj(x, g, w, *, k_col, kv_width, tm, tn):
    rows, d = x.shape
    n = w.shape[1]
    export_w = w.dtype != BF16
    assert k_col % tn == 0 and kv_width % tn == 0
    k0, nkv = k_col // tn, kv_width // tn
    v0, after = k0 + nkv, k0 + 2 * nkv
    rest_map = lambda i, j: (i, jnp.where(j < k0, j, jnp.maximum(j - 2 * nkv, k0 - 1)))
    k_map = lambda i, j: (i, jnp.clip(j - k0, 0, nkv - 1))
    v_map = lambda i, j: (i, jnp.clip(j - v0, 0, nkv - 1))
    out_specs = [pl.BlockSpec((tm, tn), rest_map), pl.BlockSpec((tm, tn), k_map), pl.BlockSpec((tm, tn), v_map)]
    out_shape = [jax.ShapeDtypeStruct((rows, n - 2 * kv_width), F32),
                 jax.ShapeDtypeStruct((rows, kv_width), F32), jax.ShapeDtypeStruct((rows, kv_width), F32)]
    if export_w:
        out_specs.append(pl.BlockSpec((d, tn), lambda i, j: (0, j)))
        out_shape.append(jax.ShapeDtypeStruct((d, n), BF16))
    outs = pl.pallas_call(
        functools.partial(_in_proj_kernel, k_tiles=(k0, v0), v_tiles=(v0, after), export_w=export_w),
        grid=(rows // tm, n // tn),
        in_specs=[
            pl.BlockSpec((tm, d), lambda i, j: (i, 0)),
            pl.BlockSpec((1, d), lambda i, j: (0, 0)),
            pl.BlockSpec((d, tn), lambda i, j: (0, j)),
        ],
        out_specs=out_specs,
        out_shape=out_shape,
        scratch_shapes=[pltpu.VMEM((tm, d), BF16)],
        compiler_params=_params("arbitrary", "arbitrary"),
        name="in_proj",
    )(x, g, w)
    return (*outs, None) if not export_w else tuple(outs)


def _pool_kernel(u_ref, hist_ref, maps_ref, scale_ref, o_ref, ext_ref, *, pos0, zero_first_hist):
    s_blk, l_blk, width = u_ref.shape
    group = width // len(POOL_WINDOWS)
    tile = pl.program_id(1)
    u = u_ref[...]
    ext_ref[:, HALO_ROWS:, :] = u
    hist_rows = hist_ref.shape[1]
    ext_ref[:, HALO_ROWS - hist_rows:HALO_ROWS, :] = hist_ref[...]
    if hist_rows < HALO_ROWS:
        ext_ref[:, 0:HALO_ROWS - hist_rows, :] = jnp.zeros((s_blk, HALO_ROWS - hist_rows, width), F32)
    if zero_first_hist:
        @pl.when(tile == 0)
        def _():
            ext_ref[:, 0:HALO_ROWS, :] = jnp.zeros((s_blk, HALO_ROWS, width), F32)

    pos = pos0 + tile * l_blk + lax.broadcasted_iota(jnp.int32, (s_blk, l_blk, group), 1)
    outs = []
    for gi, w in enumerate(POOL_WINDOWS):
        cols = slice(gi * group, (gi + 1) * group)
        acc = ext_ref[:, :, cols]
        span = 1
        while span < w:
            acc = acc + pltpu.roll(acc, span, axis=1)
            span *= 2
        win = acc[:, HALO_ROWS:, :]
        cnt = jnp.minimum(w, pos + 1).astype(F32)
        d = win / cnt - u[:, :, cols]
        d2 = d.reshape(s_blk * l_blk, group).astype(BF16)
        outs.append(_dot(d2, maps_ref[gi]))
    m = jnp.concatenate(outs, axis=-1) * scale_ref[...]
    o_ref[...] = m.astype(o_ref.dtype)


def _pool_mix(proj3, hist, maps, scale, *, s_blk, l_blk, pos0, out_dtype):
    n_seq, seq_len, _ = proj3.shape
    n_groups, group, _ = maps.shape
    width = n_groups * group
    tiles = seq_len // l_blk
    if hist is None:
        halo_blocks = l_blk // HALO_ROWS
        hist_arr = proj3
        hist_spec = pl.BlockSpec((s_blk, HALO_ROWS, width),
                                 lambda s, t: (s, jnp.maximum(t * halo_blocks - 1, 0), 0))
    else:
        assert tiles == 1
        hist_arr = hist
        hist_spec = pl.BlockSpec((s_blk, hist.shape[1], width), lambda s, t: (s, 0, 0))
    kern = functools.partial(_pool_kernel, pos0=pos0, zero_first_hist=hist is None)
    return pl.pallas_call(
        kern,
        grid=(n_seq // s_blk, tiles),
        in_specs=[
            pl.BlockSpec((s_blk, l_blk, width), lambda s, t: (s, t, 0)),
            hist_spec,
            pl.BlockSpec((n_groups, group, group), lambda s, t: (0, 0, 0)),
            pl.BlockSpec((1, width), lambda s, t: (0, 0)),
        ],
        out_specs=pl.BlockSpec((s_blk * l_blk, width), lambda s, t: (s * tiles + t, 0)),
        out_shape=jax.ShapeDtypeStruct((n_seq * seq_len, width), out_dtype),
        scratch_shapes=[pltpu.VMEM((s_blk, HALO_ROWS + l_blk, width), F32)],
        compiler_params=_params("parallel", "arbitrary"),
        name="pool_mix",
    )(proj3, hist_arr, maps, scale)


def _topk_select(route, n_valid):
    nb, n = route.shape
    blk = lax.broadcasted_iota(jnp.int32, (nb, n), 0)
    rank = jnp.zeros((nb, n), jnp.int32)
    for other in range(nb):
        c = route[other:other + 1, :]
        beats = jnp.where((c > route) | ((c == route) & (other < blk)), 1, 0)
        rank = rank + beats * (other < n_valid).astype(jnp.int32)
    return (blk < n_valid) & (rank < MOBA_TOPK)


def _topk_select_list(scores):
    sel = []
    for j, s_j in enumerate(scores):
        rank = jnp.zeros(s_j.shape, jnp.int32)
        for other, s_o in enumerate(scores):
            if other < j:
                rank = rank + jnp.where(s_o >= s_j, 1, 0)
            elif other > j:
                rank = rank + jnp.where(s_o > s_j, 1, 0)
        sel.append(rank < MOBA_TOPK)
    return sel


KV_TILE = 2 * MOBA_BLOCK
assert KV_TILE == 2 * MOBA_BLOCK
ONES_ROWS = 16
M_FLOOR = -1e20
PROMPT_HEADS_PER_STEP = 4
SCORE_LOOKAHEAD = 4


def _prompt_attn_kernel(slopes_ref, q_ref, k_ref, v_ref, o_ref,
                        kb_ref, vt_ref, kmean_ref, alibi_ref, m_ref, acc_ref):
    hgroup = pl.program_id(1)
    qt = pl.program_id(2)
    n_hb, seq_len, dh = kb_ref.shape
    tq = q_ref.shape[1]
    nb = seq_len // MOBA_BLOCK
    blocks_per_tile = KV_TILE // MOBA_BLOCK
    slopes = [slopes_ref[hgroup * n_hb + hh] for hh in range(n_hb)]
    q_minus_k = (lax.broadcasted_iota(jnp.int32, (KV_TILE, tq), 1)
                 - lax.broadcasted_iota(jnp.int32, (KV_TILE, tq), 0))

    @pl.when(qt == 0)
    def _():
        for hh in range(n_hb):
            cols = slice(hh * dh, (hh + 1) * dh)
            k = k_ref[0, :, cols]
            kb_ref[hh] = k.astype(BF16)
            kmean_ref[hh] = jnp.mean(k.reshape(nb, MOBA_BLOCK, dh), axis=1)
            ones_rows = jnp.where(lax.broadcasted_iota(jnp.int32, (ONES_ROWS, MOBA_BLOCK), 0) == 0, 1.0, 0.0)
            for c in range(nb):
                vt_ref[hh, c, 0:dh, :] = v_ref[0, c * MOBA_BLOCK:(c + 1) * MOBA_BLOCK, cols].T.astype(BF16)
                vt_ref[hh, c, dh:, :] = ones_rows.astype(BF16)
            alibi_ref[hh] = slopes[hh] * q_minus_k.astype(F32)

    blk = lax.broadcasted_iota(jnp.int32, (nb, tq), 0)
    qbs, sel_bits = [], []
    for hh in range(n_hb):
        q = q_ref[0, :, hh * dh:(hh + 1) * dh]
        qbs.append((q * dh ** -0.5).astype(BF16))
        route = _dot_nt_3pass(kmean_ref[hh], q)
        sel = _topk_select(route, qt)
        bits = jnp.sum(jnp.where(sel, jnp.left_shift(1, blk).astype(F32), 0.0), axis=0, keepdims=True)
        sel_bits.append(bits.astype(jnp.int32))

    m_ref[...] = jnp.full(m_ref.shape, M_FLOOR, F32)
    acc_ref[...] = jnp.zeros(acc_ref.shape, F32)

    def block_bias(hh, j):
        bit = jnp.right_shift(sel_bits[hh], j) & 1
        return jnp.broadcast_to(jnp.where(bit == 1, 0.0, NEG_INF), (MOBA_BLOCK, tq))

    def attend(first_blk, n_blk, bias_fn, offset):
        keys = n_blk * MOBA_BLOCK
        start = pl.multiple_of(first_blk * MOBA_BLOCK, MOBA_BLOCK)
        score = lambda hh: _dot_nt(kb_ref[hh, pl.ds(start, keys), :], qbs[hh])
        scores = {hh: score(hh) for hh in range(min(SCORE_LOOKAHEAD, n_hb))}
        for hh in range(n_hb):
            if hh + SCORE_LOOKAHEAD < n_hb:
                scores[hh + SCORE_LOOKAHEAD] = score(hh + SCORE_LOOKAHEAD)
            z = scores.pop(hh) - alibi_ref[hh, 0:keys, :] + bias_fn(hh)
            shift = slopes[hh] * jnp.asarray(offset).astype(F32)
            m_old = m_ref[hh]
            m_new = jnp.maximum(m_old, jnp.max(z, axis=0, keepdims=True) - shift)
            alpha = jnp.exp(m_old - m_new)
            pr = jnp.exp(z - (m_new + shift))
            pv = None
            for b in range(n_blk):
                part = _dot(vt_ref[hh, first_blk + b], pr[b * MOBA_BLOCK:(b + 1) * MOBA_BLOCK].astype(BF16))
                pv = part if pv is None else pv + part
            acc_ref[hh] = alpha * acc_ref[hh] + pv
            m_ref[hh] = m_new

    def past_pair(p, carry):
        first = p * blocks_per_tile
        attend(first, blocks_per_tile,
               lambda hh: jnp.concatenate([block_bias(hh, first + b) for b in range(blocks_per_tile)], axis=0),
               (qt - first) * MOBA_BLOCK)
        return carry

    lax.fori_loop(0, qt // blocks_per_tile, past_pair, 0)

    @pl.when(qt % blocks_per_tile == 1)
    def _():
        attend(qt - 1, 1, lambda hh: block_bias(hh, qt - 1), MOBA_BLOCK)

    causal = jnp.where(q_minus_k[0:MOBA_BLOCK, :] >= 0, 0.0, NEG_INF)
    attend(qt, 1, lambda hh: causal, 0)
    for hh in range(n_hb):
        acc = acc_ref[hh]
        o_ref[:, hh * dh:(hh + 1) * dh] = (acc[0:dh, :] / acc[dh:dh + 1, :]).T.astype(o_ref.dtype)


def _prompt_attention(q_src3, k3, v3, slopes, *, q_col):
    bsz, seq_len, w_att = k3.shape
    n_heads = w_att // HEAD_DIM
    tq = MOBA_BLOCK
    nq = seq_len // tq
    nb = seq_len // MOBA_BLOCK
    hb = PROMPT_HEADS_PER_STEP
    gw = hb * HEAD_DIM
    assert n_heads % hb == 0 and seq_len % KV_TILE == 0 and q_col % gw == 0
    qb0 = q_col // gw
    return pl.pallas_call(
        _prompt_attn_kernel,
        grid=(bsz, n_heads // hb, nq),
        in_specs=[
            pl.BlockSpec(memory_space=pltpu.SMEM),
            pl.BlockSpec((1, tq, gw), lambda b, h, i: (b, i, qb0 + h)),
            pl.BlockSpec((1, seq_len, gw), lambda b, h, i: (b, 0, h)),
            pl.BlockSpec((1, seq_len, gw), lambda b, h, i: (b, 0, h)),
        ],
        out_specs=pl.BlockSpec((tq, gw), lambda b, h, i: (b * nq + i, h)),
        out_shape=jax.ShapeDtypeStruct((bsz * seq_len, n_heads * HEAD_DIM), BF16),
        scratch_shapes=[
            pltpu.VMEM((hb, seq_len, HEAD_DIM), BF16),
            pltpu.VMEM((hb, nb, HEAD_DIM + ONES_ROWS, MOBA_BLOCK), BF16),
            pltpu.VMEM((hb, nb, HEAD_DIM), F32),
            pltpu.VMEM((hb, KV_TILE, tq), F32),
            pltpu.VMEM((hb, 1, tq), F32),
            pltpu.VMEM((hb, HEAD_DIM + ONES_ROWS, tq), F32),
        ],
        compiler_params=_params("parallel", "parallel", "arbitrary"),
        name="prompt_attn",
    )(slopes, q_src3, k3, v3)


def _heads_to_rows(x, n_heads, dh):
    return jnp.concatenate([x[:, h * dh:(h + 1) * dh] for h in range(n_heads)], axis=0)


def _sample_attn_kernel(pt_ref, slopes_ref, q_ref, kn_ref, vn_ref, *rest, past_len, n_pages):
    del pt_ref
    k_refs, v_refs, o_ref = rest[:n_pages], rest[n_pages:2 * n_pages], rest[2 * n_pages]
    t = q_ref.shape[1]
    page, n_heads, dh = k_refs[0].shape
    pages_per_block = MOBA_BLOCK // page
    nb = n_pages // pages_per_block
    rows = n_heads * t
    page_keys = page * n_heads

    q2 = _heads_to_rows(q_ref[0], n_heads, dh)
    q2b = (q2 * dh ** -0.5).astype(BF16)
    row_head = lax.broadcasted_iota(jnp.int32, (rows, 1), 0) // t
    row_query = lax.broadcasted_iota(jnp.int32, (rows, 1), 0) % t
    slope_rows = jnp.zeros((rows, 1), F32)
    for h in range(n_heads):
        slope_rows = jnp.where(row_head == h, slopes_ref[h], slope_rows)
    lane = lax.broadcasted_iota(jnp.int32, (rows, page_keys), 1)
    page_bias = jnp.where(lane % n_heads == row_head,
                          slope_rows * (row_query - lane // n_heads).astype(F32), -NEG_INF)

    def block_scores(b):
        ksum = jnp.zeros((n_heads, dh), F32)
        s_pages = []
        for pg in range(b * pages_per_block, (b + 1) * pages_per_block):
            kp = k_refs[pg][...]
            ksum = ksum + jnp.sum(kp, axis=0)
            s_pg = _dot_nt(q2b, kp.reshape(page_keys, dh).astype(BF16))
            s_pages.append(s_pg - page_bias - slope_rows * float(past_len - pg * page))
        kmean = ksum / MOBA_BLOCK
        kmean_rows = jnp.concatenate([jnp.broadcast_to(kmean[h:h + 1, :], (t, dh)) for h in range(n_heads)], axis=0)
        return jnp.sum(q2 * kmean_rows, axis=-1, keepdims=True), jnp.concatenate(s_pages, axis=-1)

    route, m_blk, l_blk, acc_blk = [], [], [], []
    ahead = block_scores(0)
    for b in range(nb):
        route_b, s_b = ahead
        if b + 1 < nb:
            ahead = block_scores(b + 1)
        pages = range(b * pages_per_block, (b + 1) * pages_per_block)
        vblk = jnp.concatenate([v_refs[pg][...].reshape(page_keys, dh) for pg in pages], axis=0)
        m_b = jnp.max(s_b, axis=-1, keepdims=True)
        p = jnp.exp(s_b - m_b)
        route.append(route_b)
        m_blk.append(m_b)
        l_blk.append(jnp.sum(p, axis=-1, keepdims=True))
        acc_blk.append(_dot(p.astype(BF16), vblk.astype(BF16)))

    sel = _topk_select_list(route)
    kn2 = _heads_to_rows(kn_ref[0], n_heads, dh)
    vn2 = _heads_to_rows(vn_ref[0], n_heads, dh)
    col = lax.broadcasted_iota(jnp.int32, (rows, rows), 1)
    dist = row_query - col % t
    keep = (col // t == row_head) & (dist >= 0)
    s_own = _dot_nt(q2b, kn2.astype(BF16)) - slope_rows * dist.astype(F32)
    s_own = jnp.where(keep, s_own, NEG_INF)
    m_all = jnp.max(s_own, axis=-1, keepdims=True)
    for b in range(nb):
        m_all = jnp.maximum(m_all, jnp.where(sel[b], m_blk[b], NEG_INF))
    p_own = jnp.where(keep, jnp.exp(s_own - m_all), 0.0)
    den = jnp.sum(p_own, axis=-1, keepdims=True)
    num = _dot(p_own.astype(BF16), vn2.astype(BF16))
    for b in range(nb):
        w_b = jnp.where(sel[b], jnp.exp(m_blk[b] - m_all), 0.0)
        den = den + w_b * l_blk[b]
        num = num + w_b * acc_blk[b]
    out = num / den
    o_ref[...] = jnp.concatenate([out[h * t:(h + 1) * t, :] for h in range(n_heads)], axis=-1)


def _sample_attention(q_src3, k3, v3, cache_k, cache_v, layer, page_table, slopes, *, q_col):
    n_seq, t, _ = k3.shape
    _, _, page, n_heads, dh = cache_k.shape
    width = n_heads * dh
    n_pages = page_table.shape[1]
    assert MOBA_BLOCK % page == 0 and (n_pages * page) % MOBA_BLOCK == 0 and t <= MOBA_BLOCK
    assert q_col % width == 0
    past_len = n_pages * page
    qb0 = q_col // width
    page_specs = [pl.BlockSpec((None, None, page, n_heads, dh), lambda s, pt, pg=pg: (layer, pt[s, pg], 0, 0, 0))
                  for pg in range(n_pages)]
    grid_spec = pltpu.PrefetchScalarGridSpec(
        num_scalar_prefetch=1,
        grid=(n_seq,),
        in_specs=[
            pl.BlockSpec(memory_space=pltpu.SMEM),
            pl.BlockSpec((1, t, width), lambda s, pt: (s, 0, qb0)),
            pl.BlockSpec((1, t, width), lambda s, pt: (s, 0, 0)),
            pl.BlockSpec((1, t, width), lambda s, pt: (s, 0, 0)),
            *page_specs, *page_specs,
        ],
        out_specs=pl.BlockSpec((t, width), lambda s, pt: (s, 0)),
    )
    return pl.pallas_call(
        functools.partial(_sample_attn_kernel, past_len=past_len, n_pages=n_pages),
        grid_spec=grid_spec,
        out_shape=jax.ShapeDtypeStruct((n_seq * t, width), F32),
        compiler_params=_params("parallel"),
        name="sample_attn",
    )(page_table, slopes, q_src3, k3, v3, *([cache_k] * n_pages), *([cache_v] * n_pages))


def _merge_kernel(a_ref, b_ref, wp_ref, wa_ref, ga_ref, gb_ref, o_ref):
    pa = _dot(a_ref[...].astype(BF16), wp_ref[...])
    pb = _dot(b_ref[...].astype(BF16), wa_ref[...])
    merged = jax.nn.sigmoid(ga_ref[...]) * pa + jax.nn.sigmoid(gb_ref[...]) * pb
    o_ref[...] = merged.astype(o_ref.dtype)


def _merge(a_out, b_out, w_pool_up, w_att_up, proj, *, ga_col, gb_col, tm, tn):
    rows, wa = a_out.shape
    wb = b_out.shape[1]
    n = w_pool_up.shape[1]
    ga0, gb0 = ga_col // tn, gb_col // tn
    return pl.pallas_call(
        _merge_kernel,
        grid=(rows // tm, n // tn),
        in_specs=[
            pl.BlockSpec((tm, wa), lambda i, j: (i, 0)),
            pl.BlockSpec((tm, wb), lambda i, j: (i, 0)),
            pl.BlockSpec((wa, tn), lambda i, j: (0, j)),
            pl.BlockSpec((wb, tn), lambda i, j: (0, j)),
            pl.BlockSpec((tm, tn), lambda i, j: (i, ga0 + j)),
            pl.BlockSpec((tm, tn), lambda i, j: (i, gb0 + j)),
        ],
        out_specs=pl.BlockSpec((tm, tn), lambda i, j: (i, j)),
        out_shape=jax.ShapeDtypeStruct((rows, n), BF16),
        compiler_params=_params("parallel", "arbitrary"),
        name="merge",
    )(a_out, b_out, w_pool_up, w_att_up, proj, proj)


def _matmul_residual_kernel(a_ref, w_ref, r_ref, o_ref, *w_outs):
    wb = w_ref[...].astype(BF16)
    if w_outs:
        w_outs[0][...] = wb
    o_ref[...] = r_ref[...] + _dot(a_ref[...], wb)


def _matmul_residual(a, w, res, *, tm, tn):
    rows, k = a.shape
    n = w.shape[1]
    export_w = w.dtype != BF16
    out_specs = [pl.BlockSpec((tm, tn), lambda i, j: (i, j))]
    out_shape = [jax.ShapeDtypeStruct((rows, n), F32)]
    if export_w:
        out_specs.append(pl.BlockSpec((k, tn), lambda i, j: (0, j)))
        out_shape.append(jax.ShapeDtypeStruct((k, n), BF16))
    outs = pl.pallas_call(
        _matmul_residual_kernel,
        grid=(rows // tm, n // tn),
        in_specs=[
            pl.BlockSpec((tm, k), lambda i, j: (i, 0)),
            pl.BlockSpec((k, tn), lambda i, j: (0, j)),
            pl.BlockSpec((tm, tn), lambda i, j: (i, j)),
        ],
        out_specs=out_specs,
        out_shape=out_shape,
        compiler_params=_params("arbitrary", "arbitrary"),
        name="out_proj",
    )(a, w, res)
    return outs[0], (outs[1] if export_w else None)


FFN_CHUNK_COLS = 256


def _gelu_exact(x):
    return 0.5 * x * (1.0 + lax.erf(x * (2.0 ** -0.5)))


def _ffn_kernel(h_ref, g2_ref, wa_ref, wg_ref, cw_ref, cb_ref, wd_ref, gf_ref, *rest,
                seq_rows, tiles_per_seq, per_seq_hist, export_w):
    hist_ref = rest[0] if per_seq_hist else None
    n_in = 1 if per_seq_hist else 0
    y_ref, tail_ref = rest[n_in:n_in + 2]
    wa_out, wg_out, wd_out = rest[n_in + 2:n_in + 5] if export_w else (None, None, None)
    hn_ref, carry_ref = rest[-2:]
    i = pl.program_id(0)
    j = pl.program_id(1)
    tm = h_ref.shape[0]
    tn = wa_ref.shape[1]
    sub = FFN_CHUNK_COLS
    n_chunks = tn // sub

    @pl.when(j == 0)
    def _():
        h = h_ref[...]
        hn_ref[...] = _rmsnorm(h, g2_ref[...]).astype(BF16)
        y_ref[...] = h

    if not per_seq_hist:
        @pl.when(i % tiles_per_seq == 0)
        def _():
            carry_ref[j] = jnp.zeros((SUBLANES, tn), F32)

    hn = hn_ref[...]
    chunks = [slice(c * sub, (c + 1) * sub) for c in range(n_chunks)]
    wa_parts = [wa_ref[:, cols].astype(BF16) for cols in chunks]
    wg_parts = [wg_ref[:, cols].astype(BF16) for cols in chunks]
    wd_parts = [wd_ref[cols, :].astype(BF16) for cols in chunks]
    if export_w:
        for cols, wa_c, wg_c, wd_c in zip(chunks, wa_parts, wg_parts, wd_parts):
            wa_out[:, cols] = wa_c
            wg_out[:, cols] = wg_c
            wd_out[cols, :] = wd_c
    a_parts = [_dot(hn, wa_c) for wa_c in wa_parts]
    g_parts = [_dot(hn, wg_c) for wg_c in wg_parts]
    row = lax.broadcasted_iota(jnp.int32, (tm, sub), 0)
    down = None
    for cols, a, gv, wd_c in zip(chunks, a_parts, g_parts, wd_parts):
        if per_seq_hist:
            n_seq = tm // seq_rows
            hist = hist_ref[:, :, cols]
            prev = [jnp.broadcast_to(hist[:, r:r + 1, :], (n_seq, seq_rows, sub)).reshape(tm, sub)
                    for r in range(CONV_W - 1)]
            t_in_seq = row % seq_rows
            tail_ref[:, :, cols] = a.reshape(n_seq, seq_rows, sub)[:, seq_rows - (CONV_W - 1):, :]
        else:
            carry = carry_ref[j, :, cols]
            first = SUBLANES - (CONV_W - 1)
            prev = [jnp.broadcast_to(carry[first + r:first + r + 1, :], (tm, sub)) for r in range(CONV_W - 1)]
            t_in_seq = row
            tail = a[tm - SUBLANES:, :]
            carry_ref[j, :, cols] = tail
            tail_ref[0, :, cols] = tail
        ac = cb_ref[:, cols] + cw_ref[CONV_W - 1:CONV_W, cols] * a
        for back in range(1, CONV_W):
            shifted = pltpu.roll(a, back, axis=0)
            for r in range(back):
                shifted = jnp.where(t_in_seq == r, prev[CONV_W - 1 - back + r], shifted)
            ac = ac + cw_ref[CONV_W - 1 - back:CONV_W - back, cols] * shifted
        act = (_gelu_exact(ac) * gv).astype(BF16)
        part = _dot(act, wd_c)
        down = part if down is None else down + part
    y_ref[...] += down

    @pl.when(j == pl.num_programs(1) - 1)
    def _():
        y_ref[...] = _rmsnorm(y_ref[...], gf_ref[...])


def _ffn(h, g2, w_up, conv_w, conv_b, w_down, gf, hist, *, seq_rows, tm, tn):
    rows, d = h.shape
    d_ff = w_down.shape[0]
    n_col = d_ff // tn
    n_row = rows // tm
    per_seq_hist = hist is not None
    export_w = not isinstance(w_up, tuple)
    if export_w:
        w_a, w_g, g_off = w_up, w_up, n_col
        assert w_up.dtype == F32 and w_down.dtype == F32
    else:
        (w_a, w_g), g_off = w_up, 0
    if per_seq_hist:
        assert tm % seq_rows == 0 and seq_rows >= CONV_W - 1
        tiles_per_seq = 1
        tail_shape = jax.ShapeDtypeStruct((rows // seq_rows, CONV_W - 1, d_ff), F32)
        tail_spec = pl.BlockSpec((tm // seq_rows, CONV_W - 1, tn), lambda i, j: (i, 0, j))
        hist_args = (hist,)
        hist_specs = [pl.BlockSpec((tm // seq_rows, CONV_W - 1, tn), lambda i, j: (i, 0, j))]
    else:
        assert seq_rows % tm == 0
        tiles_per_seq = seq_rows // tm
        tail_shape = jax.ShapeDtypeStruct((n_row, SUBLANES, d_ff), F32)
        tail_spec = pl.BlockSpec((1, SUBLANES, tn), lambda i, j: (i, 0, j))
        hist_args = ()
        hist_specs = []
    out_specs = [pl.BlockSpec((tm, d), lambda i, j: (i, 0)), tail_spec]
    out_shape = [jax.ShapeDtypeStruct((rows, d), F32), tail_shape]
    if export_w:
        out_specs += [pl.BlockSpec((d, tn), lambda i, j: (0, j)), pl.BlockSpec((d, tn), lambda i, j: (0, j)),
                      pl.BlockSpec((tn, d), lambda i, j: (j, 0))]
        out_shape += [jax.ShapeDtypeStruct((d, d_ff), BF16), jax.ShapeDtypeStruct((d, d_ff), BF16),
                      jax.ShapeDtypeStruct((d_ff, d), BF16)]
    outs = pl.pallas_call(
        functools.partial(_ffn_kernel, seq_rows=seq_rows, tiles_per_seq=tiles_per_seq, per_seq_hist=per_seq_hist,
                          export_w=export_w),
        grid=(n_row, n_col),
        in_specs=[
            pl.BlockSpec((tm, d), lambda i, j: (i, 0), pipeline_mode=pl.Buffered(1 if export_w else 2)),
            pl.BlockSpec((1, d), lambda i, j: (0, 0)),
            pl.BlockSpec((d, tn), lambda i, j: (0, j)),
            pl.BlockSpec((d, tn), lambda i, j: (0, g_off + j)),
            pl.BlockSpec((CONV_W, tn), lambda i, j: (0, j)),
            pl.BlockSpec((1, tn), lambda i, j: (0, j)),
            pl.BlockSpec((tn, d), lambda i, j: (j, 0)),
            pl.BlockSpec((1, d), lambda i, j: (0, 0)),
            *hist_specs,
        ],
        out_specs=out_specs,
        out_shape=out_shape,
        scratch_shapes=[
            pltpu.VMEM((tm, d), BF16),
            pltpu.VMEM((n_col, SUBLANES, tn), F32),
        ],
        compiler_params=_params("arbitrary", "arbitrary"),
        name="ffn",
    )(h, g2, w_a, w_g, conv_w, conv_b, w_down, gf, *hist_args)
    return outs[0], outs[1], (tuple(outs[2:]) if export_w else None)


def _layer(x3, pool_hist, conv_hist, attend, pos0, w, *, tm, in_tn, ffn_tn):
    n_seq, seq_len, d = x3.shape
    rows = n_seq * seq_len
    x = x3.reshape(rows, d)
    w_pool = w["pool_scale"].shape[1]
    w_att = w["w_att_up"].shape[0]
    rest, k, v, w_in_b = _in_proj(x, w["g1"], w["w_in"], k_col=w_pool + w_att, kv_width=w_att, tm=tm, tn=in_tn)
    rest3 = rest.reshape(n_seq, seq_len, -1)
    k3, v3 = k.reshape(n_seq, seq_len, w_att), v.reshape(n_seq, seq_len, w_att)
    q_col, ga_col, gb_col = w_pool, w_pool + w_att, w_pool + w_att + d
    if pool_hist is None:
        a_out = _pool_mix(rest3, None, w["pool_maps"], w["pool_scale"], s_blk=1, l_blk=POOL_TILE_ROWS, pos0=pos0,
                          out_dtype=BF16)
    else:
        a_out = _pool_mix(rest3, pool_hist, w["pool_maps"], w["pool_scale"], s_blk=32, l_blk=seq_len, pos0=pos0,
                          out_dtype=F32)
    b_out = attend(rest3, k3, v3, q_col)
    merged = _merge(a_out, b_out, w["w_pool_up"], w["w_att_up"], rest, ga_col=ga_col, gb_col=gb_col, tm=tm, tn=1024)
    h, w_out_b = _matmul_residual(merged, w["w_out"], x, tm=tm, tn=1024)
    y, tail, ffn_w = _ffn(h, w["g2"], w["w_up"], w["conv_w"], w["conv_b"], w["w_down"], w["gf"], conv_hist,
                          seq_rows=seq_len, tm=tm, tn=ffn_tn)
    return y.reshape(n_seq, seq_len, d), rest3, k3, v3, tail, (w_in_b, w_out_b, ffn_w)


def kernel(x_prompt, x_sample, cache_k, cache_v, state_pool, state_conv, page_table, norm1_g, w_in, pool_maps,
           pool_scale, w_pool_up, w_att_up, w_out, norm2_g, w_up, conv_w, conv_b, w_down, final_norm_g):
    depth = w_in.shape[0]
    assert depth == 1, "single-layer trunk"
    bsz, seq, d = x_prompt.shape
    n_dec, dec_seq, _ = x_sample.shape
    _, n_pool, page, n_heads, dh = cache_k.shape
    assert dh == HEAD_DIM
    w_att = n_heads * dh
    w_pool = pool_scale.shape[1]
    d_ff = w_down.shape[1]
    past_len = page_table.shape[1] * page
    slopes = 2.0 ** (-8.0 * jnp.arange(1, n_heads + 1, dtype=F32) / n_heads)

    l = 0
    w = dict(
        g1=norm1_g[l][None, :], w_in=w_in[l], pool_maps=pool_maps[l].astype(BF16),
        pool_scale=pool_scale[l][None, :], w_pool_up=w_pool_up[l].astype(BF16), w_att_up=w_att_up[l].astype(BF16),
        w_out=w_out[l], g2=norm2_g[l][None, :], w_up=w_up[l], conv_w=conv_w[l],
        conv_b=conv_b[l][None, :], w_down=w_down[l], gf=final_norm_g[None, :],
    )
    tm = ROW_TILE

    attend_s = lambda q_src3, k3, v3, qc: _sample_attention(q_src3, k3, v3, cache_k, cache_v, l, page_table, slopes,
                                                            q_col=qc)
    y_s, rest_s, k_s, v_s, tail_s, (w_in_b, w_out_b, (w_ua_b, w_ug_b, w_down_b)) = _layer(
        x_sample, state_pool[l], state_conv[l], attend_s, past_len, w, tm=tm, in_tn=512, ffn_tn=256)

    w_p = dict(w, w_in=w_in_b, w_out=w_out_b, w_up=(w_ua_b, w_ug_b), w_down=w_down_b)
    attend_p = lambda q_src3, k3, v3, qc: _prompt_attention(q_src3, k3, v3, slopes, q_col=qc)
    y_p, rest_p, k_p, v_p, tail_p, _ = _layer(x_prompt, None, None, attend_p, 0, w_p, tm=tm, in_tn=1024, ffn_tn=512)

    heads = lambda a3: a3.reshape(a3.shape[0], a3.shape[1], n_heads, dh)[None]
    pool_p = rest_p[:, seq - POOL_HIST:, :w_pool][None]
    pool_s = jnp.concatenate([state_pool[l], rest_s[:, :, :w_pool]], axis=1)[:, -POOL_HIST:][None]
    tiles_per_seq = seq // tm
    conv_p = tail_p.reshape(bsz, tiles_per_seq, SUBLANES, d_ff)[:, -1, SUBLANES - (CONV_W - 1):][None]
    conv_s = tail_s[None]
    return (y_p, y_s, heads(k_p), heads(v_p), heads(k_s), heads(v_s), pool_p, pool_s, conv_p, conv_s)
```

```python
import functools

import jax
import jax.numpy as jnp
from jax import lax
from jax.experimental import pallas as pl
from jax.experimental.pallas import tpu as pltpu

POOL_WINDOWS = (2, 4, 8, 16)
assert all(w & (w - 1) == 0 for w in POOL_WINDOWS)
POOL_HIST = max(POOL_WINDOWS) - 1
HEAD_DIM = 128
MOBA_BLOCK = 256
MOBA_TOPK = 3
CONV_W = 3
RMS_EPS = 1e-6
NEG_INF = -1e30

SUBLANES = 8
HALO_ROWS = 16
ROW_TILE = 1024
POOL_TILE_ROWS = 1024
POOL_SEQS_PER_STEP = 64
V7X_VMEM_BYTES = 64 * 1024 * 1024
VMEM_LIMIT_BYTES = 58 * 1024 * 1024

BF16 = jnp.bfloat16
F32 = jnp.float32


def _params(*semantics):
    return pltpu.CompilerParams(dimension_semantics=semantics, vmem_limit_bytes=VMEM_LIMIT_BYTES)


def _rmsnorm(x, g):
    return x * lax.rsqrt(jnp.mean(x * x, axis=-1, keepdims=True) + RMS_EPS) * g


def _dot(a, b):
    return jnp.dot(a, b, preferred_element_type=F32)


def _dot_nt(a, b):
    return lax.dot_general(a, b, (((1,), (1,)), ((), ())), preferred_element_type=F32)


def _split_bf16(x):
    hi = x.astype(BF16)
    lo = (x - hi.astype(F32)).astype(BF16)
    return hi, lo


def _dot_nt_3pass(a, b):
    a_hi, a_lo = _split_bf16(a)
    b_hi, b_lo = _split_bf16(b)
    return _dot_nt(a_hi, b_hi) + (_dot_nt(a_hi, b_lo) + _dot_nt(a_lo, b_hi))


def _in_proj_kernel(x_ref, g_ref, w_ref, rest_ref, k_ref, v_ref, *more, k_tiles, v_tiles, export_w):
    xn_ref = more[-1]
    j = pl.program_id(1)

    @pl.when(j == 0)
    def _():
        xn_ref[...] = _rmsnorm(x_ref[...], g_ref[...]).astype(BF16)

    def project_into(o_ref):
        wb = w_ref[...].astype(BF16)
        if export_w:
            more[0][...] = wb
        o_ref[...] = _dot(xn_ref[...], wb)

    is_k = (j >= k_tiles[0]) & (j < k_tiles[1])
    is_v = (j >= v_tiles[0]) & (j < v_tiles[1])
    pl.when(is_k)(lambda: project_into(k_ref))
    pl.when(is_v)(lambda: project_into(v_ref))
    pl.when(jnp.logical_not(is_k | is_v))(lambda: project_into(rest_ref))


def _in_proj(x, g, w, *, k_col, kv_width, tm, tn):
    rows, d = x.shape
    n = w.shape[1]
    export_w = w.dtype != BF16
    assert k_col % tn == 0 and kv_width % tn == 0
    k0, nkv = k_col // tn, kv_width // tn
    v0, after = k0 + nkv, k0 + 2 * nkv
    rest_map = lambda i, j: (i, jnp.where(j < k0, j, jnp.maximum(j - 2 * nkv, k0 - 1)))
    k_map = lambda i, j: (i, jnp.clip(j - k0, 0, nkv - 1))
    v_map = lambda i, j: (i, jnp.clip(j - v0, 0, nkv - 1))
    out_specs = [pl.BlockSpec((tm, tn), rest_map), pl.BlockSpec((tm, tn), k_map), pl.BlockSpec((tm, tn), v_map)]
    out_shape = [jax.ShapeDtypeStruct((rows, n - 2 * kv_width), F32),
                 jax.ShapeDtypeStruct((rows, kv_width), F32), jax.ShapeDtypeStruct((rows, kv_width), F32)]
    if export_w:
        out_specs.append(pl.BlockSpec((d, tn), lambda i, j: (0, j)))
        out_shape.append(jax.ShapeDtypeStruct((d, n), BF16))
    outs = pl.pallas_call(
        functools.partial(_in_proj_kernel, k_tiles=(k0, v0), v_tiles=(v0, after), export_w=export_w),
        grid=(rows // tm, n // tn),
        in_specs=[
            pl.BlockSpec((tm, d), lambda i, j: (i, 0)),
            pl.BlockSpec((1, d), lambda i, j: (0, 0)),
            pl.BlockSpec((d, tn), lambda i, j: (0, j)),
        ],
        out_specs=out_specs,
        out_shape=out_shape,
        scratch_shapes=[pltpu.VMEM((tm, d), BF16)],
        compiler_params=_params("arbitrary", "arbitrary"),
        name="in_proj",
    )(x, g, w)
    return (*outs, None) if not export_w else tuple(outs)


def _pool_kernel(u_ref, hist_ref, maps_ref, scale_ref, o_ref, ext_ref, *, pos0, zero_first_hist):
    s_blk, l_blk, width = u_ref.shape
    group = width // len(POOL_WINDOWS)
    tile = pl.program_id(1)
    u = u_ref[...]
    ext_ref[:, HALO_ROWS:, :] = u
    hist_rows = hist_ref.shape[1]
    ext_ref[:, HALO_ROWS - hist_rows:HALO_ROWS, :] = hist_ref[...]
    if hist_rows < HALO_ROWS:
        ext_ref[:, 0:HALO_ROWS - hist_rows, :] = jnp.zeros((s_blk, HALO_ROWS - hist_rows, width), F32)
    if zero_first_hist:
        @pl.when(tile == 0)
        def _():
            ext_ref[:, 0:HALO_ROWS, :] = jnp.zeros((s_blk, HALO_ROWS, width), F32)

    pos = pos0 + tile * l_blk + lax.broadcasted_iota(jnp.int32, (s_blk, l_blk, group), 1)
    outs = []
    for gi, w in enumerate(POOL_WINDOWS):
        cols = slice(gi * group, (gi + 1) * group)
        acc = ext_ref[:, :, cols]
        span = 1
        while span < w:
            acc = acc + pltpu.roll(acc, span, axis=1)
            span *= 2
        win = acc[:, HALO_ROWS:, :]
        cnt = jnp.minimum(w, pos + 1).astype(F32)
        d = win / cnt - u[:, :, cols]
        d2 = d.reshape(s_blk * l_blk, group).astype(BF16)
        outs.append(_dot(d2, maps_ref[gi]))
    m = jnp.concatenate(outs, axis=-1) * scale_ref[...]
    o_ref[...] = m.astype(o_ref.dtype)


def _pool_mix(proj3, hist, maps, scale, *, s_blk, l_blk, pos0, out_dtype):
    n_seq, seq_len, _ = proj3.shape
    n_groups, group, _ = maps.shape
    width = n_groups * group
    tiles = seq_len // l_blk
    if hist is None:
        halo_blocks = l_blk // HALO_ROWS
        hist_arr = proj3
        hist_spec = pl.BlockSpec((s_blk, HALO_ROWS, width),
                                 lambda s, t: (s, jnp.maximum(t * halo_blocks - 1, 0), 0))
    else:
        assert tiles == 1
        hist_arr = hist
        hist_spec = pl.BlockSpec((s_blk, hist.shape[1], width), lambda s, t: (s, 0, 0))
    kern = functools.partial(_pool_kernel, pos0=pos0, zero_first_hist=hist is None)
    return pl.pallas_call(
        kern,
        grid=(n_seq // s_blk, tiles),
        in_specs=[
            pl.BlockSpec((s_blk, l_blk, width), lambda s, t: (s, t, 0)),
            hist_spec,
            pl.BlockSpec((n_groups, group, group), lambda s, t: (0, 0, 0)),
            pl.BlockSpec((1, width), lambda s, t: (0, 0)),
        ],
        out_specs=pl.BlockSpec((s_blk * l_blk, width), lambda s, t: (s * tiles + t, 0)),
        out_shape=jax.ShapeDtypeStruct((n_seq * seq_len, width), out_dtype),
        scratch_shapes=[pltpu.VMEM((s_blk, HALO_ROWS + l_blk, width), F32)],
        compiler_params=_params("parallel", "arbitrary"),
        name="pool_mix",
    )(proj3, hist_arr, maps, scale)


def _topk_select(route, n_valid):
    nb, n = route.shape
    blk = lax.broadcasted_iota(jnp.int32, (nb, n), 0)
    rank = jnp.zeros((nb, n), jnp.int32)
    for other in range(nb):
        c = route[other:other + 1, :]
        beats = jnp.where((c > route) | ((c == route) & (other < blk)), 1, 0)
        rank = rank + beats * (other < n_valid).astype(jnp.int32)
    return (blk < n_valid) & (rank < MOBA_TOPK)


def _topk_select_list(scores):
    sel = []
    for j, s_j in enumerate(scores):
        rank = jnp.zeros(s_j.shape, jnp.int32)
        for other, s_o in enumerate(scores):
            if other < j:
                rank = rank + jnp.where(s_o >= s_j, 1, 0)
            elif other > j:
                rank = rank + jnp.where(s_o > s_j, 1, 0)
        sel.append(rank < MOBA_TOPK)
    return sel


KV_TILE = 2 * MOBA_BLOCK
assert KV_TILE == 2 * MOBA_BLOCK
ONES_ROWS = 16
M_FLOOR = -1e20
PROMPT_HEADS_PER_STEP = 4
SCORE_LOOKAHEAD = 4


def _prompt_attn_kernel(slopes_ref, q_ref, k_ref, v_ref, o_ref,
                        kb_ref, vt_ref, kmean_ref, alibi_ref, m_ref, acc_ref):
    hgroup = pl.program_id(1)
    qt = pl.program_id(2)
    n_hb, seq_len, dh = kb_ref.shape
    tq = q_ref.shape[1]
    nb = seq_len // MOBA_BLOCK
    blocks_per_tile = KV_TILE // MOBA_BLOCK
    slopes = [slopes_ref[hgroup * n_hb + hh] for hh in range(n_hb)]
    q_minus_k = (lax.broadcasted_iota(jnp.int32, (KV_TILE, tq), 1)
                 - lax.broadcasted_iota(jnp.int32, (KV_TILE, tq), 0))

    @pl.when(qt == 0)
    def _():
        for hh in range(n_hb):
            cols = slice(hh * dh, (hh + 1) * dh)
            k = k_ref[0, :, cols]
            kb_ref[hh] = k.astype(BF16)
            kmean_ref[hh] = jnp.mean(k.reshape(nb, MOBA_BLOCK, dh), axis=1)
            ones_rows = jnp.where(lax.broadcasted_iota(jnp.int32, (ONES_ROWS, MOBA_BLOCK), 0) == 0, 1.0, 0.0)
            for c in range(nb):
                vt_ref[hh, c, 0:dh, :] = v_ref[0, c * MOBA_BLOCK:(c + 1) * MOBA_BLOCK, cols].T.astype(BF16)
                vt_ref[hh, c, dh:, :] = ones_rows.astype(BF16)
            alibi_ref[hh] = slopes[hh] * q_minus_k.astype(F32)

    blk = lax.broadcasted_iota(jnp.int32, (nb, tq), 0)
    qbs, sel_bits = [], []
    for hh in range(n_hb):
        q = q_ref[0, :, hh * dh:(hh + 1) * dh]
        qbs.append((q * dh ** -0.5).astype(BF16))
        route = _dot_nt_3pass(kmean_ref[hh], q)
        sel = _topk_select(route, qt)
        bits = jnp.sum(jnp.where(sel, jnp.left_shift(1, blk).astype(F32), 0.0), axis=0, keepdims=True)
        sel_bits.append(bits.astype(jnp.int32))

    m_ref[...] = jnp.full(m_ref.shape, M_FLOOR, F32)
    acc_ref[...] = jnp.zeros(acc_ref.shape, F32)

    def block_bias(hh, j):
        bit = jnp.right_shift(sel_bits[hh], j) & 1
        return jnp.broadcast_to(jnp.where(bit == 1, 0.0, NEG_INF), (MOBA_BLOCK, tq))

    def attend(first_blk, n_blk, bias_fn, offset):
        keys = n_blk * MOBA_BLOCK
        start = pl.multiple_of(first_blk * MOBA_BLOCK, MOBA_BLOCK)
        score = lambda hh: _dot_nt(kb_ref[hh, pl.ds(start, keys), :], qbs[hh])
        scores = {hh: score(hh) for hh in range(min(SCORE_LOOKAHEAD, n_hb))}
        for hh in range(n_hb):
            if hh + SCORE_LOOKAHEAD < n_hb:
                scores[hh + SCORE_LOOKAHEAD] = score(hh + SCORE_LOOKAHEAD)
            z = scores.pop(hh) - alibi_ref[hh, 0:keys, :] + bias_fn(hh)
            shift = slopes[hh] * jnp.asarray(offset).astype(F32)
            m_old = m_ref[hh]
            m_new = jnp.maximum(m_old, jnp.max(z, axis=0, keepdims=True) - shift)
            alpha = jnp.exp(m_old - m_new)
            pr = jnp.exp(z - (m_new + shift))
            pv = None
            for b in range(n_blk):
                part = _dot(vt_ref[hh, first_blk + b], pr[b * MOBA_BLOCK:(b + 1) * MOBA_BLOCK].astype(BF16))
                pv = part if pv is None else pv + part
            acc_ref[hh] = alpha * acc_ref[hh] + pv
            m_ref[hh] = m_new

    def past_pair(p, carry):
        first = p * blocks_per_tile
        attend(first, blocks_per_tile,
               lambda hh: jnp.concatenate([block_bias(hh, first + b) for b in range(blocks_per_tile)], axis=0),
               (qt - first) * MOBA_BLOCK)
        return carry

    lax.fori_loop(0, qt // blocks_per_tile, past_pair, 0)

    @pl.when(qt % blocks_per_tile == 1)
    def _():
        attend(qt - 1, 1, lambda hh: block_bias(hh, qt - 1), MOBA_BLOCK)

    causal = jnp.where(q_minus_k[0:MOBA_BLOCK, :] >= 0, 0.0, NEG_INF)
    attend(qt, 1, lambda hh: causal, 0)
    for hh in range(n_hb):
        acc = acc_ref[hh]
        o_ref[:, hh * dh:(hh + 1) * dh] = (acc[0:dh, :] / acc[dh:dh + 1, :]).T.astype(o_ref.dtype)


def _prompt_attention(q_src3, k3, v3, slopes, *, q_col):
    bsz, seq_len, w_att = k3.shape
    n_heads = w_att // HEAD_DIM
    tq = MOBA_BLOCK
    nq = seq_len // tq
    nb = seq_len // MOBA_BLOCK
    hb = PROMPT_HEADS_PER_STEP
    gw = hb * HEAD_DIM
    assert n_heads % hb == 0 and seq_len % KV_TILE == 0 and q_col % gw == 0
    qb0 = q_col // gw
    return pl.pallas_call(
        _prompt_attn_kernel,
        grid=(bsz, n_heads // hb, nq),
        in_specs=[
            pl.BlockSpec(memory_space=pltpu.SMEM),
            pl.BlockSpec((1, tq, gw), lambda b, h, i: (b, i, qb0 + h)),
            pl.BlockSpec((1, seq_len, gw), lambda b, h, i: (b, 0, h)),
            pl.BlockSpec((1, seq_len, gw), lambda b, h, i: (b, 0, h)),
        ],
        out_specs=pl.BlockSpec((tq, gw), lambda b, h, i: (b * nq + i, h)),
        out_shape=jax.ShapeDtypeStruct((bsz * seq_len, n_heads * HEAD_DIM), BF16),
        scratch_shapes=[
            pltpu.VMEM((hb, seq_len, HEAD_DIM), BF16),
            pltpu.VMEM((hb, nb, HEAD_DIM + ONES_ROWS, MOBA_BLOCK), BF16),
            pltpu.VMEM((hb, nb, HEAD_DIM), F32),
            pltpu.VMEM((hb, KV_TILE, tq), F32),
            pltpu.VMEM((hb, 1, tq), F32),
            pltpu.VMEM((hb, HEAD_DIM + ONES_ROWS, tq), F32),
        ],
        compiler_params=_params("parallel", "parallel", "arbitrary"),
        name="prompt_attn",
    )(slopes, q_src3, k3, v3)


def _heads_to_rows(x, n_heads, dh):
    return jnp.concatenate([x[:, h * dh:(h + 1) * dh] for h in range(n_heads)], axis=0)


def _sample_attn_kernel(pt_ref, slopes_ref, q_ref, kn_ref, vn_ref, *rest, past_len, n_pages):
    del pt_ref
    k_refs, v_refs, o_ref = rest[:n_pages], rest[n_pages:2 * n_pages], rest[2 * n_pages]
    t = q_ref.shape[1]
    page, n_heads, dh = k_refs[0].shape
    pages_per_block = MOBA_BLOCK // page
    nb = n_pages // pages_per_block
    rows = n_heads * t
    page_keys = page * n_heads

    q2 = _heads_to_rows(q_ref[0], n_heads, dh)
    q2b = (q2 * dh ** -0.5).astype(BF16)
    row_head = lax.broadcasted_iota(jnp.int32, (rows, 1), 0) // t
    row_query = lax.broadcasted_iota(jnp.int32, (rows, 1), 0) % t
    slope_rows = jnp.zeros((rows, 1), F32)
    for h in range(n_heads):
        slope_rows = jnp.where(row_head == h, slopes_ref[h], slope_rows)
    lane = lax.broadcasted_iota(jnp.int32, (rows, page_keys), 1)
    page_bias = jnp.where(lane % n_heads == row_head,
                          slope_rows * (row_query - lane // n_heads).astype(F32), -NEG_INF)

    def block_scores(b):
        ksum = jnp.zeros((n_heads, dh), F32)
        s_pages = []
        for pg in range(b * pages_per_block, (b + 1) * pages_per_block):
            kp = k_refs[pg][...]
            ksum = ksum + jnp.sum(kp, axis=0)
            s_pg = _dot_nt(q2b, kp.reshape(page_keys, dh).astype(BF16))
            s_pages.append(s_pg - page_bias - slope_rows * float(past_len - pg * page))
        kmean = ksum / MOBA_BLOCK
        kmean_rows = jnp.concatenate([jnp.broadcast_to(kmean[h:h + 1, :], (t, dh)) for h in range(n_heads)], axis=0)
        return jnp.sum(q2 * kmean_rows, axis=-1, keepdims=True), jnp.concatenate(s_pages, axis=-1)

    route, m_blk, l_blk, acc_blk = [], [], [], []
    ahead = block_scores(0)
    for b in range(nb):
        route_b, s_b = ahead
        if b + 1 < nb:
            ahead = block_scores(b + 1)
        pages = range(b * pages_per_block, (b + 1) * pages_per_block)
        vblk = jnp.concatenate([v_refs[pg][...].reshape(page_keys, dh) for pg in pages], axis=0)
        m_b = jnp.max(s_b, axis=-1, keepdims=True)
        p = jnp.exp(s_b - m_b)
        route.append(route_b)
        m_blk.append(m_b)
        l_blk.append(jnp.sum(p, axis=-1, keepdims=True))
        acc_blk.append(_dot(p.astype(BF16), vblk.astype(BF16)))

    sel = _topk_select_list(route)
    kn2 = _heads_to_rows(kn_ref[0], n_heads, dh)
    vn2 = _heads_to_rows(vn_ref[0], n_heads, dh)
    col = lax.broadcasted_iota(jnp.int32, (rows, rows), 1)
    dist = row_query - col % t
    keep = (col // t == row_head) & (dist >= 0)
    s_own = _dot_nt(q2b, kn2.astype(BF16)) - slope_rows * dist.astype(F32)
    s_own = jnp.where(keep, s_own, NEG_INF)
    m_all = jnp.max(s_own, axis=-1, keepdims=True)
    for b in range(nb):
        m_all = jnp.maximum(m_all, jnp.where(sel[b], m_blk[b], NEG_INF))
    p_own = jnp.where(keep, jnp.exp(s_own - m_all), 0.0)
    den = jnp.sum(p_own, axis=-1, keepdims=True)
    num = _dot(p_own.astype(BF16), vn2.astype(BF16))
    for b in range(nb):
        w_b = jnp.where(sel[b], jnp.exp(m_blk[b] - m_all), 0.0)
        den = den + w_b * l_blk[b]
        num = num + w_b * acc_blk[b]
    out = num / den
    o_ref[...] = jnp.concatenate([out[h * t:(h + 1) * t, :] for h in range(n_heads)], axis=-1)


def _sample_attention(q_src3, k3, v3, cache_k, cache_v, layer, page_table, slopes, *, q_col):
    n_seq, t, _ = k3.shape
    _, _, page, n_heads, dh = cache_k.shape
    width = n_heads * dh
    n_pages = page_table.shape[1]
    assert MOBA_BLOCK % page == 0 and (n_pages * page) % MOBA_BLOCK == 0 and t <= MOBA_BLOCK
    assert q_col % width == 0
    past_len = n_pages * page
    qb0 = q_col // width
    page_specs = [pl.BlockSpec((None, None, page, n_heads, dh), lambda s, pt, pg=pg: (layer, pt[s, pg], 0, 0, 0))
                  for pg in range(n_pages)]
    grid_spec = pltpu.PrefetchScalarGridSpec(
        num_scalar_prefetch=1,
        grid=(n_seq,),
        in_specs=[
            pl.BlockSpec(memory_space=pltpu.SMEM),
            pl.BlockSpec((1, t, width), lambda s, pt: (s, 0, qb0)),
            pl.BlockSpec((1, t, width), lambda s, pt: (s, 0, 0)),
            pl.BlockSpec((1, t, width), lambda s, pt: (s, 0, 0)),
            *page_specs, *page_specs,
        ],
        out_specs=pl.BlockSpec((t, width), lambda s, pt: (s, 0)),
    )
    return pl.pallas_call(
        functools.partial(_sample_attn_kernel, past_len=past_len, n_pages=n_pages),
        grid_spec=grid_spec,
        out_shape=jax.ShapeDtypeStruct((n_seq * t, width), F32),
        compiler_params=_params("parallel"),
        name="sample_attn",
    )(page_table, slopes, q_src3, k3, v3, *([cache_k] * n_pages), *([cache_v] * n_pages))


def _merge_kernel(a_ref, b_ref, wp_ref, wa_ref, ga_ref, gb_ref, o_ref):
    pa = _dot(a_ref[...].astype(BF16), wp_ref[...])
    pb = _dot(b_ref[...].astype(BF16), wa_ref[...])
    merged = jax.nn.sigmoid(ga_ref[...]) * pa + jax.nn.sigmoid(gb_ref[...]) * pb
    o_ref[...] = merged.astype(o_ref.dtype)


def _merge(a_out, b_out, w_pool_up, w_att_up, proj, *, ga_col, gb_col, tm, tn):
    rows, wa = a_out.shape
    wb = b_out.shape[1]
    n = w_pool_up.shape[1]
    ga0, gb0 = ga_col // tn, gb_col // tn
    return pl.pallas_call(
        _merge_kernel,
        grid=(rows // tm, n // tn),
        in_specs=[
            pl.BlockSpec((tm, wa), lambda i, j: (i, 0)),
            pl.BlockSpec((tm, wb), lambda i, j: (i, 0)),
            pl.BlockSpec((wa, tn), lambda i, j: (0, j)),
            pl.BlockSpec((wb, tn), lambda i, j: (0, j)),
            pl.BlockSpec((tm, tn), lambda i, j: (i, ga0 + j)),
            pl.BlockSpec((tm, tn), lambda i, j: (i, gb0 + j)),
        ],
        out_specs=pl.BlockSpec((tm, tn), lambda i, j: (i, j)),
        out_shape=jax.ShapeDtypeStruct((rows, n), BF16),
        compiler_params=_params("parallel", "arbitrary"),
        name="merge",
    )(a_out, b_out, w_pool_up, w_att_up, proj, proj)


def _matmul_residual_kernel(a_ref, w_ref, r_ref, o_ref, *w_outs):
    wb = w_ref[...].astype(BF16)
    if w_outs:
        w_outs[0][...] = wb
    o_ref[...] = r_ref[...] + _dot(a_ref[...], wb)


def _matmul_residual(a, w, res, *, tm, tn):
    rows, k = a.shape
    n = w.shape[1]
    export_w = w.dtype != BF16
    out_specs = [pl.BlockSpec((tm, tn), lambda i, j: (i, j))]
    out_shape = [jax.ShapeDtypeStruct((rows, n), F32)]
    if export_w:
        out_specs.append(pl.BlockSpec((k, tn), lambda i, j: (0, j)))
        out_shape.append(jax.ShapeDtypeStruct((k, n), BF16))
    outs = pl.pallas_call(
        _matmul_residual_kernel,
        grid=(rows // tm, n // tn),
        in_specs=[
            pl.BlockSpec((tm, k), lambda i, j: (i, 0)),
            pl.BlockSpec((k, tn), lambda i, j: (0, j)),
            pl.BlockSpec((tm, tn), lambda i, j: (i, j)),
        ],
        out_specs=out_specs,
        out_shape=out_shape,
        compiler_params=_params("arbitrary", "arbitrary"),
        name="out_proj",
    )(a, w, res)
    return outs[0], (outs[1] if export_w else None)


FFN_CHUNK_COLS = 256


def _gelu_exact(x):
    return 0.5 * x * (1.0 + lax.erf(x * (2.0 ** -0.5)))


def _ffn_kernel(h_ref, g2_ref, wa_ref, wg_ref, cw_ref, cb_ref, wd_ref, gf_ref, *rest,
                seq_rows, tiles_per_seq, per_seq_hist, export_w):
    hist_ref = rest[0] if per_seq_hist else None
    n_in = 1 if per_seq_hist else 0
    y_ref, tail_ref = rest[n_in:n_in + 2]
    wa_out, wg_out, wd_out = rest[n_in + 2:n_in + 5] if export_w else (None, None, None)
    hn_ref, carry_ref = rest[-2:]
    i = pl.program_id(0)
    j = pl.program_id(1)
    tm = h_ref.shape[0]
    tn = wa_ref.shape[1]
    sub = FFN_CHUNK_COLS
    n_chunks = tn // sub

    @pl.when(j == 0)
    def _():
        h = h_ref[...]
        hn_ref[...] = _rmsnorm(h, g2_ref[...]).astype(BF16)
        y_ref[...] = h

    if not per_seq_hist:
        @pl.when(i % tiles_per_seq == 0)
        def _():
            carry_ref[j] = jnp.zeros((SUBLANES, tn), F32)

    hn = hn_ref[...]
    chunks = [slice(c * sub, (c + 1) * sub) for c in range(n_chunks)]
    wa_parts = [wa_ref[:, cols].astype(BF16) for cols in chunks]
    wg_parts = [wg_ref[:, cols].astype(BF16) for cols in chunks]
    wd_parts = [wd_ref[cols, :].astype(BF16) for cols in chunks]
    if export_w:
        for cols, wa_c, wg_c, wd_c in zip(chunks, wa_parts, wg_parts, wd_parts):
            wa_out[:, cols] = wa_c
            wg_out[:, cols] = wg_c
            wd_out[cols, :] = wd_c
    a_parts = [_dot(hn, wa_c) for wa_c in wa_parts]
    g_parts = [_dot(hn, wg_c) for wg_c in wg_parts]
    row = lax.broadcasted_iota(jnp.int32, (tm, sub), 0)
    down = None
    for cols, a, gv, wd_c in zip(chunks, a_parts, g_parts, wd_parts):
        if per_seq_hist:
            n_seq = tm // seq_rows
            hist = hist_ref[:, :, cols]
            prev = [jnp.broadcast_to(hist[:, r:r + 1, :], (n_seq, seq_rows, sub)).reshape(tm, sub)
                    for r in range(CONV_W - 1)]
            t_in_seq = row % seq_rows
            tail_ref[:, :, cols] = a.reshape(n_seq, seq_rows, sub)[:, seq_rows - (CONV_W - 1):, :]
        else:
            carry = carry_ref[j, :, cols]
            first = SUBLANES - (CONV_W - 1)
            prev = [jnp.broadcast_to(carry[first + r:first + r + 1, :], (tm, sub)) for r in range(CONV_W - 1)]
            t_in_seq = row
            tail = a[tm - SUBLANES:, :]
            carry_ref[j, :, cols] = tail
            tail_ref[0, :, cols] = tail
        ac = cb_ref[:, cols] + cw_ref[CONV_W - 1:CONV_W, cols] * a
        for back in range(1, CONV_W):
            shifted = pltpu.roll(a, back, axis=0)
            for r in range(back):
                shifted = jnp.where(t_in_seq == r, prev[CONV_W - 1 - back + r], shifted)
            ac = ac + cw_ref[CONV_W - 1 - back:CONV_W - back, cols] * shifted
        act = (_gelu_exact(ac) * gv).astype(BF16)
        part = _dot(act, wd_c)
        down = part if down is None else down + part
    y_ref[...] += down

    @pl.when(j == pl.num_programs(1) - 1)
    def _():
        y_ref[...] = _rmsnorm(y_ref[...], gf_ref[...])


def _ffn(h, g2, w_up, conv_w, conv_b, w_down, gf, hist, *, seq_rows, tm, tn):
    rows, d = h.shape
    d_ff = w_down.shape[0]
    n_col = d_ff // tn
    n_row = rows // tm
    per_seq_hist = hist is not None
    export_w = not isinstance(w_up, tuple)
    if export_w:
        w_a, w_g, g_off = w_up, w_up, n_col
        assert w_up.dtype == F32 and w_down.dtype == F32
    else:
        (w_a, w_g), g_off = w_up, 0
    if per_seq_hist:
        assert tm % seq_rows == 0 and seq_rows >= CONV_W - 1
        tiles_per_seq = 1
        tail_shape = jax.ShapeDtypeStruct((rows // seq_rows, CONV_W - 1, d_ff), F32)
        tail_spec = pl.BlockSpec((tm // seq_rows, CONV_W - 1, tn), lambda i, j: (i, 0, j))
        hist_args = (hist,)
        hist_specs = [pl.BlockSpec((tm // seq_rows, CONV_W - 1, tn), lambda i, j: (i, 0, j))]
    else:
        assert seq_rows % tm == 0
        tiles_per_seq = seq_rows // tm
        tail_shape = jax.ShapeDtypeStruct((n_row, SUBLANES, d_ff), F32)
        tail_spec = pl.BlockSpec((1, SUBLANES, tn), lambda i, j: (i, 0, j))
        hist_args = ()
        hist_specs = []
    out_specs = [pl.BlockSpec((tm, d), lambda i, j: (i, 0)), tail_spec]
    out_shape = [jax.ShapeDtypeStruct((rows, d), F32), tail_shape]
    if export_w:
        out_specs += [pl.BlockSpec((d, tn), lambda i, j: (0, j)), pl.BlockSpec((d, tn), lambda i, j: (0, j)),
                      pl.BlockSpec((tn, d), lambda i, j: (j, 0))]
        out_shape += [jax.ShapeDtypeStruct((d, d_ff), BF16), jax.ShapeDtypeStruct((d, d_ff), BF16),
                      jax.ShapeDtypeStruct((d_ff, d), BF16)]
    outs = pl.pallas_call(
        functools.partial(_ffn_kernel, seq_rows=seq_rows, tiles_per_seq=tiles_per_seq, per_seq_hist=per_seq_hist,
                          export_w=export_w),
        grid=(n_row, n_col),
        in_specs=[
            pl.BlockSpec((tm, d), lambda i, j: (i, 0), pipeline_mode=pl.Buffered(1 if export_w else 2)),
            pl.BlockSpec((1, d), lambda i, j: (0, 0)),
            pl.BlockSpec((d, tn), lambda i, j: (0, j)),
            pl.BlockSpec((d, tn), lambda i, j: (0, g_off + j)),
            pl.BlockSpec((CONV_W, tn), lambda i, j: (0, j)),
            pl.BlockSpec((1, tn), lambda i, j: (0, j)),
            pl.BlockSpec((tn, d), lambda i, j: (j, 0)),
            pl.BlockSpec((1, d), lambda i, j: (0, 0)),
            *hist_specs,
        ],
        out_specs=out_specs,
        out_shape=out_shape,
        scratch_shapes=[
            pltpu.VMEM((tm, d), BF16),
            pltpu.VMEM((n_col, SUBLANES, tn), F32),
        ],
        compiler_params=_params("arbitrary", "arbitrary"),
        name="ffn",
    )(h, g2, w_a, w_g, conv_w, conv_b, w_down, gf, *hist_args)
    return outs[0], outs[1], (tuple(outs[2:]) if export_w else None)


def _layer(x3, pool_hist, conv_hist, attend, pos0, w, *, tm, in_tn, ffn_tn, mid_tile):
    n_seq, seq_len, d = x3.shape
    rows = n_seq * seq_len
    x = x3.reshape(rows, d)
    w_pool = w["pool_scale"].shape[1]
    w_att = w["w_att_up"].shape[0]
    rest, k, v, w_in_b = _in_proj(x, w["g1"], w["w_in"], k_col=w_pool + w_att, kv_width=w_att, tm=tm, tn=in_tn)
    rest3 = rest.reshape(n_seq, seq_len, -1)
    k3, v3 = k.reshape(n_seq, seq_len, w_att), v.reshape(n_seq, seq_len, w_att)
    q_col, ga_col, gb_col = w_pool, w_pool + w_att, w_pool + w_att + d
    if pool_hist is None:
        a_out = _pool_mix(rest3, None, w["pool_maps"], w["pool_scale"], s_blk=1, l_blk=POOL_TILE_ROWS, pos0=pos0,
                          out_dtype=BF16)
    else:
        a_out = _pool_mix(rest3, pool_hist, w["pool_maps"], w["pool_scale"], s_blk=POOL_SEQS_PER_STEP, l_blk=seq_len,
                          pos0=pos0, out_dtype=F32)
    b_out = attend(rest3, k3, v3, q_col)
    mid_tm, mid_tn = mid_tile
    merged = _merge(a_out, b_out, w["w_pool_up"], w["w_att_up"], rest, ga_col=ga_col, gb_col=gb_col,
                    tm=mid_tm, tn=mid_tn)
    h, w_out_b = _matmul_residual(merged, w["w_out"], x, tm=mid_tm, tn=mid_tn)
    y, tail, ffn_w = _ffn(h, w["g2"], w["w_up"], w["conv_w"], w["conv_b"], w["w_down"], w["gf"], conv_hist,
                          seq_rows=seq_len, tm=tm, tn=ffn_tn)
    return y.reshape(n_seq, seq_len, d), rest3, k3, v3, tail, (w_in_b, w_out_b, ffn_w)


def kernel(x_prompt, x_sample, cache_k, cache_v, state_pool, state_conv, page_table, norm1_g, w_in, pool_maps,
           pool_scale, w_pool_up, w_att_up, w_out, norm2_g, w_up, conv_w, conv_b, w_down, final_norm_g):
    depth = w_in.shape[0]
    assert depth == 1, "single-layer trunk"
    bsz, seq, d = x_prompt.shape
    n_dec, dec_seq, _ = x_sample.shape
    _, n_pool, page, n_heads, dh = cache_k.shape
    assert dh == HEAD_DIM
    w_att = n_heads * dh
    w_pool = pool_scale.shape[1]
    d_ff = w_down.shape[1]
    past_len = page_table.shape[1] * page
    slopes = 2.0 ** (-8.0 * jnp.arange(1, n_heads + 1, dtype=F32) / n_heads)

    l = 0
    w = dict(
        g1=norm1_g[l][None, :], w_in=w_in[l], pool_maps=pool_maps[l].astype(BF16),
        pool_scale=pool_scale[l][None, :], w_pool_up=w_pool_up[l].astype(BF16), w_att_up=w_att_up[l].astype(BF16),
        w_out=w_out[l], g2=norm2_g[l][None, :], w_up=w_up[l], conv_w=conv_w[l],
        conv_b=conv_b[l][None, :], w_down=w_down[l], gf=final_norm_g[None, :],
    )
    tm = ROW_TILE

    attend_s = lambda q_src3, k3, v3, qc: _sample_attention(q_src3, k3, v3, cache_k, cache_v, l, page_table, slopes,
                                                            q_col=qc)
    y_s, rest_s, k_s, v_s, tail_s, (w_in_b, w_out_b, (w_ua_b, w_ug_b, w_down_b)) = _layer(
        x_sample, state_pool[l], state_conv[l], attend_s, past_len, w, tm=tm, in_tn=512, ffn_tn=256,
        mid_tile=(tm, 1024))

    w_p = dict(w, w_in=w_in_b, w_out=w_out_b, w_up=(w_ua_b, w_ug_b), w_down=w_down_b)
    attend_p = lambda q_src3, k3, v3, qc: _prompt_attention(q_src3, k3, v3, slopes, q_col=qc)
    y_p, rest_p, k_p, v_p, tail_p, _ = _layer(x_prompt, None, None, attend_p, 0, w_p, tm=tm, in_tn=1024, ffn_tn=512,
                                              mid_tile=(tm // 2, d))

    heads = lambda a3: a3.reshape(a3.shape[0], a3.shape[1], n_heads, dh)[None]
    pool_p = rest_p[:, seq - POOL_HIST:, :w_pool][None]
    pool_s = jnp.concatenate([state_pool[l], rest_s[:, :, :w_pool]], axis=1)[:, -POOL_HIST:][None]
    tiles_per_seq = seq // tm
    conv_p = tail_p.reshape(bsz, tiles_per_seq, SUBLANES, d_ff)[:, -1, SUBLANES - (CONV_W - 1):][None]
    conv_s = tail_s[None]
    return (y_p, y_s, heads(k_p), heads(v_p), heads(k_s), heads(v_s), pool_p, pool_s, conv_p, conv_s)
```

```python
import functools

import jax
import jax.numpy as jnp
from jax import lax
from jax.experimental import pallas as pl
from jax.experimental.pallas import tpu as pltpu

POOL_WINDOWS = (2, 4, 8, 16)
assert all(w & (w - 1) == 0 for w in POOL_WINDOWS)
POOL_HIST = max(POOL_WINDOWS) - 1
HEAD_DIM = 128
MOBA_BLOCK = 256
MOBA_TOPK = 3
CONV_W = 3
RMS_EPS = 1e-6
NEG_INF = -1e30

SUBLANES = 8
HALO_ROWS = 16
ROW_TILE = 1024
POOL_TILE_ROWS = 1024
POOL_SEQS_PER_STEP = 64
V7X_VMEM_BYTES = 64 * 1024 * 1024
VMEM_LIMIT_BYTES = V7X_VMEM_BYTES - 6 * 1024 * 1024

BF16 = jnp.bfloat16
F32 = jnp.float32


def _params(*semantics):
    return pltpu.CompilerParams(dimension_semantics=semantics, vmem_limit_bytes=VMEM_LIMIT_BYTES)


def _rmsnorm(x, g):
    return x * lax.rsqrt(jnp.mean(x * x, axis=-1, keepdims=True) + RMS_EPS) * g


def _dot(a, b):
    return jnp.dot(a, b, preferred_element_type=F32)


def _dot_nt(a, b):
    return lax.dot_general(a, b, (((1,), (1,)), ((), ())), preferred_element_type=F32)


def _split_bf16(x):
    hi = x.astype(BF16)
    lo = (x - hi.astype(F32)).astype(BF16)
    return hi, lo


def _dot_nt_3pass(a, b):
    a_hi, a_lo = _split_bf16(a)
    b_hi, b_lo = _split_bf16(b)
    return _dot_nt(a_hi, b_hi) + (_dot_nt(a_hi, b_lo) + _dot_nt(a_lo, b_hi))


def _in_proj_kernel(x_ref, g_ref, w_ref, rest_ref, k_ref, v_ref, *more, k_tiles, v_tiles, export_w):
    xn_ref = more[-1]
    j = pl.program_id(1)

    @pl.when(j == 0)
    def _():
        xn_ref[...] = _rmsnorm(x_ref[...], g_ref[...]).astype(BF16)

    def project_into(o_ref):
        wb = w_ref[...].astype(BF16)
        if export_w:
            more[0][...] = wb
        o_ref[...] = _dot(xn_ref[...], wb)

    is_k = (j >= k_tiles[0]) & (j < k_tiles[1])
    is_v = (j >= v_tiles[0]) & (j < v_tiles[1])
    pl.when(is_k)(lambda: project_into(k_ref))
    pl.when(is_v)(lambda: project_into(v_ref))
    pl.when(jnp.logical_not(is_k | is_v))(lambda: project_into(rest_ref))


def _in_proj(x, g, w, *, k_col, kv_width, tm, tn):
    rows, d = x.shape
    n = w.shape[1]
    export_w = w.dtype != BF16
    assert k_col % tn == 0 and kv_width % tn == 0
    k0, nkv = k_col // tn, kv_width // tn
    v0, after = k0 + nkv, k0 + 2 * nkv
    rest_map = lambda i, j: (i, jnp.where(j < k0, j, jnp.maximum(j - 2 * nkv, k0 - 1)))
    k_map = lambda i, j: (i, jnp.clip(j - k0, 0, nkv - 1))
    v_map = lambda i, j: (i, jnp.clip(j - v0, 0, nkv - 1))
    out_specs = [pl.BlockSpec((tm, tn), rest_map), pl.BlockSpec((tm, tn), k_map), pl.BlockSpec((tm, tn), v_map)]
    out_shape = [jax.ShapeDtypeStruct((rows, n - 2 * kv_width), F32),
                 jax.ShapeDtypeStruct((rows, kv_width), F32), jax.ShapeDtypeStruct((rows, kv_width), F32)]
    if export_w:
        out_specs.append(pl.BlockSpec((d, tn), lambda i, j: (0, j)))
        out_shape.append(jax.ShapeDtypeStruct((d, n), BF16))
    outs = pl.pallas_call(
        functools.partial(_in_proj_kernel, k_tiles=(k0, v0), v_tiles=(v0, after), export_w=export_w),
        grid=(rows // tm, n // tn),
        in_specs=[
            pl.BlockSpec((tm, d), lambda i, j: (i, 0)),
            pl.BlockSpec((1, d), lambda i, j: (0, 0)),
            pl.BlockSpec((d, tn), lambda i, j: (0, j)),
        ],
        out_specs=out_specs,
        out_shape=out_shape,
        scratch_shapes=[pltpu.VMEM((tm, d), BF16)],
        compiler_params=_params("arbitrary", "arbitrary"),
        name="in_proj",
    )(x, g, w)
    return (*outs, None) if not export_w else tuple(outs)


def _pool_kernel(u_ref, hist_ref, maps_ref, scale_ref, o_ref, ext_ref, *, pos0, zero_first_hist):
    s_blk, l_blk, width = u_ref.shape
    group = width // len(POOL_WINDOWS)
    tile = pl.program_id(1)
    u = u_ref[...]
    ext_ref[:, HALO_ROWS:, :] = u
    hist_rows = hist_ref.shape[1]
    ext_ref[:, HALO_ROWS - hist_rows:HALO_ROWS, :] = hist_ref[...]
    if hist_rows < HALO_ROWS:
        ext_ref[:, 0:HALO_ROWS - hist_rows, :] = jnp.zeros((s_blk, HALO_ROWS - hist_rows, width), F32)
    if zero_first_hist:
        @pl.when(tile == 0)
        def _():
            ext_ref[:, 0:HALO_ROWS, :] = jnp.zeros((s_blk, HALO_ROWS, width), F32)

    pos = pos0 + tile * l_blk + lax.broadcasted_iota(jnp.int32, (s_blk, l_blk, group), 1)
    outs = []
    for gi, w in enumerate(POOL_WINDOWS):
        cols = slice(gi * group, (gi + 1) * group)
        acc = ext_ref[:, :, cols]
        span = 1
        while span < w:
            acc = acc + pltpu.roll(acc, span, axis=1)
            span *= 2
        win = acc[:, HALO_ROWS:, :]
        cnt = jnp.minimum(w, pos + 1).astype(F32)
        d = win / cnt - u[:, :, cols]
        d2 = d.reshape(s_blk * l_blk, group).astype(BF16)
        outs.append(_dot(d2, maps_ref[gi]))
    m = jnp.concatenate(outs, axis=-1) * scale_ref[...]
    o_ref[...] = m.astype(o_ref.dtype)


def _pool_mix(proj3, hist, maps, scale, *, s_blk, l_blk, pos0, out_dtype):
    n_seq, seq_len, _ = proj3.shape
    n_groups, group, _ = maps.shape
    width = n_groups * group
    tiles = seq_len // l_blk
    if hist is None:
        halo_blocks = l_blk // HALO_ROWS
        hist_arr = proj3
        hist_spec = pl.BlockSpec((s_blk, HALO_ROWS, width),
                                 lambda s, t: (s, jnp.maximum(t * halo_blocks - 1, 0), 0))
    else:
        assert tiles == 1
        hist_arr = hist
        hist_spec = pl.BlockSpec((s_blk, hist.shape[1], width), lambda s, t: (s, 0, 0))
    kern = functools.partial(_pool_kernel, pos0=pos0, zero_first_hist=hist is None)
    return pl.pallas_call(
        kern,
        grid=(n_seq // s_blk, tiles),
        in_specs=[
            pl.BlockSpec((s_blk, l_blk, width), lambda s, t: (s, t, 0)),
            hist_spec,
            pl.BlockSpec((n_groups, group, group), lambda s, t: (0, 0, 0)),
            pl.BlockSpec((1, width), lambda s, t: (0, 0)),
        ],
        out_specs=pl.BlockSpec((s_blk * l_blk, width), lambda s, t: (s * tiles + t, 0)),
        out_shape=jax.ShapeDtypeStruct((n_seq * seq_len, width), out_dtype),
        scratch_shapes=[pltpu.VMEM((s_blk, HALO_ROWS + l_blk, width), F32)],
        compiler_params=_params("parallel", "arbitrary"),
        name="pool_mix",
    )(proj3, hist_arr, maps, scale)


def _topk_select(route, n_valid):
    nb, n = route.shape
    blk = lax.broadcasted_iota(jnp.int32, (nb, n), 0)
    rank = jnp.zeros((nb, n), jnp.int32)
    for other in range(nb):
        c = route[other:other + 1, :]
        beats = jnp.where((c > route) | ((c == route) & (other < blk)), 1, 0)
        rank = rank + beats * (other < n_valid).astype(jnp.int32)
    return (blk < n_valid) & (rank < MOBA_TOPK)


def _topk_select_list(scores):
    sel = []
    for j, s_j in enumerate(scores):
        rank = jnp.zeros(s_j.shape, jnp.int32)
        for other, s_o in enumerate(scores):
            if other < j:
                rank = rank + jnp.where(s_o >= s_j, 1, 0)
            elif other > j:
                rank = rank + jnp.where(s_o > s_j, 1, 0)
        sel.append(rank < MOBA_TOPK)
    return sel


KV_TILE = 2 * MOBA_BLOCK
assert KV_TILE == 2 * MOBA_BLOCK
ONES_ROWS = 16
M_FLOOR = -1e20
PROMPT_HEADS_PER_STEP = 4
SCORE_LOOKAHEAD = 4


def _prompt_attn_kernel(slopes_ref, q_ref, k_ref, v_ref, o_ref,
                        kb_ref, vt_ref, kmean_ref, alibi_ref, m_ref, acc_ref):
    hgroup = pl.program_id(1)
    qt = pl.program_id(2)
    n_hb, seq_len, dh = kb_ref.shape
    tq = q_ref.shape[1]
    nb = seq_len // MOBA_BLOCK
    blocks_per_tile = KV_TILE // MOBA_BLOCK
    slopes = [slopes_ref[hgroup * n_hb + hh] for hh in range(n_hb)]
    q_minus_k = (lax.broadcasted_iota(jnp.int32, (KV_TILE, tq), 1)
                 - lax.broadcasted_iota(jnp.int32, (KV_TILE, tq), 0))

    @pl.when(qt == 0)
    def _():
        for hh in range(n_hb):
            cols = slice(hh * dh, (hh + 1) * dh)
            k = k_ref[0, :, cols]
            kb_ref[hh] = k.astype(BF16)
            kmean_ref[hh] = jnp.mean(k.reshape(nb, MOBA_BLOCK, dh), axis=1)
            ones_rows = jnp.where(lax.broadcasted_iota(jnp.int32, (ONES_ROWS, MOBA_BLOCK), 0) == 0, 1.0, 0.0)
            for c in range(nb):
                vt_ref[hh, c, 0:dh, :] = v_ref[0, c * MOBA_BLOCK:(c + 1) * MOBA_BLOCK, cols].T.astype(BF16)
                vt_ref[hh, c, dh:, :] = ones_rows.astype(BF16)
            alibi_ref[hh] = slopes[hh] * q_minus_k.astype(F32)

    blk = lax.broadcasted_iota(jnp.int32, (nb, tq), 0)
    qbs, sel_bits = [], []
    for hh in range(n_hb):
        q = q_ref[0, :, hh * dh:(hh + 1) * dh]
        qbs.append((q * dh ** -0.5).astype(BF16))
        route = _dot_nt_3pass(kmean_ref[hh], q)
        sel = _topk_select(route, qt)
        bits = jnp.sum(jnp.where(sel, jnp.left_shift(1, blk).astype(F32), 0.0), axis=0, keepdims=True)
        sel_bits.append(bits.astype(jnp.int32))

    m_ref[...] = jnp.full(m_ref.shape, M_FLOOR, F32)
    acc_ref[...] = jnp.zeros(acc_ref.shape, F32)

    def block_bias(hh, j):
        bit = jnp.right_shift(sel_bits[hh], j) & 1
        return jnp.broadcast_to(jnp.where(bit == 1, 0.0, NEG_INF), (MOBA_BLOCK, tq))

    def attend(first_blk, n_blk, bias_fn, offset):
        keys = n_blk * MOBA_BLOCK
        start = pl.multiple_of(first_blk * MOBA_BLOCK, MOBA_BLOCK)
        score = lambda hh: _dot_nt(kb_ref[hh, pl.ds(start, keys), :], qbs[hh])
        scores = {hh: score(hh) for hh in range(min(SCORE_LOOKAHEAD, n_hb))}
        for hh in range(n_hb):
            if hh + SCORE_LOOKAHEAD < n_hb:
                scores[hh + SCORE_LOOKAHEAD] = score(hh + SCORE_LOOKAHEAD)
            z = scores.pop(hh) - alibi_ref[hh, 0:keys, :] + bias_fn(hh)
            shift = slopes[hh] * jnp.asarray(offset).astype(F32)
            m_old = m_ref[hh]
            m_new = jnp.maximum(m_old, jnp.max(z, axis=0, keepdims=True) - shift)
            alpha = jnp.exp(m_old - m_new)
            pr = jnp.exp(z - (m_new + shift))
            pv = None
            for b in range(n_blk):
                part = _dot(vt_ref[hh, first_blk + b], pr[b * MOBA_BLOCK:(b + 1) * MOBA_BLOCK].astype(BF16))
                pv = part if pv is None else pv + part
            acc_ref[hh] = alpha * acc_ref[hh] + pv
            m_ref[hh] = m_new

    def past_pair(p, carry):
        first = p * blocks_per_tile
        attend(first, blocks_per_tile,
               lambda hh: jnp.concatenate([block_bias(hh, first + b) for b in range(blocks_per_tile)], axis=0),
               (qt - first) * MOBA_BLOCK)
        return carry

    lax.fori_loop(0, qt // blocks_per_tile, past_pair, 0)

    causal = jnp.where(q_minus_k[0:MOBA_BLOCK, :] >= 0, 0.0, NEG_INF)

    @pl.when(qt % blocks_per_tile == 1)
    def _():
        attend(qt - 1, 2, lambda hh: jnp.concatenate([block_bias(hh, qt - 1), causal], axis=0), MOBA_BLOCK)

    @pl.when(qt % blocks_per_tile == 0)
    def _():
        attend(qt, 1, lambda hh: causal, 0)
    for hh in range(n_hb):
        acc = acc_ref[hh]
        o_ref[:, hh * dh:(hh + 1) * dh] = (acc[0:dh, :] / acc[dh:dh + 1, :]).T.astype(o_ref.dtype)


def _prompt_attention(q_src3, k3, v3, slopes, *, q_col):
    bsz, seq_len, w_att = k3.shape
    n_heads = w_att // HEAD_DIM
    tq = MOBA_BLOCK
    nq = seq_len // tq
    nb = seq_len // MOBA_BLOCK
    hb = PROMPT_HEADS_PER_STEP
    gw = hb * HEAD_DIM
    assert n_heads % hb == 0 and seq_len % KV_TILE == 0 and q_col % gw == 0
    qb0 = q_col // gw
    return pl.pallas_call(
        _prompt_attn_kernel,
        grid=(bsz, n_heads // hb, nq),
        in_specs=[
            pl.BlockSpec(memory_space=pltpu.SMEM),
            pl.BlockSpec((1, tq, gw), lambda b, h, i: (b, i, qb0 + h)),
            pl.BlockSpec((1, seq_len, gw), lambda b, h, i: (b, 0, h)),
            pl.BlockSpec((1, seq_len, gw), lambda b, h, i: (b, 0, h)),
        ],
        out_specs=pl.BlockSpec((tq, gw), lambda b, h, i: (b * nq + i, h)),
        out_shape=jax.ShapeDtypeStruct((bsz * seq_len, n_heads * HEAD_DIM), BF16),
        scratch_shapes=[
            pltpu.VMEM((hb, seq_len, HEAD_DIM), BF16),
            pltpu.VMEM((hb, nb, HEAD_DIM + ONES_ROWS, MOBA_BLOCK), BF16),
            pltpu.VMEM((hb, nb, HEAD_DIM), F32),
            pltpu.VMEM((hb, KV_TILE, tq), F32),
            pltpu.VMEM((hb, 1, tq), F32),
            pltpu.VMEM((hb, HEAD_DIM + ONES_ROWS, tq), F32),
        ],
        compiler_params=_params("parallel", "parallel", "arbitrary"),
        name="prompt_attn",
    )(slopes, q_src3, k3, v3)


def _heads_to_rows(x, n_heads, dh):
    return jnp.concatenate([x[:, h * dh:(h + 1) * dh] for h in range(n_heads)], axis=0)


def _sample_attn_kernel(pt_ref, slopes_ref, q_ref, kn_ref, vn_ref, *rest, past_len, n_pages):
    del pt_ref
    k_refs, v_refs, o_ref = rest[:n_pages], rest[n_pages:2 * n_pages], rest[2 * n_pages]
    t = q_ref.shape[1]
    page, n_heads, dh = k_refs[0].shape
    pages_per_block = MOBA_BLOCK // page
    nb = n_pages // pages_per_block
    rows = n_heads * t
    page_keys = page * n_heads

    q2 = _heads_to_rows(q_ref[0], n_heads, dh)
    q2b = (q2 * dh ** -0.5).astype(BF16)
    row_head = lax.broadcasted_iota(jnp.int32, (rows, 1), 0) // t
    row_query = lax.broadcasted_iota(jnp.int32, (rows, 1), 0) % t
    slope_rows = jnp.zeros((rows, 1), F32)
    for h in range(n_heads):
        slope_rows = jnp.where(row_head == h, slopes_ref[h], slope_rows)
    lane = lax.broadcasted_iota(jnp.int32, (rows, page_keys), 1)
    page_bias = jnp.where(lane % n_heads == row_head,
                          slope_rows * (row_query - lane // n_heads).astype(F32), -NEG_INF)

    def block_scores(b):
        ksum = jnp.zeros((n_heads, dh), F32)
        s_pages = []
        for pg in range(b * pages_per_block, (b + 1) * pages_per_block):
            kp = k_refs[pg][...]
            ksum = ksum + jnp.sum(kp, axis=0)
            s_pg = _dot_nt(q2b, kp.reshape(page_keys, dh).astype(BF16))
            s_pages.append(s_pg - page_bias - slope_rows * float(past_len - pg * page))
        kmean = ksum / MOBA_BLOCK
        kmean_rows = jnp.concatenate([jnp.broadcast_to(kmean[h:h + 1, :], (t, dh)) for h in range(n_heads)], axis=0)
        return jnp.sum(q2 * kmean_rows, axis=-1, keepdims=True), jnp.concatenate(s_pages, axis=-1)

    route, m_blk, l_blk, acc_blk = [], [], [], []
    ahead = block_scores(0)
    for b in range(nb):
        route_b, s_b = ahead
        if b + 1 < nb:
            ahead = block_scores(b + 1)
        pages = range(b * pages_per_block, (b + 1) * pages_per_block)
        vblk = jnp.concatenate([v_refs[pg][...].reshape(page_keys, dh) for pg in pages], axis=0)
        m_b = jnp.max(s_b, axis=-1, keepdims=True)
        p = jnp.exp(s_b - m_b)
        route.append(route_b)
        m_blk.append(m_b)
        l_blk.append(jnp.sum(p, axis=-1, keepdims=True))
        acc_blk.append(_dot(p.astype(BF16), vblk.astype(BF16)))

    sel = _topk_select_list(route)
    kn2 = _heads_to_rows(kn_ref[0], n_heads, dh)
    vn2 = _heads_to_rows(vn_ref[0], n_heads, dh)
    col = lax.broadcasted_iota(jnp.int32, (rows, rows), 1)
    dist = row_query - col % t
    keep = (col // t == row_head) & (dist >= 0)
    s_own = _dot_nt(q2b, kn2.astype(BF16)) - slope_rows * dist.astype(F32)
    s_own = jnp.where(keep, s_own, NEG_INF)
    m_all = jnp.max(s_own, axis=-1, keepdims=True)
    for b in range(nb):
        m_all = jnp.maximum(m_all, jnp.where(sel[b], m_blk[b], NEG_INF))
    p_own = jnp.where(keep, jnp.exp(s_own - m_all), 0.0)
    den = jnp.sum(p_own, axis=-1, keepdims=True)
    num = _dot(p_own.astype(BF16), vn2.astype(BF16))
    for b in range(nb):
        w_b = jnp.where(sel[b], jnp.exp(m_blk[b] - m_all), 0.0)
        den = den + w_b * l_blk[b]
        num = num + w_b * acc_blk[b]
    out = num / den
    o_ref[...] = jnp.concatenate([out[h * t:(h + 1) * t, :] for h in range(n_heads)], axis=-1)


def _sample_attention(q_src3, k3, v3, cache_k, cache_v, layer, page_table, slopes, *, q_col):
    n_seq, t, _ = k3.shape
    _, _, page, n_heads, dh = cache_k.shape
    width = n_heads * dh
    n_pages = page_table.shape[1]
    assert MOBA_BLOCK % page == 0 and (n_pages * page) % MOBA_BLOCK == 0 and t <= MOBA_BLOCK
    assert q_col % width == 0
    past_len = n_pages * page
    qb0 = q_col // width
    page_specs = [pl.BlockSpec((None, None, page, n_heads, dh), lambda s, pt, pg=pg: (layer, pt[s, pg], 0, 0, 0))
                  for pg in range(n_pages)]
    grid_spec = pltpu.PrefetchScalarGridSpec(
        num_scalar_prefetch=1,
        grid=(n_seq,),
        in_specs=[
            pl.BlockSpec(memory_space=pltpu.SMEM),
            pl.BlockSpec((1, t, width), lambda s, pt: (s, 0, qb0)),
            pl.BlockSpec((1, t, width), lambda s, pt: (s, 0, 0)),
            pl.BlockSpec((1, t, width), lambda s, pt: (s, 0, 0)),
            *page_specs, *page_specs,
        ],
        out_specs=pl.BlockSpec((t, width), lambda s, pt: (s, 0)),
    )
    return pl.pallas_call(
        functools.partial(_sample_attn_kernel, past_len=past_len, n_pages=n_pages),
        grid_spec=grid_spec,
        out_shape=jax.ShapeDtypeStruct((n_seq * t, width), F32),
        compiler_params=_params("parallel"),
        name="sample_attn",
    )(page_table, slopes, q_src3, k3, v3, *([cache_k] * n_pages), *([cache_v] * n_pages))


def _merge_kernel(a_ref, b_ref, wp_ref, wa_ref, ga_ref, gb_ref, o_ref):
    pa = _dot(a_ref[...].astype(BF16), wp_ref[...])
    pb = _dot(b_ref[...].astype(BF16), wa_ref[...])
    merged = jax.nn.sigmoid(ga_ref[...]) * pa + jax.nn.sigmoid(gb_ref[...]) * pb
    o_ref[...] = merged.astype(o_ref.dtype)


def _merge(a_out, b_out, w_pool_up, w_att_up, proj, *, ga_col, gb_col, tm, tn):
    rows, wa = a_out.shape
    wb = b_out.shape[1]
    n = w_pool_up.shape[1]
    ga0, gb0 = ga_col // tn, gb_col // tn
    return pl.pallas_call(
        _merge_kernel,
        grid=(rows // tm, n // tn),
        in_specs=[
            pl.BlockSpec((tm, wa), lambda i, j: (i, 0)),
            pl.BlockSpec((tm, wb), lambda i, j: (i, 0)),
            pl.BlockSpec((wa, tn), lambda i, j: (0, j)),
            pl.BlockSpec((wb, tn), lambda i, j: (0, j)),
            pl.BlockSpec((tm, tn), lambda i, j: (i, ga0 + j)),
            pl.BlockSpec((tm, tn), lambda i, j: (i, gb0 + j)),
        ],
        out_specs=pl.BlockSpec((tm, tn), lambda i, j: (i, j)),
        out_shape=jax.ShapeDtypeStruct((rows, n), BF16),
        compiler_params=_params("parallel", "arbitrary"),
        name="merge",
    )(a_out, b_out, w_pool_up, w_att_up, proj, proj)


def _matmul_residual_kernel(a_ref, w_ref, r_ref, o_ref, *w_outs):
    wb = w_ref[...].astype(BF16)
    if w_outs:
        w_outs[0][...] = wb
    o_ref[...] = r_ref[...] + _dot(a_ref[...], wb)


def _matmul_residual(a, w, res, *, tm, tn):
    rows, k = a.shape
    n = w.shape[1]
    export_w = w.dtype != BF16
    out_specs = [pl.BlockSpec((tm, tn), lambda i, j: (i, j))]
    out_shape = [jax.ShapeDtypeStruct((rows, n), F32)]
    if export_w:
        out_specs.append(pl.BlockSpec((k, tn), lambda i, j: (0, j)))
        out_shape.append(jax.ShapeDtypeStruct((k, n), BF16))
    outs = pl.pallas_call(
        _matmul_residual_kernel,
        grid=(rows // tm, n // tn),
        in_specs=[
            pl.BlockSpec((tm, k), lambda i, j: (i, 0)),
            pl.BlockSpec((k, tn), lambda i, j: (0, j)),
            pl.BlockSpec((tm, tn), lambda i, j: (i, j)),
        ],
        out_specs=out_specs,
        out_shape=out_shape,
        compiler_params=_params("arbitrary", "arbitrary"),
        name="out_proj",
    )(a, w, res)
    return outs[0], (outs[1] if export_w else None)


FFN_CHUNK_COLS = 256


def _gelu_exact(x):
    return 0.5 * x * (1.0 + lax.erf(x * (2.0 ** -0.5)))


def _ffn_kernel(h_ref, g2_ref, wa_ref, wg_ref, cw_ref, cb_ref, wd_ref, gf_ref, *rest,
                seq_rows, tiles_per_seq, per_seq_hist, export_w):
    hist_ref = rest[0] if per_seq_hist else None
    n_in = 1 if per_seq_hist else 0
    y_ref, tail_ref = rest[n_in:n_in + 2]
    wa_out, wg_out, wd_out = rest[n_in + 2:n_in + 5] if export_w else (None, None, None)
    hn_ref, carry_ref = rest[-2:]
    i = pl.program_id(0)
    j = pl.program_id(1)
    tm = h_ref.shape[0]
    tn = wa_ref.shape[1]
    sub = FFN_CHUNK_COLS
    n_chunks = tn // sub

    @pl.when(j == 0)
    def _():
        h = h_ref[...]
        hn_ref[...] = _rmsnorm(h, g2_ref[...]).astype(BF16)
        y_ref[...] = h

    if not per_seq_hist:
        @pl.when(i % tiles_per_seq == 0)
        def _():
            carry_ref[j] = jnp.zeros((SUBLANES, tn), F32)

    hn = hn_ref[...]
    chunks = [slice(c * sub, (c + 1) * sub) for c in range(n_chunks)]
    wa_parts = [wa_ref[:, cols].astype(BF16) for cols in chunks]
    wg_parts = [wg_ref[:, cols].astype(BF16) for cols in chunks]
    wd_parts = [wd_ref[cols, :].astype(BF16) for cols in chunks]
    if export_w:
        for cols, wa_c, wg_c, wd_c in zip(chunks, wa_parts, wg_parts, wd_parts):
            wa_out[:, cols] = wa_c
            wg_out[:, cols] = wg_c
            wd_out[cols, :] = wd_c
    a_parts = [_dot(hn, wa_c) for wa_c in wa_parts]
    g_parts = [_dot(hn, wg_c) for wg_c in wg_parts]
    row = lax.broadcasted_iota(jnp.int32, (tm, sub), 0)
    down = None
    for cols, a, gv, wd_c in zip(chunks, a_parts, g_parts, wd_parts):
        if per_seq_hist:
            n_seq = tm // seq_rows
            hist = hist_ref[:, :, cols]
            prev = [jnp.broadcast_to(hist[:, r:r + 1, :], (n_seq, seq_rows, sub)).reshape(tm, sub)
                    for r in range(CONV_W - 1)]
            t_in_seq = row % seq_rows
            tail_ref[:, :, cols] = a.reshape(n_seq, seq_rows, sub)[:, seq_rows - (CONV_W - 1):, :]
        else:
            carry = carry_ref[j, :, cols]
            first = SUBLANES - (CONV_W - 1)
            prev = [jnp.broadcast_to(carry[first + r:first + r + 1, :], (tm, sub)) for r in range(CONV_W - 1)]
            t_in_seq = row
            tail = a[tm - SUBLANES:, :]
            carry_ref[j, :, cols] = tail
            tail_ref[0, :, cols] = tail
        ac = cb_ref[:, cols] + cw_ref[CONV_W - 1:CONV_W, cols] * a
        for back in range(1, CONV_W):
            shifted = pltpu.roll(a, back, axis=0)
            for r in range(back):
                shifted = jnp.where(t_in_seq == r, prev[CONV_W - 1 - back + r], shifted)
            ac = ac + cw_ref[CONV_W - 1 - back:CONV_W - back, cols] * shifted
        act = (_gelu_exact(ac) * gv).astype(BF16)
        part = _dot(act, wd_c)
        down = part if down is None else down + part
    y_ref[...] += down

    @pl.when(j == pl.num_programs(1) - 1)
    def _():
        y_ref[...] = _rmsnorm(y_ref[...], gf_ref[...])


def _ffn(h, g2, w_up, conv_w, conv_b, w_down, gf, hist, *, seq_rows, tm, tn):
    rows, d = h.shape
    d_ff = w_down.shape[0]
    n_col = d_ff // tn
    n_row = rows // tm
    per_seq_hist = hist is not None
    export_w = not isinstance(w_up, tuple)
    if export_w:
        w_a, w_g, g_off = w_up, w_up, n_col
        assert w_up.dtype == F32 and w_down.dtype == F32
    else:
        (w_a, w_g), g_off = w_up, 0
    if per_seq_hist:
        assert tm % seq_rows == 0 and seq_rows >= CONV_W - 1
        tiles_per_seq = 1
        tail_shape = jax.ShapeDtypeStruct((rows // seq_rows, CONV_W - 1, d_ff), F32)
        tail_spec = pl.BlockSpec((tm // seq_rows, CONV_W - 1, tn), lambda i, j: (i, 0, j))
        hist_args = (hist,)
        hist_specs = [pl.BlockSpec((tm // seq_rows, CONV_W - 1, tn), lambda i, j: (i, 0, j))]
    else:
        assert seq_rows % tm == 0
        tiles_per_seq = seq_rows // tm
        tail_shape = jax.ShapeDtypeStruct((n_row, SUBLANES, d_ff), F32)
        tail_spec = pl.BlockSpec((1, SUBLANES, tn), lambda i, j: (i, 0, j))
        hist_args = ()
        hist_specs = []
    out_specs = [pl.BlockSpec((tm, d), lambda i, j: (i, 0)), tail_spec]
    out_shape = [jax.ShapeDtypeStruct((rows, d), F32), tail_shape]
    if export_w:
        out_specs += [pl.BlockSpec((d, tn), lambda i, j: (0, j)), pl.BlockSpec((d, tn), lambda i, j: (0, j)),
                      pl.BlockSpec((tn, d), lambda i, j: (j, 0))]
        out_shape += [jax.ShapeDtypeStruct((d, d_ff), BF16), jax.ShapeDtypeStruct((d, d_ff), BF16),
                      jax.ShapeDtypeStruct((d_ff, d), BF16)]
    outs = pl.pallas_call(
        functools.partial(_ffn_kernel, seq_rows=seq_rows, tiles_per_seq=tiles_per_seq, per_seq_hist=per_seq_hist,
                          export_w=export_w),
        grid=(n_row, n_col),
        in_specs=[
            pl.BlockSpec((tm, d), lambda i, j: (i, 0), pipeline_mode=pl.Buffered(1 if export_w else 2)),
            pl.BlockSpec((1, d), lambda i, j: (0, 0)),
            pl.BlockSpec((d, tn), lambda i, j: (0, j)),
            pl.BlockSpec((d, tn), lambda i, j: (0, g_off + j)),
            pl.BlockSpec((CONV_W, tn), lambda i, j: (0, j)),
            pl.BlockSpec((1, tn), lambda i, j: (0, j)),
            pl.BlockSpec((tn, d), lambda i, j: (j, 0)),
            pl.BlockSpec((1, d), lambda i, j: (0, 0)),
            *hist_specs,
        ],
        out_specs=out_specs,
        out_shape=out_shape,
        scratch_shapes=[
            pltpu.VMEM((tm, d), BF16),
            pltpu.VMEM((n_col, SUBLANES, tn), F32),
        ],
        compiler_params=_params("arbitrary", "arbitrary"),
        name="ffn",
    )(h, g2, w_a, w_g, conv_w, conv_b, w_down, gf, *hist_args)
    return outs[0], outs[1], (tuple(outs[2:]) if export_w else None)


def _layer(x3, pool_hist, conv_hist, attend, pos0, w, *, tm, in_tn, ffn_tn, mid_tile):
    n_seq, seq_len, d = x3.shape
    rows = n_seq * seq_len
    x = x3.reshape(rows, d)
    w_pool = w["pool_scale"].shape[1]
    w_att = w["w_att_up"].shape[0]
    rest, k, v, w_in_b = _in_proj(x, w["g1"], w["w_in"], k_col=w_pool + w_att, kv_width=w_att, tm=tm, tn=in_tn)
    rest3 = rest.reshape(n_seq, seq_len, -1)
    k3, v3 = k.reshape(n_seq, seq_len, w_att), v.reshape(n_seq, seq_len, w_att)
    q_col, ga_col, gb_col = w_pool, w_pool + w_att, w_pool + w_att + d
    if pool_hist is None:
        a_out = _pool_mix(rest3, None, w["pool_maps"], w["pool_scale"], s_blk=1, l_blk=POOL_TILE_ROWS, pos0=pos0,
                          out_dtype=BF16)
    else:
        a_out = _pool_mix(rest3, pool_hist, w["pool_maps"], w["pool_scale"], s_blk=POOL_SEQS_PER_STEP, l_blk=seq_len,
                          pos0=pos0, out_dtype=F32)
    b_out = attend(rest3, k3, v3, q_col)
    mid_tm, mid_tn = mid_tile
    merged = _merge(a_out, b_out, w["w_pool_up"], w["w_att_up"], rest, ga_col=ga_col, gb_col=gb_col,
                    tm=mid_tm, tn=mid_tn)
    h, w_out_b = _matmul_residual(merged, w["w_out"], x, tm=mid_tm, tn=mid_tn)
    y, tail, ffn_w = _ffn(h, w["g2"], w["w_up"], w["conv_w"], w["conv_b"], w["w_down"], w["gf"], conv_hist,
                          seq_rows=seq_len, tm=tm, tn=ffn_tn)
    return y.reshape(n_seq, seq_len, d), rest3, k3, v3, tail, (w_in_b, w_out_b, ffn_w)


def kernel(x_prompt, x_sample, cache_k, cache_v, state_pool, state_conv, page_table, norm1_g, w_in, pool_maps,
           pool_scale, w_pool_up, w_att_up, w_out, norm2_g, w_up, conv_w, conv_b, w_down, final_norm_g):
    depth = w_in.shape[0]
    assert depth == 1, "single-layer trunk"
    bsz, seq, d = x_prompt.shape
    n_dec, dec_seq, _ = x_sample.shape
    _, n_pool, page, n_heads, dh = cache_k.shape
    assert dh == HEAD_DIM
    w_att = n_heads * dh
    w_pool = pool_scale.shape[1]
    d_ff = w_down.shape[1]
    past_len = page_table.shape[1] * page
    slopes = 2.0 ** (-8.0 * jnp.arange(1, n_heads + 1, dtype=F32) / n_heads)

    l = 0
    w = dict(
        g1=norm1_g[l][None, :], w_in=w_in[l], pool_maps=pool_maps[l].astype(BF16),
        pool_scale=pool_scale[l][None, :], w_pool_up=w_pool_up[l].astype(BF16), w_att_up=w_att_up[l].astype(BF16),
        w_out=w_out[l], g2=norm2_g[l][None, :], w_up=w_up[l], conv_w=conv_w[l],
        conv_b=conv_b[l][None, :], w_down=w_down[l], gf=final_norm_g[None, :],
    )
    tm = ROW_TILE

    attend_s = lambda q_src3, k3, v3, qc: _sample_attention(q_src3, k3, v3, cache_k, cache_v, l, page_table, slopes,
                                                            q_col=qc)
    y_s, rest_s, k_s, v_s, tail_s, (w_in_b, w_out_b, (w_ua_b, w_ug_b, w_down_b)) = _layer(
        x_sample, state_pool[l], state_conv[l], attend_s, past_len, w, tm=tm, in_tn=512, ffn_tn=256,
        mid_tile=(tm, 1024))

    w_p = dict(w, w_in=w_in_b, w_out=w_out_b, w_up=(w_ua_b, w_ug_b), w_down=w_down_b)
    attend_p = lambda q_src3, k3, v3, qc: _prompt_attention(q_src3, k3, v3, slopes, q_col=qc)
    y_p, rest_p, k_p, v_p, tail_p, _ = _layer(x_prompt, None, None, attend_p, 0, w_p, tm=tm, in_tn=1024, ffn_tn=512,
                                              mid_tile=(tm // 2, d))

    heads = lambda a3: a3.reshape(a3.shape[0], a3.shape[1], n_heads, dh)[None]
    pool_p = rest_p[:, seq - POOL_HIST:, :w_pool][None]
    pool_s = jnp.concatenate([state_pool[l], rest_s[:, :, :w_pool]], axis=1)[:, -POOL_HIST:][None]
    tiles_per_seq = seq // tm
    conv_p = tail_p.reshape(bsz, tiles_per_seq, SUBLANES, d_ff)[:, -1, SUBLANES - (CONV_W - 1):][None]
    conv_s = tail_s[None]
    return (y_p, y_s, heads(k_p), heads(v_p), heads(k_s), heads(v_s), pool_p, pool_s, conv_p, conv_s)
```

```python
import functools

import jax
import jax.numpy as jnp
from jax import lax
from jax.experimental import pallas as pl
from jax.experimental.pallas import tpu as pltpu

POOL_WINDOWS = (2, 4, 8, 16)
assert all(w & (w - 1) == 0 for w in POOL_WINDOWS)
POOL_HIST = max(POOL_WINDOWS) - 1
HEAD_DIM = 128
MOBA_BLOCK = 256
MOBA_TOPK = 3
CONV_W = 3
RMS_EPS = 1e-6
NEG_INF = -1e30

SUBLANES = 8
HALO_ROWS = 16
ROW_TILE = 1024
POOL_TILE_ROWS = 1024
POOL_SEQS_PER_STEP = 64
IN_PROJ_COLS_F32, IN_PROJ_COLS_BF16 = 512, 1024
FFN_COLS_F32, FFN_COLS_BF16 = 256, 512
OUT_PROJ_COLS_F32 = 1024
V7X_VMEM_BYTES = 64 * 1024 * 1024
VMEM_LIMIT_BYTES = V7X_VMEM_BYTES - 6 * 1024 * 1024

BF16 = jnp.bfloat16
F32 = jnp.float32


def _params(*semantics):
    return pltpu.CompilerParams(dimension_semantics=semantics, vmem_limit_bytes=VMEM_LIMIT_BYTES)


def _rmsnorm(x, g):
    return x * lax.rsqrt(jnp.mean(x * x, axis=-1, keepdims=True) + RMS_EPS) * g


def _dot(a, b):
    return jnp.dot(a, b, preferred_element_type=F32)


def _dot_nt(a, b):
    return lax.dot_general(a, b, (((1,), (1,)), ((), ())), preferred_element_type=F32)


def _split_bf16(x):
    hi = x.astype(BF16)
    lo = (x - hi.astype(F32)).astype(BF16)
    return hi, lo


def _dot_nt_3pass(a, b):
    a_hi, a_lo = _split_bf16(a)
    b_hi, b_lo = _split_bf16(b)
    return _dot_nt(a_hi, b_hi) + (_dot_nt(a_hi, b_lo) + _dot_nt(a_lo, b_hi))


def _in_proj_kernel(x_ref, g_ref, w_ref, rest_ref, k_ref, v_ref, *more, k_tiles, v_tiles, export_w):
    xn_ref = more[-1]
    j = pl.program_id(1)

    @pl.when(j == 0)
    def _():
        xn_ref[...] = _rmsnorm(x_ref[...], g_ref[...]).astype(BF16)

    def project_into(o_ref):
        wb = w_ref[...].astype(BF16)
        if export_w:
            more[0][...] = wb
        o_ref[...] = _dot(xn_ref[...], wb)

    is_k = (j >= k_tiles[0]) & (j < k_tiles[1])
    is_v = (j >= v_tiles[0]) & (j < v_tiles[1])
    pl.when(is_k)(lambda: project_into(k_ref))
    pl.when(is_v)(lambda: project_into(v_ref))
    pl.when(jnp.logical_not(is_k | is_v))(lambda: project_into(rest_ref))


def _in_proj(x, g, w, *, k_col, kv_width, tm, tn):
    rows, d = x.shape
    n = w.shape[1]
    export_w = w.dtype != BF16
    assert k_col % tn == 0 and kv_width % tn == 0
    k0, nkv = k_col // tn, kv_width // tn
    v0, after = k0 + nkv, k0 + 2 * nkv
    rest_map = lambda i, j: (i, jnp.where(j < k0, j, jnp.maximum(j - 2 * nkv, k0 - 1)))
    k_map = lambda i, j: (i, jnp.clip(j - k0, 0, nkv - 1))
    v_map = lambda i, j: (i, jnp.clip(j - v0, 0, nkv - 1))
    out_specs = [pl.BlockSpec((tm, tn), rest_map), pl.BlockSpec((tm, tn), k_map), pl.BlockSpec((tm, tn), v_map)]
    out_shape = [jax.ShapeDtypeStruct((rows, n - 2 * kv_width), F32),
                 jax.ShapeDtypeStruct((rows, kv_width), F32), jax.ShapeDtypeStruct((rows, kv_width), F32)]
    if export_w:
        out_specs.append(pl.BlockSpec((d, tn), lambda i, j: (0, j)))
        out_shape.append(jax.ShapeDtypeStruct((d, n), BF16))
    outs = pl.pallas_call(
        functools.partial(_in_proj_kernel, k_tiles=(k0, v0), v_tiles=(v0, after), export_w=export_w),
        grid=(rows // tm, n // tn),
        in_specs=[
            pl.BlockSpec((tm, d), lambda i, j: (i, 0)),
            pl.BlockSpec((1, d), lambda i, j: (0, 0)),
            pl.BlockSpec((d, tn), lambda i, j: (0, j)),
        ],
        out_specs=out_specs,
        out_shape=out_shape,
        scratch_shapes=[pltpu.VMEM((tm, d), BF16)],
        compiler_params=_params("arbitrary", "arbitrary"),
        name="in_proj",
    )(x, g, w)
    return (*outs, None) if not export_w else tuple(outs)


def _pool_kernel(u_ref, hist_ref, maps_ref, scale_ref, o_ref, ext_ref, *, pos0, zero_first_hist):
    s_blk, l_blk, width = u_ref.shape
    group = width // len(POOL_WINDOWS)
    tile = pl.program_id(1)
    u = u_ref[...]
    ext_ref[:, HALO_ROWS:, :] = u
    hist_rows = hist_ref.shape[1]
    ext_ref[:, HALO_ROWS - hist_rows:HALO_ROWS, :] = hist_ref[...]
    if hist_rows < HALO_ROWS:
        ext_ref[:, 0:HALO_ROWS - hist_rows, :] = jnp.zeros((s_blk, HALO_ROWS - hist_rows, width), F32)
    if zero_first_hist:
        @pl.when(tile == 0)
        def _():
            ext_ref[:, 0:HALO_ROWS, :] = jnp.zeros((s_blk, HALO_ROWS, width), F32)

    pos = pos0 + tile * l_blk + lax.broadcasted_iota(jnp.int32, (s_blk, l_blk, group), 1)
    outs = []
    for gi, w in enumerate(POOL_WINDOWS):
        cols = slice(gi * group, (gi + 1) * group)
        acc = ext_ref[:, :, cols]
        span = 1
        while span < w:
            acc = acc + pltpu.roll(acc, span, axis=1)
            span *= 2
        win = acc[:, HALO_ROWS:, :]
        cnt = jnp.minimum(w, pos + 1).astype(F32)
        d = win / cnt - u[:, :, cols]
        d2 = d.reshape(s_blk * l_blk, group).astype(BF16)
        outs.append(_dot(d2, maps_ref[gi]))
    m = jnp.concatenate(outs, axis=-1) * scale_ref[...]
    o_ref[...] = m.astype(o_ref.dtype)


def _pool_mix(proj3, hist, maps, scale, *, s_blk, l_blk, pos0, out_dtype):
    n_seq, seq_len, _ = proj3.shape
    n_groups, group, _ = maps.shape
    width = n_groups * group
    tiles = seq_len // l_blk
    if hist is None:
        halo_blocks = l_blk // HALO_ROWS
        hist_arr = proj3
        hist_spec = pl.BlockSpec((s_blk, HALO_ROWS, width),
                                 lambda s, t: (s, jnp.maximum(t * halo_blocks - 1, 0), 0))
    else:
        assert tiles == 1
        hist_arr = hist
        hist_spec = pl.BlockSpec((s_blk, hist.shape[1], width), lambda s, t: (s, 0, 0))
    kern = functools.partial(_pool_kernel, pos0=pos0, zero_first_hist=hist is None)
    return pl.pallas_call(
        kern,
        grid=(n_seq // s_blk, tiles),
        in_specs=[
            pl.BlockSpec((s_blk, l_blk, width), lambda s, t: (s, t, 0)),
            hist_spec,
            pl.BlockSpec((n_groups, group, group), lambda s, t: (0, 0, 0)),
            pl.BlockSpec((1, width), lambda s, t: (0, 0)),
        ],
        out_specs=pl.BlockSpec((s_blk * l_blk, width), lambda s, t: (s * tiles + t, 0)),
        out_shape=jax.ShapeDtypeStruct((n_seq * seq_len, width), out_dtype),
        scratch_shapes=[pltpu.VMEM((s_blk, HALO_ROWS + l_blk, width), F32)],
        compiler_params=_params("parallel", "arbitrary"),
        name="pool_mix",
    )(proj3, hist_arr, maps, scale)


def _topk_select(route, n_valid):
    nb, n = route.shape
    blk = lax.broadcasted_iota(jnp.int32, (nb, n), 0)
    rank = jnp.zeros((nb, n), jnp.int32)
    for other in range(nb):
        c = route[other:other + 1, :]
        beats = jnp.where((c > route) | ((c == route) & (other < blk)), 1, 0)
        rank = rank + beats * (other < n_valid).astype(jnp.int32)
    return (blk < n_valid) & (rank < MOBA_TOPK)


def _topk_select_list(scores):
    sel = []
    for j, s_j in enumerate(scores):
        rank = jnp.zeros(s_j.shape, jnp.int32)
        for other, s_o in enumerate(scores):
            if other < j:
                rank = rank + jnp.where(s_o >= s_j, 1, 0)
            elif other > j:
                rank = rank + jnp.where(s_o > s_j, 1, 0)
        sel.append(rank < MOBA_TOPK)
    return sel


KV_TILE = 2 * MOBA_BLOCK
assert KV_TILE == 2 * MOBA_BLOCK
ONES_ROWS = 16
M_FLOOR = -1e20
PROMPT_HEADS_PER_STEP = 4
SCORE_LOOKAHEAD = 4


def _prompt_attn_kernel(slopes_ref, q_ref, k_ref, v_ref, o_ref,
                        kb_ref, vt_ref, kmean_ref, alibi_ref, m_ref, acc_ref):
    hgroup = pl.program_id(1)
    qt = pl.program_id(2)
    n_hb, seq_len, dh = kb_ref.shape
    tq = q_ref.shape[1]
    nb = seq_len // MOBA_BLOCK
    blocks_per_tile = KV_TILE // MOBA_BLOCK
    slopes = [slopes_ref[hgroup * n_hb + hh] for hh in range(n_hb)]
    q_minus_k = (lax.broadcasted_iota(jnp.int32, (KV_TILE, tq), 1)
                 - lax.broadcasted_iota(jnp.int32, (KV_TILE, tq), 0))

    @pl.when(qt == 0)
    def _():
        for hh in range(n_hb):
            cols = slice(hh * dh, (hh + 1) * dh)
            k = k_ref[0, :, cols]
            kb_ref[hh] = k.astype(BF16)
            kmean_ref[hh] = jnp.mean(k.reshape(nb, MOBA_BLOCK, dh), axis=1)
            ones_rows = jnp.where(lax.broadcasted_iota(jnp.int32, (ONES_ROWS, MOBA_BLOCK), 0) == 0, 1.0, 0.0)
            for c in range(nb):
                vt_ref[hh, c, 0:dh, :] = v_ref[0, c * MOBA_BLOCK:(c + 1) * MOBA_BLOCK, cols].T.astype(BF16)
                vt_ref[hh, c, dh:, :] = ones_rows.astype(BF16)
            alibi_ref[hh] = slopes[hh] * q_minus_k.astype(F32)

    blk = lax.broadcasted_iota(jnp.int32, (nb, tq), 0)
    qbs, sel_bits = [], []
    for hh in range(n_hb):
        q = q_ref[0, :, hh * dh:(hh + 1) * dh]
        qbs.append((q * dh ** -0.5).astype(BF16))
        route = _dot_nt_3pass(kmean_ref[hh], q)
        sel = _topk_select(route, qt)
        bits = jnp.sum(jnp.where(sel, jnp.left_shift(1, blk).astype(F32), 0.0), axis=0, keepdims=True)
        sel_bits.append(bits.astype(jnp.int32))

    m_ref[...] = jnp.full(m_ref.shape, M_FLOOR, F32)
    acc_ref[...] = jnp.zeros(acc_ref.shape, F32)

    def block_bias(hh, j):
        bit = jnp.right_shift(sel_bits[hh], j) & 1
        return jnp.broadcast_to(jnp.where(bit == 1, 0.0, NEG_INF), (MOBA_BLOCK, tq))

    def attend(first_blk, n_blk, bias_fn, offset):
        keys = n_blk * MOBA_BLOCK
        start = pl.multiple_of(first_blk * MOBA_BLOCK, MOBA_BLOCK)
        score = lambda hh: _dot_nt(kb_ref[hh, pl.ds(start, keys), :], qbs[hh])
        scores = {hh: score(hh) for hh in range(min(SCORE_LOOKAHEAD, n_hb))}
        for hh in range(n_hb):
            if hh + SCORE_LOOKAHEAD < n_hb:
                scores[hh + SCORE_LOOKAHEAD] = score(hh + SCORE_LOOKAHEAD)
            z = scores.pop(hh) - alibi_ref[hh, 0:keys, :] + bias_fn(hh)
            shift = slopes[hh] * jnp.asarray(offset).astype(F32)
            m_old = m_ref[hh]
            m_new = jnp.maximum(m_old, jnp.max(z, axis=0, keepdims=True) - shift)
            alpha = jnp.exp(m_old - m_new)
            pr = jnp.exp(z - (m_new + shift))
            pv = None
            for b in range(n_blk):
                part = _dot(vt_ref[hh, first_blk + b], pr[b * MOBA_BLOCK:(b + 1) * MOBA_BLOCK].astype(BF16))
                pv = part if pv is None else pv + part
            acc_ref[hh] = alpha * acc_ref[hh] + pv
            m_ref[hh] = m_new

    def past_pair(p, carry):
        first = p * blocks_per_tile
        attend(first, blocks_per_tile,
               lambda hh: jnp.concatenate([block_bias(hh, first + b) for b in range(blocks_per_tile)], axis=0),
               (qt - first) * MOBA_BLOCK)
        return carry

    lax.fori_loop(0, qt // blocks_per_tile, past_pair, 0)

    causal = jnp.where(q_minus_k[0:MOBA_BLOCK, :] >= 0, 0.0, NEG_INF)

    @pl.when(qt % blocks_per_tile == 1)
    def _():
        attend(qt - 1, 2, lambda hh: jnp.concatenate([block_bias(hh, qt - 1), causal], axis=0), MOBA_BLOCK)

    @pl.when(qt % blocks_per_tile == 0)
    def _():
        attend(qt, 1, lambda hh: causal, 0)
    for hh in range(n_hb):
        acc = acc_ref[hh]
        o_ref[:, hh * dh:(hh + 1) * dh] = (acc[0:dh, :] / acc[dh:dh + 1, :]).T.astype(o_ref.dtype)


def _prompt_attention(q_src3, k3, v3, slopes, *, q_col):
    bsz, seq_len, w_att = k3.shape
    n_heads = w_att // HEAD_DIM
    tq = MOBA_BLOCK
    nq = seq_len // tq
    nb = seq_len // MOBA_BLOCK
    hb = PROMPT_HEADS_PER_STEP
    gw = hb * HEAD_DIM
    assert n_heads % hb == 0 and seq_len % KV_TILE == 0 and q_col % gw == 0
    qb0 = q_col // gw
    return pl.pallas_call(
        _prompt_attn_kernel,
        grid=(bsz, n_heads // hb, nq),
        in_specs=[
            pl.BlockSpec(memory_space=pltpu.SMEM),
            pl.BlockSpec((1, tq, gw), lambda b, h, i: (b, i, qb0 + h)),
            pl.BlockSpec((1, seq_len, gw), lambda b, h, i: (b, 0, h)),
            pl.BlockSpec((1, seq_len, gw), lambda b, h, i: (b, 0, h)),
        ],
        out_specs=pl.BlockSpec((tq, gw), lambda b, h, i: (b * nq + i, h)),
        out_shape=jax.ShapeDtypeStruct((bsz * seq_len, n_heads * HEAD_DIM), BF16),
        scratch_shapes=[
            pltpu.VMEM((hb, seq_len, HEAD_DIM), BF16),
            pltpu.VMEM((hb, nb, HEAD_DIM + ONES_ROWS, MOBA_BLOCK), BF16),
            pltpu.VMEM((hb, nb, HEAD_DIM), F32),
            pltpu.VMEM((hb, KV_TILE, tq), F32),
            pltpu.VMEM((hb, 1, tq), F32),
            pltpu.VMEM((hb, HEAD_DIM + ONES_ROWS, tq), F32),
        ],
        compiler_params=_params("parallel", "parallel", "arbitrary"),
        name="prompt_attn",
    )(slopes, q_src3, k3, v3)


def _heads_to_rows(x, n_heads, dh):
    return jnp.concatenate([x[:, h * dh:(h + 1) * dh] for h in range(n_heads)], axis=0)


def _sample_attn_kernel(pt_ref, slopes_ref, q_ref, kn_ref, vn_ref, *rest, past_len, n_pages):
    del pt_ref
    k_refs, v_refs, o_ref = rest[:n_pages], rest[n_pages:2 * n_pages], rest[2 * n_pages]
    t = q_ref.shape[1]
    page, n_heads, dh = k_refs[0].shape
    pages_per_block = MOBA_BLOCK // page
    nb = n_pages // pages_per_block
    rows = n_heads * t
    page_keys = page * n_heads

    q2 = _heads_to_rows(q_ref[0], n_heads, dh)
    q2b = (q2 * dh ** -0.5).astype(BF16)
    row_head = lax.broadcasted_iota(jnp.int32, (rows, 1), 0) // t
    row_query = lax.broadcasted_iota(jnp.int32, (rows, 1), 0) % t
    slope_rows = jnp.zeros((rows, 1), F32)
    for h in range(n_heads):
        slope_rows = jnp.where(row_head == h, slopes_ref[h], slope_rows)
    lane = lax.broadcasted_iota(jnp.int32, (rows, page_keys), 1)
    page_bias = jnp.where(lane % n_heads == row_head,
                          slope_rows * (row_query - lane // n_heads).astype(F32), -NEG_INF)

    def block_scores(b):
        ksum = jnp.zeros((n_heads, dh), F32)
        s_pages = []
        for pg in range(b * pages_per_block, (b + 1) * pages_per_block):
            kp = k_refs[pg][...]
            ksum = ksum + jnp.sum(kp, axis=0)
            s_pg = _dot_nt(q2b, kp.reshape(page_keys, dh).astype(BF16))
            s_pages.append(s_pg - page_bias - slope_rows * float(past_len - pg * page))
        kmean = ksum / MOBA_BLOCK
        kmean_rows = jnp.concatenate([jnp.broadcast_to(kmean[h:h + 1, :], (t, dh)) for h in range(n_heads)], axis=0)
        return jnp.sum(q2 * kmean_rows, axis=-1, keepdims=True), jnp.concatenate(s_pages, axis=-1)

    route, m_blk, l_blk, acc_blk = [], [], [], []
    ahead = block_scores(0)
    for b in range(nb):
        route_b, s_b = ahead
        if b + 1 < nb:
            ahead = block_scores(b + 1)
        pages = range(b * pages_per_block, (b + 1) * pages_per_block)
        vblk = jnp.concatenate([v_refs[pg][...].reshape(page_keys, dh) for pg in pages], axis=0)
        m_b = jnp.max(s_b, axis=-1, keepdims=True)
        p = jnp.exp(s_b - m_b)
        route.append(route_b)
        m_blk.append(m_b)
        l_blk.append(jnp.sum(p, axis=-1, keepdims=True))
        acc_blk.append(_dot(p.astype(BF16), vblk.astype(BF16)))

    sel = _topk_select_list(route)
    kn2 = _heads_to_rows(kn_ref[0], n_heads, dh)
    vn2 = _heads_to_rows(vn_ref[0], n_heads, dh)
    col = lax.broadcasted_iota(jnp.int32, (rows, rows), 1)
    dist = row_query - col % t
    keep = (col // t == row_head) & (dist >= 0)
    s_own = _dot_nt(q2b, kn2.astype(BF16)) - slope_rows * dist.astype(F32)
    s_own = jnp.where(keep, s_own, NEG_INF)
    m_all = jnp.max(s_own, axis=-1, keepdims=True)
    for b in range(nb):
        m_all = jnp.maximum(m_all, jnp.where(sel[b], m_blk[b], NEG_INF))
    p_own = jnp.where(keep, jnp.exp(s_own - m_all), 0.0)
    den = jnp.sum(p_own, axis=-1, keepdims=True)
    num = _dot(p_own.astype(BF16), vn2.astype(BF16))
    for b in range(nb):
        w_b = jnp.where(sel[b], jnp.exp(m_blk[b] - m_all), 0.0)
        den = den + w_b * l_blk[b]
        num = num + w_b * acc_blk[b]
    out = num / den
    o_ref[...] = jnp.concatenate([out[h * t:(h + 1) * t, :] for h in range(n_heads)], axis=-1)


def _sample_attention(q_src3, k3, v3, cache_k, cache_v, layer, page_table, slopes, *, q_col):
    n_seq, t, _ = k3.shape
    _, _, page, n_heads, dh = cache_k.shape
    width = n_heads * dh
    n_pages = page_table.shape[1]
    assert MOBA_BLOCK % page == 0 and (n_pages * page) % MOBA_BLOCK == 0 and t <= MOBA_BLOCK
    assert q_col % width == 0
    past_len = n_pages * page
    qb0 = q_col // width
    page_specs = [pl.BlockSpec((None, None, page, n_heads, dh), lambda s, pt, pg=pg: (layer, pt[s, pg], 0, 0, 0))
                  for pg in range(n_pages)]
    grid_spec = pltpu.PrefetchScalarGridSpec(
        num_scalar_prefetch=1,
        grid=(n_seq,),
        in_specs=[
            pl.BlockSpec(memory_space=pltpu.SMEM),
            pl.BlockSpec((1, t, width), lambda s, pt: (s, 0, qb0)),
            pl.BlockSpec((1, t, width), lambda s, pt: (s, 0, 0)),
            pl.BlockSpec((1, t, width), lambda s, pt: (s, 0, 0)),
            *page_specs, *page_specs,
        ],
        out_specs=pl.BlockSpec((t, width), lambda s, pt: (s, 0)),
    )
    return pl.pallas_call(
        functools.partial(_sample_attn_kernel, past_len=past_len, n_pages=n_pages),
        grid_spec=grid_spec,
        out_shape=jax.ShapeDtypeStruct((n_seq * t, width), F32),
        compiler_params=_params("parallel"),
        name="sample_attn",
    )(page_table, slopes, q_src3, k3, v3, *([cache_k] * n_pages), *([cache_v] * n_pages))


def _merge_kernel(a_ref, b_ref, wp_ref, wa_ref, ga_ref, gb_ref, o_ref):
    pa = _dot(a_ref[...].astype(BF16), wp_ref[...])
    pb = _dot(b_ref[...].astype(BF16), wa_ref[...])
    merged = jax.nn.sigmoid(ga_ref[...]) * pa + jax.nn.sigmoid(gb_ref[...]) * pb
    o_ref[...] = merged.astype(o_ref.dtype)


def _merge(a_out, b_out, w_pool_up, w_att_up, proj, *, ga_col, gb_col, tm, tn):
    rows, wa = a_out.shape
    wb = b_out.shape[1]
    n = w_pool_up.shape[1]
    ga0, gb0 = ga_col // tn, gb_col // tn
    return pl.pallas_call(
        _merge_kernel,
        grid=(rows // tm, n // tn),
        in_specs=[
            pl.BlockSpec((tm, wa), lambda i, j: (i, 0)),
            pl.BlockSpec((tm, wb), lambda i, j: (i, 0)),
            pl.BlockSpec((wa, tn), lambda i, j: (0, j)),
            pl.BlockSpec((wb, tn), lambda i, j: (0, j)),
            pl.BlockSpec((tm, tn), lambda i, j: (i, ga0 + j)),
            pl.BlockSpec((tm, tn), lambda i, j: (i, gb0 + j)),
        ],
        out_specs=pl.BlockSpec((tm, tn), lambda i, j: (i, j)),
        out_shape=jax.ShapeDtypeStruct((rows, n), BF16),
        compiler_params=_params("parallel", "arbitrary"),
        name="merge",
    )(a_out, b_out, w_pool_up, w_att_up, proj, proj)


def _matmul_residual_kernel(a_ref, w_ref, r_ref, o_ref, *w_outs):
    wb = w_ref[...].astype(BF16)
    if w_outs:
        w_outs[0][...] = wb
    o_ref[...] = r_ref[...] + _dot(a_ref[...], wb)


def _matmul_residual(a, w, res, *, tm, tn):
    rows, k = a.shape
    n = w.shape[1]
    export_w = w.dtype != BF16
    out_specs = [pl.BlockSpec((tm, tn), lambda i, j: (i, j))]
    out_shape = [jax.ShapeDtypeStruct((rows, n), F32)]
    if export_w:
        out_specs.append(pl.BlockSpec((k, tn), lambda i, j: (0, j)))
        out_shape.append(jax.ShapeDtypeStruct((k, n), BF16))
    outs = pl.pallas_call(
        _matmul_residual_kernel,
        grid=(rows // tm, n // tn),
        in_specs=[
            pl.BlockSpec((tm, k), lambda i, j: (i, 0)),
            pl.BlockSpec((k, tn), lambda i, j: (0, j)),
            pl.BlockSpec((tm, tn), lambda i, j: (i, j)),
        ],
        out_specs=out_specs,
        out_shape=out_shape,
        compiler_params=_params("arbitrary", "arbitrary"),
        name="out_proj",
    )(a, w, res)
    return outs[0], (outs[1] if export_w else None)


FFN_CHUNK_COLS = 256


def _gelu_exact(x):
    return 0.5 * x * (1.0 + lax.erf(x * (2.0 ** -0.5)))


def _ffn_kernel(h_ref, g2_ref, wa_ref, wg_ref, cw_ref, cb_ref, wd_ref, gf_ref, *rest,
                seq_rows, tiles_per_seq, per_seq_hist, export_w):
    hist_ref = rest[0] if per_seq_hist else None
    n_in = 1 if per_seq_hist else 0
    y_ref, tail_ref = rest[n_in:n_in + 2]
    wa_out, wg_out, wd_out = rest[n_in + 2:n_in + 5] if export_w else (None, None, None)
    hn_ref, carry_ref = rest[-2:]
    i = pl.program_id(0)
    j = pl.program_id(1)
    tm = h_ref.shape[0]
    tn = wa_ref.shape[1]
    sub = FFN_CHUNK_COLS
    n_chunks = tn // sub

    @pl.when(j == 0)
    def _():
        h = h_ref[...]
        hn_ref[...] = _rmsnorm(h, g2_ref[...]).astype(BF16)
        y_ref[...] = h

    if not per_seq_hist:
        @pl.when(i % tiles_per_seq == 0)
        def _():
            carry_ref[j] = jnp.zeros((SUBLANES, tn), F32)

    hn = hn_ref[...]
    chunks = [slice(c * sub, (c + 1) * sub) for c in range(n_chunks)]
    wa_parts = [wa_ref[:, cols].astype(BF16) for cols in chunks]
    wg_parts = [wg_ref[:, cols].astype(BF16) for cols in chunks]
    wd_parts = [wd_ref[cols, :].astype(BF16) for cols in chunks]
    if export_w:
        for cols, wa_c, wg_c, wd_c in zip(chunks, wa_parts, wg_parts, wd_parts):
            wa_out[:, cols] = wa_c
            wg_out[:, cols] = wg_c
            wd_out[cols, :] = wd_c
    a_parts = [_dot(hn, wa_c) for wa_c in wa_parts]
    g_parts = [_dot(hn, wg_c) for wg_c in wg_parts]
    row = lax.broadcasted_iota(jnp.int32, (tm, sub), 0)
    down = None
    for cols, a, gv, wd_c in zip(chunks, a_parts, g_parts, wd_parts):
        if per_seq_hist:
            n_seq = tm // seq_rows
            hist = hist_ref[:, :, cols]
            prev = [jnp.broadcast_to(hist[:, r:r + 1, :], (n_seq, seq_rows, sub)).reshape(tm, sub)
                    for r in range(CONV_W - 1)]
            t_in_seq = row % seq_rows
            tail_ref[:, :, cols] = a.reshape(n_seq, seq_rows, sub)[:, seq_rows - (CONV_W - 1):, :]
        else:
            carry = carry_ref[j, :, cols]
            first = SUBLANES - (CONV_W - 1)
            prev = [jnp.broadcast_to(carry[first + r:first + r + 1, :], (tm, sub)) for r in range(CONV_W - 1)]
            t_in_seq = row
            tail = a[tm - SUBLANES:, :]
            carry_ref[j, :, cols] = tail
            tail_ref[0, :, cols] = tail
        ac = cb_ref[:, cols] + cw_ref[CONV_W - 1:CONV_W, cols] * a
        for back in range(1, CONV_W):
            shifted = pltpu.roll(a, back, axis=0)
            for r in range(back):
                shifted = jnp.where(t_in_seq == r, prev[CONV_W - 1 - back + r], shifted)
            ac = ac + cw_ref[CONV_W - 1 - back:CONV_W - back, cols] * shifted
        act = (_gelu_exact(ac) * gv).astype(BF16)
        part = _dot(act, wd_c)
        down = part if down is None else down + part
    y_ref[...] += down

    @pl.when(j == pl.num_programs(1) - 1)
    def _():
        y_ref[...] = _rmsnorm(y_ref[...], gf_ref[...])


def _ffn(h, g2, w_up, conv_w, conv_b, w_down, gf, hist, *, seq_rows, tm, tn):
    rows, d = h.shape
    d_ff = w_down.shape[0]
    n_col = d_ff // tn
    n_row = rows // tm
    per_seq_hist = hist is not None
    export_w = not isinstance(w_up, tuple)
    if export_w:
        w_a, w_g, g_off = w_up, w_up, n_col
        assert w_up.dtype == F32 and w_down.dtype == F32
    else:
        (w_a, w_g), g_off = w_up, 0
    if per_seq_hist:
        assert tm % seq_rows == 0 and seq_rows >= CONV_W - 1
        tiles_per_seq = 1
        tail_shape = jax.ShapeDtypeStruct((rows // seq_rows, CONV_W - 1, d_ff), F32)
        tail_spec = pl.BlockSpec((tm // seq_rows, CONV_W - 1, tn), lambda i, j: (i, 0, j))
        hist_args = (hist,)
        hist_specs = [pl.BlockSpec((tm // seq_rows, CONV_W - 1, tn), lambda i, j: (i, 0, j))]
    else:
        assert seq_rows % tm == 0
        tiles_per_seq = seq_rows // tm
        tail_shape = jax.ShapeDtypeStruct((n_row, SUBLANES, d_ff), F32)
        tail_spec = pl.BlockSpec((1, SUBLANES, tn), lambda i, j: (i, 0, j))
        hist_args = ()
        hist_specs = []
    out_specs = [pl.BlockSpec((tm, d), lambda i, j: (i, 0)), tail_spec]
    out_shape = [jax.ShapeDtypeStruct((rows, d), F32), tail_shape]
    if export_w:
        out_specs += [pl.BlockSpec((d, tn), lambda i, j: (0, j)), pl.BlockSpec((d, tn), lambda i, j: (0, j)),
                      pl.BlockSpec((tn, d), lambda i, j: (j, 0))]
        out_shape += [jax.ShapeDtypeStruct((d, d_ff), BF16), jax.ShapeDtypeStruct((d, d_ff), BF16),
                      jax.ShapeDtypeStruct((d_ff, d), BF16)]
    outs = pl.pallas_call(
        functools.partial(_ffn_kernel, seq_rows=seq_rows, tiles_per_seq=tiles_per_seq, per_seq_hist=per_seq_hist,
                          export_w=export_w),
        grid=(n_row, n_col),
        in_specs=[
            pl.BlockSpec((tm, d), lambda i, j: (i, 0), pipeline_mode=pl.Buffered(1 if export_w else 2)),
            pl.BlockSpec((1, d), lambda i, j: (0, 0)),
            pl.BlockSpec((d, tn), lambda i, j: (0, j)),
            pl.BlockSpec((d, tn), lambda i, j: (0, g_off + j)),
            pl.BlockSpec((CONV_W, tn), lambda i, j: (0, j)),
            pl.BlockSpec((1, tn), lambda i, j: (0, j)),
            pl.BlockSpec((tn, d), lambda i, j: (j, 0)),
            pl.BlockSpec((1, d), lambda i, j: (0, 0)),
            *hist_specs,
        ],
        out_specs=out_specs,
        out_shape=out_shape,
        scratch_shapes=[
            pltpu.VMEM((tm, d), BF16),
            pltpu.VMEM((n_col, SUBLANES, tn), F32),
        ],
        compiler_params=_params("arbitrary", "arbitrary"),
        name="ffn",
    )(h, g2, w_a, w_g, conv_w, conv_b, w_down, gf, *hist_args)
    return outs[0], outs[1], (tuple(outs[2:]) if export_w else None)


def _layer(x3, pool_hist, conv_hist, attend, pos0, w, *, tm, in_tn, ffn_tn, mid_tile):
    n_seq, seq_len, d = x3.shape
    rows = n_seq * seq_len
    x = x3.reshape(rows, d)
    w_pool = w["pool_scale"].shape[1]
    w_att = w["w_att_up"].shape[0]
    rest, k, v, w_in_b = _in_proj(x, w["g1"], w["w_in"], k_col=w_pool + w_att, kv_width=w_att, tm=tm, tn=in_tn)
    rest3 = rest.reshape(n_seq, seq_len, -1)
    k3, v3 = k.reshape(n_seq, seq_len, w_att), v.reshape(n_seq, seq_len, w_att)
    q_col, ga_col, gb_col = w_pool, w_pool + w_att, w_pool + w_att + d
    if pool_hist is None:
        a_out = _pool_mix(rest3, None, w["pool_maps"], w["pool_scale"], s_blk=1, l_blk=POOL_TILE_ROWS, pos0=pos0,
                          out_dtype=BF16)
    else:
        a_out = _pool_mix(rest3, pool_hist, w["pool_maps"], w["pool_scale"], s_blk=POOL_SEQS_PER_STEP, l_blk=seq_len,
                          pos0=pos0, out_dtype=F32)
    b_out = attend(rest3, k3, v3, q_col)
    mid_tm, mid_tn = mid_tile
    merged = _merge(a_out, b_out, w["w_pool_up"], w["w_att_up"], rest, ga_col=ga_col, gb_col=gb_col,
                    tm=mid_tm, tn=mid_tn)
    h, w_out_b = _matmul_residual(merged, w["w_out"], x, tm=mid_tm, tn=mid_tn)
    y, tail, ffn_w = _ffn(h, w["g2"], w["w_up"], w["conv_w"], w["conv_b"], w["w_down"], w["gf"], conv_hist,
                          seq_rows=seq_len, tm=tm, tn=ffn_tn)
    return y.reshape(n_seq, seq_len, d), rest3, k3, v3, tail, (w_in_b, w_out_b, ffn_w)


def kernel(x_prompt, x_sample, cache_k, cache_v, state_pool, state_conv, page_table, norm1_g, w_in, pool_maps,
           pool_scale, w_pool_up, w_att_up, w_out, norm2_g, w_up, conv_w, conv_b, w_down, final_norm_g):
    depth = w_in.shape[0]
    assert depth == 1, "single-layer trunk"
    bsz, seq, d = x_prompt.shape
    n_dec, dec_seq, _ = x_sample.shape
    _, n_pool, page, n_heads, dh = cache_k.shape
    assert dh == HEAD_DIM
    w_att = n_heads * dh
    w_pool = pool_scale.shape[1]
    d_ff = w_down.shape[1]
    past_len = page_table.shape[1] * page
    slopes = 2.0 ** (-8.0 * jnp.arange(1, n_heads + 1, dtype=F32) / n_heads)

    l = 0
    w = dict(
        g1=norm1_g[l][None, :], w_in=w_in[l], pool_maps=pool_maps[l].astype(BF16),
        pool_scale=pool_scale[l][None, :], w_pool_up=w_pool_up[l].astype(BF16), w_att_up=w_att_up[l].astype(BF16),
        w_out=w_out[l], g2=norm2_g[l][None, :], w_up=w_up[l], conv_w=conv_w[l],
        conv_b=conv_b[l][None, :], w_down=w_down[l], gf=final_norm_g[None, :],
    )
    tm = ROW_TILE

    attend_s = lambda q_src3, k3, v3, qc: _sample_attention(q_src3, k3, v3, cache_k, cache_v, l, page_table, slopes,
                                                            q_col=qc)
    y_s, rest_s, k_s, v_s, tail_s, (w_in_b, w_out_b, (w_ua_b, w_ug_b, w_down_b)) = _layer(
        x_sample, state_pool[l], state_conv[l], attend_s, past_len, w, tm=tm, in_tn=IN_PROJ_COLS_F32,
        ffn_tn=FFN_COLS_F32, mid_tile=(tm, OUT_PROJ_COLS_F32))

    w_p = dict(w, w_in=w_in_b, w_out=w_out_b, w_up=(w_ua_b, w_ug_b), w_down=w_down_b)
    attend_p = lambda q_src3, k3, v3, qc: _prompt_attention(q_src3, k3, v3, slopes, q_col=qc)
    y_p, rest_p, k_p, v_p, tail_p, _ = _layer(x_prompt, None, None, attend_p, 0, w_p, tm=tm, in_tn=IN_PROJ_COLS_BF16,
                                              ffn_tn=FFN_COLS_BF16, mid_tile=(tm // 2, d))

    heads = lambda a3: a3.reshape(a3.shape[0], a3.shape[1], n_heads, dh)[None]
    pool_p = rest_p[:, seq - POOL_HIST:, :w_pool][None]
    pool_s = jnp.concatenate([state_pool[l], rest_s[:, :, :w_pool]], axis=1)[:, -POOL_HIST:][None]
    tiles_per_seq = seq // tm
    conv_p = tail_p.reshape(bsz, tiles_per_seq, SUBLANES, d_ff)[:, -1, SUBLANES - (CONV_W - 1):][None]
    conv_s = tail_s[None]
    return (y_p, y_s, heads(k_p), heads(v_p), heads(k_s), heads(v_s), pool_p, pool_s, conv_p, conv_s)
```

```python
import functools

import jax
import jax.numpy as jnp
from jax import lax
from jax.experimental import pallas as pl
from jax.experimental.pallas import tpu as pltpu

POOL_WINDOWS = (2, 4, 8, 16)
assert all(w & (w - 1) == 0 for w in POOL_WINDOWS)
POOL_HIST = max(POOL_WINDOWS) - 1
HEAD_DIM = 128
MOBA_BLOCK = 256
MOBA_TOPK = 3
CONV_W = 3
RMS_EPS = 1e-6
NEG_INF = -1e30

SUBLANES = 8
HALO_ROWS = 16
ROW_TILE = 1024
POOL_TILE_ROWS = 1024
POOL_SEQS_PER_STEP = 64
IN_PROJ_COLS_F32, IN_PROJ_COLS_BF16 = 512, 1024
FFN_COLS_F32, FFN_COLS_BF16 = 256, 512
OUT_PROJ_COLS_F32 = 1024
V7X_VMEM_BYTES = 64 * 1024 * 1024
VMEM_LIMIT_BYTES = V7X_VMEM_BYTES - 6 * 1024 * 1024

BF16 = jnp.bfloat16
F32 = jnp.float32


def _params(*semantics):
    return pltpu.CompilerParams(dimension_semantics=semantics, vmem_limit_bytes=VMEM_LIMIT_BYTES)


def _rmsnorm(x, g):
    return x * lax.rsqrt(jnp.mean(x * x, axis=-1, keepdims=True) + RMS_EPS) * g


def _dot(a, b):
    return jnp.dot(a, b, preferred_element_type=F32)


def _dot_nt(a, b):
    return lax.dot_general(a, b, (((1,), (1,)), ((), ())), preferred_element_type=F32)


def _split_bf16(x):
    hi = x.astype(BF16)
    lo = (x - hi.astype(F32)).astype(BF16)
    return hi, lo


def _dot_nt_3pass(a, b):
    a_hi, a_lo = _split_bf16(a)
    b_hi, b_lo = _split_bf16(b)
    return _dot_nt(a_hi, b_hi) + (_dot_nt(a_hi, b_lo) + _dot_nt(a_lo, b_hi))


def _in_proj_kernel(x_ref, g_ref, w_ref, rest_ref, k_ref, v_ref, *more, k_tiles, v_tiles, export_w):
    xn_ref = more[-1]
    j = pl.program_id(1)

    @pl.when(j == 0)
    def _():
        xn_ref[...] = _rmsnorm(x_ref[...], g_ref[...]).astype(BF16)

    def project_into(o_ref):
        wb = w_ref[...].astype(BF16)
        if export_w:
            more[0][...] = wb
        o_ref[...] = _dot(xn_ref[...], wb)

    is_k = (j >= k_tiles[0]) & (j < k_tiles[1])
    is_v = (j >= v_tiles[0]) & (j < v_tiles[1])
    pl.when(is_k)(lambda: project_into(k_ref))
    pl.when(is_v)(lambda: project_into(v_ref))
    pl.when(jnp.logical_not(is_k | is_v))(lambda: project_into(rest_ref))


def _in_proj(x, g, w, *, k_col, kv_width, tm, tn):
    rows, d = x.shape
    n = w.shape[1]
    export_w = w.dtype != BF16
    assert k_col % tn == 0 and kv_width % tn == 0
    k0, nkv = k_col // tn, kv_width // tn
    v0, after = k0 + nkv, k0 + 2 * nkv
    rest_map = lambda i, j: (i, jnp.where(j < k0, j, jnp.maximum(j - 2 * nkv, k0 - 1)))
    k_map = lambda i, j: (i, jnp.clip(j - k0, 0, nkv - 1))
    v_map = lambda i, j: (i, jnp.clip(j - v0, 0, nkv - 1))
    out_specs = [pl.BlockSpec((tm, tn), rest_map), pl.BlockSpec((tm, tn), k_map), pl.BlockSpec((tm, tn), v_map)]
    out_shape = [jax.ShapeDtypeStruct((rows, n - 2 * kv_width), F32),
                 jax.ShapeDtypeStruct((rows, kv_width), F32), jax.ShapeDtypeStruct((rows, kv_width), F32)]
    if export_w:
        out_specs.append(pl.BlockSpec((d, tn), lambda i, j: (0, j)))
        out_shape.append(jax.ShapeDtypeStruct((d, n), BF16))
    outs = pl.pallas_call(
        functools.partial(_in_proj_kernel, k_tiles=(k0, v0), v_tiles=(v0, after), export_w=export_w),
        grid=(rows // tm, n // tn),
        in_specs=[
            pl.BlockSpec((tm, d), lambda i, j: (i, 0)),
            pl.BlockSpec((1, d), lambda i, j: (0, 0)),
            pl.BlockSpec((d, tn), lambda i, j: (0, j)),
        ],
        out_specs=out_specs,
        out_shape=out_shape,
        scratch_shapes=[pltpu.VMEM((tm, d), BF16)],
        compiler_params=_params("arbitrary", "arbitrary"),
        name="in_proj",
    )(x, g, w)
    return (*outs, None) if not export_w else tuple(outs)


def _pool_kernel(u_ref, hist_ref, maps_ref, scale_ref, o_ref, ext_ref, *, pos0, zero_first_hist):
    s_blk, l_blk, width = u_ref.shape
    group = width // len(POOL_WINDOWS)
    tile = pl.program_id(1)
    u = u_ref[...]
    ext_ref[:, HALO_ROWS:, :] = u
    hist_rows = hist_ref.shape[1]
    ext_ref[:, HALO_ROWS - hist_rows:HALO_ROWS, :] = hist_ref[...]
    if hist_rows < HALO_ROWS:
        ext_ref[:, 0:HALO_ROWS - hist_rows, :] = jnp.zeros((s_blk, HALO_ROWS - hist_rows, width), F32)
    if zero_first_hist:
        @pl.when(tile == 0)
        def _():
            ext_ref[:, 0:HALO_ROWS, :] = jnp.zeros((s_blk, HALO_ROWS, width), F32)

    pos = pos0 + tile * l_blk + lax.broadcasted_iota(jnp.int32, (s_blk, l_blk, group), 1)
    outs = []
    for gi, w in enumerate(POOL_WINDOWS):
        cols = slice(gi * group, (gi + 1) * group)
        acc = ext_ref[:, :, cols]
        span = 1
        while span < w:
            acc = acc + pltpu.roll(acc, span, axis=1)
            span *= 2
        win = acc[:, HALO_ROWS:, :]
        cnt = jnp.minimum(w, pos + 1).astype(F32)
        d = win / cnt - u[:, :, cols]
        d2 = d.reshape(s_blk * l_blk, group).astype(BF16)
        outs.append(_dot(d2, maps_ref[gi]))
    m = jnp.concatenate(outs, axis=-1) * scale_ref[...]
    o_ref[...] = m.astype(o_ref.dtype)


def _pool_mix(proj3, hist, maps, scale, *, s_blk, l_blk, pos0, out_dtype):
    n_seq, seq_len, _ = proj3.shape
    n_groups, group, _ = maps.shape
    width = n_groups * group
    tiles = seq_len // l_blk
    if hist is None:
        halo_blocks = l_blk // HALO_ROWS
        hist_arr = proj3
        hist_spec = pl.BlockSpec((s_blk, HALO_ROWS, width),
                                 lambda s, t: (s, jnp.maximum(t * halo_blocks - 1, 0), 0))
    else:
        assert tiles == 1
        hist_arr = hist
        hist_spec = pl.BlockSpec((s_blk, hist.shape[1], width), lambda s, t: (s, 0, 0))
    kern = functools.partial(_pool_kernel, pos0=pos0, zero_first_hist=hist is None)
    return pl.pallas_call(
        kern,
        grid=(n_seq // s_blk, tiles),
        in_specs=[
            pl.BlockSpec((s_blk, l_blk, width), lambda s, t: (s, t, 0)),
            hist_spec,
            pl.BlockSpec((n_groups, group, group), lambda s, t: (0, 0, 0)),
            pl.BlockSpec((1, width), lambda s, t: (0, 0)),
        ],
        out_specs=pl.BlockSpec((s_blk * l_blk, width), lambda s, t: (s * tiles + t, 0)),
        out_shape=jax.ShapeDtypeStruct((n_seq * seq_len, width), out_dtype),
        scratch_shapes=[pltpu.VMEM((s_blk, HALO_ROWS + l_blk, width), F32)],
        compiler_params=_params("parallel", "arbitrary"),
        name="pool_mix",
    )(proj3, hist_arr, maps, scale)


def _topk_select(route, n_valid):
    nb, n = route.shape
    blk = lax.broadcasted_iota(jnp.int32, (nb, n), 0)
    rank = jnp.zeros((nb, n), jnp.int32)
    for other in range(nb):
        c = route[other:other + 1, :]
        beats = jnp.where((c > route) | ((c == route) & (other < blk)), 1, 0)
        rank = rank + beats * (other < n_valid).astype(jnp.int32)
    return (blk < n_valid) & (rank < MOBA_TOPK)


def _topk_select_list(scores):
    sel = []
    for j, s_j in enumerate(scores):
        rank = jnp.zeros(s_j.shape, jnp.int32)
        for other, s_o in enumerate(scores):
            if other < j:
                rank = rank + jnp.where(s_o >= s_j, 1, 0)
            elif other > j:
                rank = rank + jnp.where(s_o > s_j, 1, 0)
        sel.append(rank < MOBA_TOPK)
    return sel


KV_TILE = 2 * MOBA_BLOCK
assert KV_TILE == 2 * MOBA_BLOCK
ONES_ROWS = 16
M_FLOOR = -1e20
PROMPT_HEADS_PER_STEP = 4
SCORE_LOOKAHEAD = 4


def _prompt_attn_kernel(slopes_ref, q_ref, k_ref, v_ref, o_ref,
                        kb_ref, vt_ref, kmean_ref, alibi_ref, m_ref, acc_ref):
    hgroup = pl.program_id(1)
    qt = pl.program_id(2)
    n_hb, seq_len, dh = kb_ref.shape
    tq = q_ref.shape[1]
    nb = seq_len // MOBA_BLOCK
    blocks_per_tile = KV_TILE // MOBA_BLOCK
    slopes = [slopes_ref[hgroup * n_hb + hh] for hh in range(n_hb)]
    q_minus_k = (lax.broadcasted_iota(jnp.int32, (KV_TILE, tq), 1)
                 - lax.broadcasted_iota(jnp.int32, (KV_TILE, tq), 0))

    @pl.when(qt == 0)
    def _():
        for hh in range(n_hb):
            cols = slice(hh * dh, (hh + 1) * dh)
            k = k_ref[0, :, cols]
            kb_ref[hh] = k.astype(BF16)
            kmean_ref[hh] = jnp.mean(k.reshape(nb, MOBA_BLOCK, dh), axis=1)
            ones_rows = jnp.where(lax.broadcasted_iota(jnp.int32, (ONES_ROWS, MOBA_BLOCK), 0) == 0, 1.0, 0.0)
            for c in range(nb):
                vt_ref[hh, c, 0:dh, :] = v_ref[0, c * MOBA_BLOCK:(c + 1) * MOBA_BLOCK, cols].T.astype(BF16)
                vt_ref[hh, c, dh:, :] = ones_rows.astype(BF16)
            alibi_ref[hh] = slopes[hh] * q_minus_k.astype(F32)

    blk = lax.broadcasted_iota(jnp.int32, (nb, tq), 0)
    qbs, sel_bits = [], []
    for hh in range(n_hb):
        q = q_ref[0, :, hh * dh:(hh + 1) * dh]
        qbs.append((q * dh ** -0.5).astype(BF16))
        route = _dot_nt_3pass(kmean_ref[hh], q)
        sel = _topk_select(route, qt)
        bits = jnp.sum(jnp.where(sel, jnp.left_shift(1, blk).astype(F32), 0.0), axis=0, keepdims=True)
        sel_bits.append(bits.astype(jnp.int32))

    m_ref[...] = jnp.full(m_ref.shape, M_FLOOR, F32)
    acc_ref[...] = jnp.zeros(acc_ref.shape, F32)

    def block_bias(hh, j):
        bit = jnp.right_shift(sel_bits[hh], j) & 1
        return jnp.broadcast_to(jnp.where(bit == 1, 0.0, NEG_INF), (MOBA_BLOCK, tq))

    def attend(first_blk, n_blk, bias_fn, offset):
        keys = n_blk * MOBA_BLOCK
        start = pl.multiple_of(first_blk * MOBA_BLOCK, MOBA_BLOCK)
        score = lambda hh: _dot_nt(kb_ref[hh, pl.ds(start, keys), :], qbs[hh])
        scores = {hh: score(hh) for hh in range(min(SCORE_LOOKAHEAD, n_hb))}
        for hh in range(n_hb):
            if hh + SCORE_LOOKAHEAD < n_hb:
                scores[hh + SCORE_LOOKAHEAD] = score(hh + SCORE_LOOKAHEAD)
            z = scores.pop(hh) - alibi_ref[hh, 0:keys, :] + bias_fn(hh)
            shift = slopes[hh] * jnp.asarray(offset).astype(F32)
            m_old = m_ref[hh]
            m_new = jnp.maximum(m_old, jnp.max(z, axis=0, keepdims=True) - shift)
            alpha = jnp.exp(m_old - m_new)
            pr = jnp.exp(z - (m_new + shift))
            pv = None
            for b in range(n_blk):
                part = _dot(vt_ref[hh, first_blk + b], pr[b * MOBA_BLOCK:(b + 1) * MOBA_BLOCK].astype(BF16))
                pv = part if pv is None else pv + part
            acc_ref[hh] = alpha * acc_ref[hh] + pv
            m_ref[hh] = m_new

    def past_pair(p, carry):
        first = p * blocks_per_tile
        attend(first, blocks_per_tile,
               lambda hh: jnp.concatenate([block_bias(hh, first + b) for b in range(blocks_per_tile)], axis=0),
               (qt - first) * MOBA_BLOCK)
        return carry

    lax.fori_loop(0, qt // blocks_per_tile, past_pair, 0)

    causal = jnp.where(q_minus_k[0:MOBA_BLOCK, :] >= 0, 0.0, NEG_INF)

    @pl.when(qt % blocks_per_tile == 1)
    def _():
        attend(qt - 1, 2, lambda hh: jnp.concatenate([block_bias(hh, qt - 1), causal], axis=0), MOBA_BLOCK)

    @pl.when(qt % blocks_per_tile == 0)
    def _():
        attend(qt, 1, lambda hh: causal, 0)
    for hh in range(n_hb):
        acc = acc_ref[hh]
        o_ref[:, hh * dh:(hh + 1) * dh] = (acc[0:dh, :] / acc[dh:dh + 1, :]).T.astype(o_ref.dtype)


def _prompt_attention(q_src3, k3, v3, slopes, *, q_col):
    bsz, seq_len, w_att = k3.shape
    n_heads = w_att // HEAD_DIM
    tq = MOBA_BLOCK
    nq = seq_len // tq
    nb = seq_len // MOBA_BLOCK
    hb = PROMPT_HEADS_PER_STEP
    gw = hb * HEAD_DIM
    assert n_heads % hb == 0 and seq_len % KV_TILE == 0 and q_col % gw == 0
    qb0 = q_col // gw
    return pl.pallas_call(
        _prompt_attn_kernel,
        grid=(bsz, n_heads // hb, nq),
        in_specs=[
            pl.BlockSpec(memory_space=pltpu.SMEM),
            pl.BlockSpec((1, tq, gw), lambda b, h, i: (b, i, qb0 + h)),
            pl.BlockSpec((1, seq_len, gw), lambda b, h, i: (b, 0, h)),
            pl.BlockSpec((1, seq_len, gw), lambda b, h, i: (b, 0, h)),
        ],
        out_specs=pl.BlockSpec((tq, gw), lambda b, h, i: (b * nq + i, h)),
        out_shape=jax.ShapeDtypeStruct((bsz * seq_len, n_heads * HEAD_DIM), BF16),
        scratch_shapes=[
            pltpu.VMEM((hb, seq_len, HEAD_DIM), BF16),
            pltpu.VMEM((hb, nb, HEAD_DIM + ONES_ROWS, MOBA_BLOCK), BF16),
            pltpu.VMEM((hb, nb, HEAD_DIM), F32),
            pltpu.VMEM((hb, KV_TILE, tq), F32),
            pltpu.VMEM((hb, 1, tq), F32),
            pltpu.VMEM((hb, HEAD_DIM + ONES_ROWS, tq), F32),
        ],
        compiler_params=_params("parallel", "parallel", "arbitrary"),
        name="prompt_attn",
    )(slopes, q_src3, k3, v3)


def _heads_to_rows(x, n_heads, dh):
    return jnp.concatenate([x[:, h * dh:(h + 1) * dh] for h in range(n_heads)], axis=0)


CACHE_RING = 3


def _sample_attn_kernel(pt_ref, slopes_ref, q_ref, kn_ref, vn_ref, ck_hbm, cv_hbm, o_ref, kbuf, vbuf, sem, *,
                        past_len, n_pages, layer):
    seq = pl.program_id(0)
    n_seq = pl.num_programs(0)

    def page_copies(s):
        slot = s % CACHE_RING
        copies = []
        for pg in range(n_pages):
            src = pt_ref[s, pg]
            copies.append(pltpu.make_async_copy(ck_hbm.at[layer, src], kbuf.at[slot, pg], sem.at[slot]))
            copies.append(pltpu.make_async_copy(cv_hbm.at[layer, src], vbuf.at[slot, pg], sem.at[slot]))
        return copies

    def start(s):
        for n, c in enumerate(page_copies(s)):
            c.start(priority=n % 2)

    @pl.when(seq == 0)
    def _():
        for s in range(CACHE_RING - 1):
            start(s)

    @pl.when(seq + CACHE_RING - 1 < n_seq)
    def _():
        start(seq + CACHE_RING - 1)

    for c in page_copies(seq):
        c.wait()
    slot = seq % CACHE_RING
    k_refs = [kbuf.at[slot, pg] for pg in range(n_pages)]
    v_refs = [vbuf.at[slot, pg] for pg in range(n_pages)]
    t = q_ref.shape[1]
    page, n_heads, dh = k_refs[0].shape
    pages_per_block = MOBA_BLOCK // page
    nb = n_pages // pages_per_block
    rows = n_heads * t
    page_keys = page * n_heads

    q2 = _heads_to_rows(q_ref[0], n_heads, dh)
    q2b = (q2 * dh ** -0.5).astype(BF16)
    row_head = lax.broadcasted_iota(jnp.int32, (rows, 1), 0) // t
    row_query = lax.broadcasted_iota(jnp.int32, (rows, 1), 0) % t
    slope_rows = jnp.zeros((rows, 1), F32)
    for h in range(n_heads):
        slope_rows = jnp.where(row_head == h, slopes_ref[h], slope_rows)
    lane = lax.broadcasted_iota(jnp.int32, (rows, page_keys), 1)
    page_bias = jnp.where(lane % n_heads == row_head,
                          slope_rows * (row_query - lane // n_heads).astype(F32), -NEG_INF)

    def block_scores(b):
        ksum = jnp.zeros((n_heads, dh), F32)
        s_pages = []
        for pg in range(b * pages_per_block, (b + 1) * pages_per_block):
            kp = k_refs[pg][...]
            ksum = ksum + jnp.sum(kp, axis=0)
            s_pg = _dot_nt(q2b, kp.reshape(page_keys, dh).astype(BF16))
            s_pages.append(s_pg - page_bias - slope_rows * float(past_len - pg * page))
        kmean = ksum / MOBA_BLOCK
        kmean_rows = jnp.concatenate([jnp.broadcast_to(kmean[h:h + 1, :], (t, dh)) for h in range(n_heads)], axis=0)
        return jnp.sum(q2 * kmean_rows, axis=-1, keepdims=True), jnp.concatenate(s_pages, axis=-1)

    route, m_blk, l_blk, acc_blk = [], [], [], []
    ahead = block_scores(0)
    for b in range(nb):
        route_b, s_b = ahead
        if b + 1 < nb:
            ahead = block_scores(b + 1)
        pages = range(b * pages_per_block, (b + 1) * pages_per_block)
        vblk = jnp.concatenate([v_refs[pg][...].reshape(page_keys, dh) for pg in pages], axis=0)
        m_b = jnp.max(s_b, axis=-1, keepdims=True)
        p = jnp.exp(s_b - m_b)
        route.append(route_b)
        m_blk.append(m_b)
        l_blk.append(jnp.sum(p, axis=-1, keepdims=True))
        acc_blk.append(_dot(p.astype(BF16), vblk.astype(BF16)))

    sel = _topk_select_list(route)
    kn2 = _heads_to_rows(kn_ref[0], n_heads, dh)
    vn2 = _heads_to_rows(vn_ref[0], n_heads, dh)
    col = lax.broadcasted_iota(jnp.int32, (rows, rows), 1)
    dist = row_query - col % t
    keep = (col // t == row_head) & (dist >= 0)
    s_own = _dot_nt(q2b, kn2.astype(BF16)) - slope_rows * dist.astype(F32)
    s_own = jnp.where(keep, s_own, NEG_INF)
    m_all = jnp.max(s_own, axis=-1, keepdims=True)
    for b in range(nb):
        m_all = jnp.maximum(m_all, jnp.where(sel[b], m_blk[b], NEG_INF))
    p_own = jnp.where(keep, jnp.exp(s_own - m_all), 0.0)
    den = jnp.sum(p_own, axis=-1, keepdims=True)
    num = _dot(p_own.astype(BF16), vn2.astype(BF16))
    for b in range(nb):
        w_b = jnp.where(sel[b], jnp.exp(m_blk[b] - m_all), 0.0)
        den = den + w_b * l_blk[b]
        num = num + w_b * acc_blk[b]
    out = num / den
    o_ref[...] = jnp.concatenate([out[h * t:(h + 1) * t, :] for h in range(n_heads)], axis=-1)


def _sample_attention(q_src3, k3, v3, cache_k, cache_v, layer, page_table, slopes, *, q_col):
    n_seq, t, _ = k3.shape
    _, _, page, n_heads, dh = cache_k.shape
    width = n_heads * dh
    n_pages = page_table.shape[1]
    assert MOBA_BLOCK % page == 0 and (n_pages * page) % MOBA_BLOCK == 0 and t <= MOBA_BLOCK
    assert q_col % width == 0
    assert n_seq >= CACHE_RING - 1
    past_len = n_pages * page
    qb0 = q_col // width
    grid_spec = pltpu.PrefetchScalarGridSpec(
        num_scalar_prefetch=1,
        grid=(n_seq,),
        in_specs=[
            pl.BlockSpec(memory_space=pltpu.SMEM),
            pl.BlockSpec((1, t, width), lambda s, pt: (s, 0, qb0)),
            pl.BlockSpec((1, t, width), lambda s, pt: (s, 0, 0)),
            pl.BlockSpec((1, t, width), lambda s, pt: (s, 0, 0)),
            pl.BlockSpec(memory_space=pl.ANY),
            pl.BlockSpec(memory_space=pl.ANY),
        ],
        out_specs=pl.BlockSpec((t, width), lambda s, pt: (s, 0)),
        scratch_shapes=[
            pltpu.VMEM((CACHE_RING, n_pages, page, n_heads, dh), F32),
            pltpu.VMEM((CACHE_RING, n_pages, page, n_heads, dh), F32),
            pltpu.SemaphoreType.DMA((CACHE_RING,)),
        ],
    )
    return pl.pallas_call(
        functools.partial(_sample_attn_kernel, past_len=past_len, n_pages=n_pages, layer=layer),
        grid_spec=grid_spec,
        out_shape=jax.ShapeDtypeStruct((n_seq * t, width), F32),
        compiler_params=_params("arbitrary"),
        name="sample_attn",
    )(page_table, slopes, q_src3, k3, v3, cache_k, cache_v)


def _merge_kernel(a_ref, b_ref, wp_ref, wa_ref, ga_ref, gb_ref, o_ref):
    pa = _dot(a_ref[...].astype(BF16), wp_ref[...])
    pb = _dot(b_ref[...].astype(BF16), wa_ref[...])
    merged = jax.nn.sigmoid(ga_ref[...]) * pa + jax.nn.sigmoid(gb_ref[...]) * pb
    o_ref[...] = merged.astype(o_ref.dtype)


def _merge(a_out, b_out, w_pool_up, w_att_up, proj, *, ga_col, gb_col, tm, tn):
    rows, wa = a_out.shape
    wb = b_out.shape[1]
    n = w_pool_up.shape[1]
    ga0, gb0 = ga_col // tn, gb_col // tn
    return pl.pallas_call(
        _merge_kernel,
        grid=(rows // tm, n // tn),
        in_specs=[
            pl.BlockSpec((tm, wa), lambda i, j: (i, 0)),
            pl.BlockSpec((tm, wb), lambda i, j: (i, 0)),
            pl.BlockSpec((wa, tn), lambda i, j: (0, j)),
            pl.BlockSpec((wb, tn), lambda i, j: (0, j)),
            pl.BlockSpec((tm, tn), lambda i, j: (i, ga0 + j)),
            pl.BlockSpec((tm, tn), lambda i, j: (i, gb0 + j)),
        ],
        out_specs=pl.BlockSpec((tm, tn), lambda i, j: (i, j)),
        out_shape=jax.ShapeDtypeStruct((rows, n), BF16),
        compiler_params=_params("parallel", "arbitrary"),
        name="merge",
    )(a_out, b_out, w_pool_up, w_att_up, proj, proj)


def _matmul_residual_kernel(a_ref, w_ref, r_ref, o_ref, *w_outs):
    wb = w_ref[...].astype(BF16)
    if w_outs:
        w_outs[0][...] = wb
    o_ref[...] = r_ref[...] + _dot(a_ref[...], wb)


def _matmul_residual(a, w, res, *, tm, tn):
    rows, k = a.shape
    n = w.shape[1]
    export_w = w.dtype != BF16
    out_specs = [pl.BlockSpec((tm, tn), lambda i, j: (i, j))]
    out_shape = [jax.ShapeDtypeStruct((rows, n), F32)]
    if export_w:
        out_specs.append(pl.BlockSpec((k, tn), lambda i, j: (0, j)))
        out_shape.append(jax.ShapeDtypeStruct((k, n), BF16))
    outs = pl.pallas_call(
        _matmul_residual_kernel,
        grid=(rows // tm, n // tn),
        in_specs=[
            pl.BlockSpec((tm, k), lambda i, j: (i, 0)),
            pl.BlockSpec((k, tn), lambda i, j: (0, j)),
            pl.BlockSpec((tm, tn), lambda i, j: (i, j)),
        ],
        out_specs=out_specs,
        out_shape=out_shape,
        compiler_params=_params("arbitrary", "arbitrary"),
        name="out_proj",
    )(a, w, res)
    return outs[0], (outs[1] if export_w else None)


FFN_CHUNK_COLS = 256


def _gelu_exact(x):
    return 0.5 * x * (1.0 + lax.erf(x * (2.0 ** -0.5)))


def _ffn_kernel(h_ref, g2_ref, wa_ref, wg_ref, cw_ref, cb_ref, wd_ref, gf_ref, *rest,
                seq_rows, tiles_per_seq, per_seq_hist, export_w):
    hist_ref = rest[0] if per_seq_hist else None
    n_in = 1 if per_seq_hist else 0
    y_ref, tail_ref = rest[n_in:n_in + 2]
    wa_out, wg_out, wd_out = rest[n_in + 2:n_in + 5] if export_w else (None, None, None)
    hn_ref, carry_ref = rest[-2:]
    i = pl.program_id(0)
    j = pl.program_id(1)
    tm = h_ref.shape[0]
    tn = wa_ref.shape[1]
    sub = FFN_CHUNK_COLS
    n_chunks = tn // sub

    @pl.when(j == 0)
    def _():
        h = h_ref[...]
        hn_ref[...] = _rmsnorm(h, g2_ref[...]).astype(BF16)
        y_ref[...] = h

    if not per_seq_hist:
        @pl.when(i % tiles_per_seq == 0)
        def _():
            carry_ref[j] = jnp.zeros((SUBLANES, tn), F32)

    hn = hn_ref[...]
    chunks = [slice(c * sub, (c + 1) * sub) for c in range(n_chunks)]
    wa_parts = [wa_ref[:, cols].astype(BF16) for cols in chunks]
    wg_parts = [wg_ref[:, cols].astype(BF16) for cols in chunks]
    wd_parts = [wd_ref[cols, :].astype(BF16) for cols in chunks]
    if export_w:
        for cols, wa_c, wg_c, wd_c in zip(chunks, wa_parts, wg_parts, wd_parts):
            wa_out[:, cols] = wa_c
            wg_out[:, cols] = wg_c
            wd_out[cols, :] = wd_c
    a_parts = [_dot(hn, wa_c) for wa_c in wa_parts]
    g_parts = [_dot(hn, wg_c) for wg_c in wg_parts]
    row = lax.broadcasted_iota(jnp.int32, (tm, sub), 0)
    down = None
    for cols, a, gv, wd_c in zip(chunks, a_parts, g_parts, wd_parts):
        if per_seq_hist:
            n_seq = tm // seq_rows
            hist = hist_ref[:, :, cols]
            prev = [jnp.broadcast_to(hist[:, r:r + 1, :], (n_seq, seq_rows, sub)).reshape(tm, sub)
                    for r in range(CONV_W - 1)]
            t_in_seq = row % seq_rows
            tail_ref[:, :, cols] = a.reshape(n_seq, seq_rows, sub)[:, seq_rows - (CONV_W - 1):, :]
        else:
            carry = carry_ref[j, :, cols]
            first = SUBLANES - (CONV_W - 1)
            prev = [jnp.broadcast_to(carry[first + r:first + r + 1, :], (tm, sub)) for r in range(CONV_W - 1)]
            t_in_seq = row
            tail = a[tm - SUBLANES:, :]
            carry_ref[j, :, cols] = tail
            tail_ref[0, :, cols] = tail
        ac = cb_ref[:, cols] + cw_ref[CONV_W - 1:CONV_W, cols] * a
        for back in range(1, CONV_W):
            shifted = pltpu.roll(a, back, axis=0)
            for r in range(back):
                shifted = jnp.where(t_in_seq == r, prev[CONV_W - 1 - back + r], shifted)
            ac = ac + cw_ref[CONV_W - 1 - back:CONV_W - back, cols] * shifted
        act = (_gelu_exact(ac) * gv).astype(BF16)
        part = _dot(act, wd_c)
        down = part if down is None else down + part
    y_ref[...] += down

    @pl.when(j == pl.num_programs(1) - 1)
    def _():
        y_ref[...] = _rmsnorm(y_ref[...], gf_ref[...])


def _ffn(h, g2, w_up, conv_w, conv_b, w_down, gf, hist, *, seq_rows, tm, tn):
    rows, d = h.shape
    d_ff = w_down.shape[0]
    n_col = d_ff // tn
    n_row = rows // tm
    per_seq_hist = hist is not None
    export_w = not isinstance(w_up, tuple)
    if export_w:
        w_a, w_g, g_off = w_up, w_up, n_col
        assert w_up.dtype == F32 and w_down.dtype == F32
    else:
        (w_a, w_g), g_off = w_up, 0
    if per_seq_hist:
        assert tm % seq_rows == 0 and seq_rows >= CONV_W - 1
        tiles_per_seq = 1
        tail_shape = jax.ShapeDtypeStruct((rows // seq_rows, CONV_W - 1, d_ff), F32)
        tail_spec = pl.BlockSpec((tm // seq_rows, CONV_W - 1, tn), lambda i, j: (i, 0, j))
        hist_args = (hist,)
        hist_specs = [pl.BlockSpec((tm // seq_rows, CONV_W - 1, tn), lambda i, j: (i, 0, j))]
    else:
        assert seq_rows % tm == 0
        tiles_per_seq = seq_rows // tm
        tail_shape = jax.ShapeDtypeStruct((n_row, SUBLANES, d_ff), F32)
        tail_spec = pl.BlockSpec((1, SUBLANES, tn), lambda i, j: (i, 0, j))
        hist_args = ()
        hist_specs = []
    out_specs = [pl.BlockSpec((tm, d), lambda i, j: (i, 0)), tail_spec]
    out_shape = [jax.ShapeDtypeStruct((rows, d), F32), tail_shape]
    if export_w:
        out_specs += [pl.BlockSpec((d, tn), lambda i, j: (0, j)), pl.BlockSpec((d, tn), lambda i, j: (0, j)),
                      pl.BlockSpec((tn, d), lambda i, j: (j, 0))]
        out_shape += [jax.ShapeDtypeStruct((d, d_ff), BF16), jax.ShapeDtypeStruct((d, d_ff), BF16),
                      jax.ShapeDtypeStruct((d_ff, d), BF16)]
    outs = pl.pallas_call(
        functools.partial(_ffn_kernel, seq_rows=seq_rows, tiles_per_seq=tiles_per_seq, per_seq_hist=per_seq_hist,
                          export_w=export_w),
        grid=(n_row, n_col),
        in_specs=[
            pl.BlockSpec((tm, d), lambda i, j: (i, 0), pipeline_mode=pl.Buffered(1 if export_w else 2)),
            pl.BlockSpec((1, d), lambda i, j: (0, 0)),
            pl.BlockSpec((d, tn), lambda i, j: (0, j)),
            pl.BlockSpec((d, tn), lambda i, j: (0, g_off + j)),
            pl.BlockSpec((CONV_W, tn), lambda i, j: (0, j)),
            pl.BlockSpec((1, tn), lambda i, j: (0, j)),
            pl.BlockSpec((tn, d), lambda i, j: (j, 0)),
            pl.BlockSpec((1, d), lambda i, j: (0, 0)),
            *hist_specs,
        ],
        out_specs=out_specs,
        out_shape=out_shape,
        scratch_shapes=[
            pltpu.VMEM((tm, d), BF16),
            pltpu.VMEM((n_col, SUBLANES, tn), F32),
        ],
        compiler_params=_params("arbitrary", "arbitrary"),
        name="ffn",
    )(h, g2, w_a, w_g, conv_w, conv_b, w_down, gf, *hist_args)
    return outs[0], outs[1], (tuple(outs[2:]) if export_w else None)


def _layer(x3, pool_hist, conv_hist, attend, pos0, w, *, tm, in_tn, ffn_tn, mid_tile):
    n_seq, seq_len, d = x3.shape
    rows = n_seq * seq_len
    x = x3.reshape(rows, d)
    w_pool = w["pool_scale"].shape[1]
    w_att = w["w_att_up"].shape[0]
    rest, k, v, w_in_b = _in_proj(x, w["g1"], w["w_in"], k_col=w_pool + w_att, kv_width=w_att, tm=tm, tn=in_tn)
    rest3 = rest.reshape(n_seq, seq_len, -1)
    k3, v3 = k.reshape(n_seq, seq_len, w_att), v.reshape(n_seq, seq_len, w_att)
    q_col, ga_col, gb_col = w_pool, w_pool + w_att, w_pool + w_att + d
    if pool_hist is None:
        a_out = _pool_mix(rest3, None, w["pool_maps"], w["pool_scale"], s_blk=1, l_blk=POOL_TILE_ROWS, pos0=pos0,
                          out_dtype=BF16)
    else:
        a_out = _pool_mix(rest3, pool_hist, w["pool_maps"], w["pool_scale"], s_blk=POOL_SEQS_PER_STEP, l_blk=seq_len,
                          pos0=pos0, out_dtype=F32)
    b_out = attend(rest3, k3, v3, q_col)
    mid_tm, mid_tn = mid_tile
    merged = _merge(a_out, b_out, w["w_pool_up"], w["w_att_up"], rest, ga_col=ga_col, gb_col=gb_col,
                    tm=mid_tm, tn=mid_tn)
    h, w_out_b = _matmul_residual(merged, w["w_out"], x, tm=mid_tm, tn=mid_tn)
    y, tail, ffn_w = _ffn(h, w["g2"], w["w_up"], w["conv_w"], w["conv_b"], w["w_down"], w["gf"], conv_hist,
                          seq_rows=seq_len, tm=tm, tn=ffn_tn)
    return y.reshape(n_seq, seq_len, d), rest3, k3, v3, tail, (w_in_b, w_out_b, ffn_w)


def kernel(x_prompt, x_sample, cache_k, cache_v, state_pool, state_conv, page_table, norm1_g, w_in, pool_maps,
           pool_scale, w_pool_up, w_att_up, w_out, norm2_g, w_up, conv_w, conv_b, w_down, final_norm_g):
    depth = w_in.shape[0]
    assert depth == 1, "single-layer trunk"
    bsz, seq, d = x_prompt.shape
    n_dec, dec_seq, _ = x_sample.shape
    _, n_pool, page, n_heads, dh = cache_k.shape
    assert dh == HEAD_DIM
    w_att = n_heads * dh
    w_pool = pool_scale.shape[1]
    d_ff = w_down.shape[1]
    past_len = page_table.shape[1] * page
    slopes = 2.0 ** (-8.0 * jnp.arange(1, n_heads + 1, dtype=F32) / n_heads)

    l = 0
    w = dict(
        g1=norm1_g[l][None, :], w_in=w_in[l], pool_maps=pool_maps[l].astype(BF16),
        pool_scale=pool_scale[l][None, :], w_pool_up=w_pool_up[l].astype(BF16), w_att_up=w_att_up[l].astype(BF16),
        w_out=w_out[l], g2=norm2_g[l][None, :], w_up=w_up[l], conv_w=conv_w[l],
        conv_b=conv_b[l][None, :], w_down=w_down[l], gf=final_norm_g[None, :],
    )
    tm = ROW_TILE

    attend_s = lambda q_src3, k3, v3, qc: _sample_attention(q_src3, k3, v3, cache_k, cache_v, l, page_table, slopes,
                                                            q_col=qc)
    y_s, rest_s, k_s, v_s, tail_s, (w_in_b, w_out_b, (w_ua_b, w_ug_b, w_down_b)) = _layer(
        x_sample, state_pool[l], state_conv[l], attend_s, past_len, w, tm=tm, in_tn=IN_PROJ_COLS_F32,
        ffn_tn=FFN_COLS_F32, mid_tile=(tm, OUT_PROJ_COLS_F32))

    w_p = dict(w, w_in=w_in_b, w_out=w_out_b, w_up=(w_ua_b, w_ug_b), w_down=w_down_b)
    attend_p = lambda q_src3, k3, v3, qc: _prompt_attention(q_src3, k3, v3, slopes, q_col=qc)
    y_p, rest_p, k_p, v_p, tail_p, _ = _layer(x_prompt, None, None, attend_p, 0, w_p, tm=tm, in_tn=IN_PROJ_COLS_BF16,
                                              ffn_tn=FFN_COLS_BF16, mid_tile=(tm // 2, d))

    heads = lambda a3: a3.reshape(a3.shape[0], a3.shape[1], n_heads, dh)[None]
    pool_p = rest_p[:, seq - POOL_HIST:, :w_pool][None]
    pool_s = jnp.concatenate([state_pool[l], rest_s[:, :, :w_pool]], axis=1)[:, -POOL_HIST:][None]
    tiles_per_seq = seq // tm
    conv_p = tail_p.reshape(bsz, tiles_per_seq, SUBLANES, d_ff)[:, -1, SUBLANES - (CONV_W - 1):][None]
    conv_s = tail_s[None]
    return (y_p, y_s, heads(k_p), heads(v_p), heads(k_s), heads(v_s), pool_p, pool_s, conv_p, conv_s)
```

```python
import functools

import jax
import jax.numpy as jnp
from jax import lax
from jax.experimental import pallas as pl
from jax.experimental.pallas import tpu as pltpu

POOL_WINDOWS = (2, 4, 8, 16)
assert all(w & (w - 1) == 0 for w in POOL_WINDOWS)
POOL_HIST = max(POOL_WINDOWS) - 1
HEAD_DIM = 128
MOBA_BLOCK = 256
MOBA_TOPK = 3
CONV_W = 3
RMS_EPS = 1e-6
NEG_INF = -1e30

SUBLANES = 8
HALO_ROWS = 16
ROW_TILE = 1024
POOL_TILE_ROWS = 1024
IN_PROJ_COLS_F32, IN_PROJ_COLS_BF16 = 512, 1024
FFN_COLS_F32, FFN_COLS_BF16 = 256, 512
OUT_PROJ_COLS_F32 = 1024
V7X_VMEM_BYTES = 64 * 1024 * 1024
VMEM_LIMIT_BYTES = V7X_VMEM_BYTES - 6 * 1024 * 1024

BF16 = jnp.bfloat16
F32 = jnp.float32


def _params(*semantics):
    return pltpu.CompilerParams(dimension_semantics=semantics, vmem_limit_bytes=VMEM_LIMIT_BYTES)


def _rmsnorm(x, g):
    return x * lax.rsqrt(jnp.mean(x * x, axis=-1, keepdims=True) + RMS_EPS) * g


def _dot(a, b):
    return jnp.dot(a, b, preferred_element_type=F32)


def _dot_nt(a, b):
    return lax.dot_general(a, b, (((1,), (1,)), ((), ())), preferred_element_type=F32)


def _split_bf16(x):
    hi = x.astype(BF16)
    lo = (x - hi.astype(F32)).astype(BF16)
    return hi, lo


def _dot_nt_3pass(a, b):
    a_hi, a_lo = _split_bf16(a)
    b_hi, b_lo = _split_bf16(b)
    return _dot_nt(a_hi, b_hi) + (_dot_nt(a_hi, b_lo) + _dot_nt(a_lo, b_hi))


def _in_proj_kernel(x_ref, g_ref, w_ref, rest_ref, k_ref, v_ref, *more, k_tiles, v_tiles, export_w):
    xn_ref = more[-1]
    j = pl.program_id(1)

    @pl.when(j == 0)
    def _():
        xn_ref[...] = _rmsnorm(x_ref[...], g_ref[...]).astype(BF16)

    def project_into(o_ref):
        wb = w_ref[...].astype(BF16)
        if export_w:
            more[0][...] = wb
        o_ref[...] = _dot(xn_ref[...], wb)

    is_k = (j >= k_tiles[0]) & (j < k_tiles[1])
    is_v = (j >= v_tiles[0]) & (j < v_tiles[1])
    pl.when(is_k)(lambda: project_into(k_ref))
    pl.when(is_v)(lambda: project_into(v_ref))
    pl.when(jnp.logical_not(is_k | is_v))(lambda: project_into(rest_ref))


def _in_proj(x, g, w, *, k_col, kv_width, tm, tn):
    rows, d = x.shape
    n = w.shape[1]
    export_w = w.dtype != BF16
    assert k_col % tn == 0 and kv_width % tn == 0
    k0, nkv = k_col // tn, kv_width // tn
    v0, after = k0 + nkv, k0 + 2 * nkv
    rest_map = lambda i, j: (i, jnp.where(j < k0, j, jnp.maximum(j - 2 * nkv, k0 - 1)))
    k_map = lambda i, j: (i, jnp.clip(j - k0, 0, nkv - 1))
    v_map = lambda i, j: (i, jnp.clip(j - v0, 0, nkv - 1))
    out_specs = [pl.BlockSpec((tm, tn), rest_map), pl.BlockSpec((tm, tn), k_map), pl.BlockSpec((tm, tn), v_map)]
    out_shape = [jax.ShapeDtypeStruct((rows, n - 2 * kv_width), F32),
                 jax.ShapeDtypeStruct((rows, kv_width), F32), jax.ShapeDtypeStruct((rows, kv_width), F32)]
    if export_w:
        out_specs.append(pl.BlockSpec((d, tn), lambda i, j: (0, j)))
        out_shape.append(jax.ShapeDtypeStruct((d, n), BF16))
    outs = pl.pallas_call(
        functools.partial(_in_proj_kernel, k_tiles=(k0, v0), v_tiles=(v0, after), export_w=export_w),
        grid=(rows // tm, n // tn),
        in_specs=[
            pl.BlockSpec((tm, d), lambda i, j: (i, 0)),
            pl.BlockSpec((1, d), lambda i, j: (0, 0)),
            pl.BlockSpec((d, tn), lambda i, j: (0, j)),
        ],
        out_specs=out_specs,
        out_shape=out_shape,
        scratch_shapes=[pltpu.VMEM((tm, d), BF16)],
        compiler_params=_params("arbitrary", "arbitrary"),
        name="in_proj",
    )(x, g, w)
    return (*outs, None) if not export_w else tuple(outs)


def _pool_kernel(u_ref, hist_ref, maps_ref, scale_ref, o_ref, ext_ref, *, pos0, zero_first_hist):
    _pool_body(u_ref, hist_ref, maps_ref, scale_ref, o_ref, ext_ref, pl.program_id(1), pos0, zero_first_hist)


def _pool_body(u_ref, hist_ref, maps_ref, scale_ref, o_ref, ext_ref, tile, pos0, zero_first_hist):
    s_blk, l_blk, width = u_ref.shape
    group = width // len(POOL_WINDOWS)
    u = u_ref[...]
    ext_ref[:, HALO_ROWS:, :] = u
    hist_rows = hist_ref.shape[1]
    ext_ref[:, HALO_ROWS - hist_rows:HALO_ROWS, :] = hist_ref[...]
    if hist_rows < HALO_ROWS:
        ext_ref[:, 0:HALO_ROWS - hist_rows, :] = jnp.zeros((s_blk, HALO_ROWS - hist_rows, width), F32)
    if zero_first_hist:
        @pl.when(tile == 0)
        def _():
            ext_ref[:, 0:HALO_ROWS, :] = jnp.zeros((s_blk, HALO_ROWS, width), F32)

    pos = pos0 + tile * l_blk + lax.broadcasted_iota(jnp.int32, (s_blk, l_blk, group), 1)
    outs = []
    for gi, w in enumerate(POOL_WINDOWS):
        cols = slice(gi * group, (gi + 1) * group)
        acc = ext_ref[:, :, cols]
        span = 1
        while span < w:
            acc = acc + pltpu.roll(acc, span, axis=1)
            span *= 2
        win = acc[:, HALO_ROWS:, :]
        cnt = jnp.minimum(w, pos + 1).astype(F32)
        d = win / cnt - u[:, :, cols]
        d2 = d.reshape(s_blk * l_blk, group).astype(BF16)
        outs.append(_dot(d2, maps_ref[gi]))
    m = jnp.concatenate(outs, axis=-1) * scale_ref[...]
    o_ref[...] = m.astype(o_ref.dtype)


def _pool_mix(proj3, hist, maps, scale, *, s_blk, l_blk, pos0, out_dtype):
    n_seq, seq_len, _ = proj3.shape
    n_groups, group, _ = maps.shape
    width = n_groups * group
    tiles = seq_len // l_blk
    if hist is None:
        halo_blocks = l_blk // HALO_ROWS
        hist_arr = proj3
        hist_spec = pl.BlockSpec((s_blk, HALO_ROWS, width),
                                 lambda s, t: (s, jnp.maximum(t * halo_blocks - 1, 0), 0))
    else:
        assert tiles == 1
        hist_arr = hist
        hist_spec = pl.BlockSpec((s_blk, hist.shape[1], width), lambda s, t: (s, 0, 0))
    kern = functools.partial(_pool_kernel, pos0=pos0, zero_first_hist=hist is None)
    return pl.pallas_call(
        kern,
        grid=(n_seq // s_blk, tiles),
        in_specs=[
            pl.BlockSpec((s_blk, l_blk, width), lambda s, t: (s, t, 0)),
            hist_spec,
            pl.BlockSpec((n_groups, group, group), lambda s, t: (0, 0, 0)),
            pl.BlockSpec((1, width), lambda s, t: (0, 0)),
        ],
        out_specs=pl.BlockSpec((s_blk * l_blk, width), lambda s, t: (s * tiles + t, 0)),
        out_shape=jax.ShapeDtypeStruct((n_seq * seq_len, width), out_dtype),
        scratch_shapes=[pltpu.VMEM((s_blk, HALO_ROWS + l_blk, width), F32)],
        compiler_params=_params("parallel", "arbitrary"),
        name="pool_mix",
    )(proj3, hist_arr, maps, scale)


def _topk_select(route, n_valid):
    nb, n = route.shape
    blk = lax.broadcasted_iota(jnp.int32, (nb, n), 0)
    rank = jnp.zeros((nb, n), jnp.int32)
    for other in range(nb):
        c = route[other:other + 1, :]
        beats = jnp.where((c > route) | ((c == route) & (other < blk)), 1, 0)
        rank = rank + beats * (other < n_valid).astype(jnp.int32)
    return (blk < n_valid) & (rank < MOBA_TOPK)


def _topk_select_list(scores):
    sel = []
    for j, s_j in enumerate(scores):
        rank = jnp.zeros(s_j.shape, jnp.int32)
        for other, s_o in enumerate(scores):
            if other < j:
                rank = rank + jnp.where(s_o >= s_j, 1, 0)
            elif other > j:
                rank = rank + jnp.where(s_o > s_j, 1, 0)
        sel.append(rank < MOBA_TOPK)
    return sel


KV_TILE = 2 * MOBA_BLOCK
assert KV_TILE == 2 * MOBA_BLOCK
ONES_ROWS = 16
M_FLOOR = -1e20
PROMPT_HEADS_PER_STEP = 4
SCORE_LOOKAHEAD = 4


def _prompt_attn_kernel(slopes_ref, q_ref, k_ref, v_ref, o_ref,
                        kb_ref, vt_ref, kmean_ref, alibi_ref, m_ref, acc_ref):
    hgroup = pl.program_id(1)
    qt = pl.program_id(2)
    n_hb, seq_len, dh = kb_ref.shape
    tq = q_ref.shape[1]
    nb = seq_len // MOBA_BLOCK
    blocks_per_tile = KV_TILE // MOBA_BLOCK
    slopes = [slopes_ref[hgroup * n_hb + hh] for hh in range(n_hb)]
    q_minus_k = (lax.broadcasted_iota(jnp.int32, (KV_TILE, tq), 1)
                 - lax.broadcasted_iota(jnp.int32, (KV_TILE, tq), 0))

    @pl.when(qt == 0)
    def _():
        for hh in range(n_hb):
            cols = slice(hh * dh, (hh + 1) * dh)
            k = k_ref[0, :, cols]
            kb_ref[hh] = k.astype(BF16)
            kmean_ref[hh] = jnp.mean(k.reshape(nb, MOBA_BLOCK, dh), axis=1)
            ones_rows = jnp.where(lax.broadcasted_iota(jnp.int32, (ONES_ROWS, MOBA_BLOCK), 0) == 0, 1.0, 0.0)
            for c in range(nb):
                vt_ref[hh, c, 0:dh, :] = v_ref[0, c * MOBA_BLOCK:(c + 1) * MOBA_BLOCK, cols].T.astype(BF16)
                vt_ref[hh, c, dh:, :] = ones_rows.astype(BF16)
            alibi_ref[hh] = slopes[hh] * q_minus_k.astype(F32)

    blk = lax.broadcasted_iota(jnp.int32, (nb, tq), 0)
    qbs, sel_bits = [], []
    for hh in range(n_hb):
        q = q_ref[0, :, hh * dh:(hh + 1) * dh]
        qbs.append((q * dh ** -0.5).astype(BF16))
        route = _dot_nt_3pass(kmean_ref[hh], q)
        sel = _topk_select(route, qt)
        bits = jnp.sum(jnp.where(sel, jnp.left_shift(1, blk).astype(F32), 0.0), axis=0, keepdims=True)
        sel_bits.append(bits.astype(jnp.int32))

    m_ref[...] = jnp.full(m_ref.shape, M_FLOOR, F32)
    acc_ref[...] = jnp.zeros(acc_ref.shape, F32)

    def block_bias(hh, j):
        bit = jnp.right_shift(sel_bits[hh], j) & 1
        return jnp.broadcast_to(jnp.where(bit == 1, 0.0, NEG_INF), (MOBA_BLOCK, tq))

    def attend(first_blk, n_blk, bias_fn, offset):
        keys = n_blk * MOBA_BLOCK
        start = pl.multiple_of(first_blk * MOBA_BLOCK, MOBA_BLOCK)
        score = lambda hh: _dot_nt(kb_ref[hh, pl.ds(start, keys), :], qbs[hh])
        scores = {hh: score(hh) for hh in range(min(SCORE_LOOKAHEAD, n_hb))}
        for hh in range(n_hb):
            if hh + SCORE_LOOKAHEAD < n_hb:
                scores[hh + SCORE_LOOKAHEAD] = score(hh + SCORE_LOOKAHEAD)
            z = scores.pop(hh) - alibi_ref[hh, 0:keys, :] + bias_fn(hh)
            shift = slopes[hh] * jnp.asarray(offset).astype(F32)
            m_old = m_ref[hh]
            m_new = jnp.maximum(m_old, jnp.max(z, axis=0, keepdims=True) - shift)
            alpha = jnp.exp(m_old - m_new)
            pr = jnp.exp(z - (m_new + shift))
            pv = None
            for b in range(n_blk):
                part = _dot(vt_ref[hh, first_blk + b], pr[b * MOBA_BLOCK:(b + 1) * MOBA_BLOCK].astype(BF16))
                pv = part if pv is None else pv + part
            acc_ref[hh] = alpha * acc_ref[hh] + pv
            m_ref[hh] = m_new

    def past_pair(p, carry):
        first = p * blocks_per_tile
        attend(first, blocks_per_tile,
               lambda hh: jnp.concatenate([block_bias(hh, first + b) for b in range(blocks_per_tile)], axis=0),
               (qt - first) * MOBA_BLOCK)
        return carry

    lax.fori_loop(0, qt // blocks_per_tile, past_pair, 0)

    causal = jnp.where(q_minus_k[0:MOBA_BLOCK, :] >= 0, 0.0, NEG_INF)

    @pl.when(qt % blocks_per_tile == 1)
    def _():
        attend(qt - 1, 2, lambda hh: jnp.concatenate([block_bias(hh, qt - 1), causal], axis=0), MOBA_BLOCK)

    @pl.when(qt % blocks_per_tile == 0)
    def _():
        attend(qt, 1, lambda hh: causal, 0)
    for hh in range(n_hb):
        acc = acc_ref[hh]
        o_ref[:, hh * dh:(hh + 1) * dh] = (acc[0:dh, :] / acc[dh:dh + 1, :]).T.astype(o_ref.dtype)


def _prompt_attention(q_src3, k3, v3, slopes, *, q_col):
    bsz, seq_len, w_att = k3.shape
    n_heads = w_att // HEAD_DIM
    tq = MOBA_BLOCK
    nq = seq_len // tq
    nb = seq_len // MOBA_BLOCK
    hb = PROMPT_HEADS_PER_STEP
    gw = hb * HEAD_DIM
    assert n_heads % hb == 0 and seq_len % KV_TILE == 0 and q_col % gw == 0
    qb0 = q_col // gw
    return pl.pallas_call(
        _prompt_attn_kernel,
        grid=(bsz, n_heads // hb, nq),
        in_specs=[
            pl.BlockSpec(memory_space=pltpu.SMEM),
            pl.BlockSpec((1, tq, gw), lambda b, h, i: (b, i, qb0 + h)),
            pl.BlockSpec((1, seq_len, gw), lambda b, h, i: (b, 0, h)),
            pl.BlockSpec((1, seq_len, gw), lambda b, h, i: (b, 0, h)),
        ],
        out_specs=pl.BlockSpec((tq, gw), lambda b, h, i: (b * nq + i, h)),
        out_shape=jax.ShapeDtypeStruct((bsz * seq_len, n_heads * HEAD_DIM), BF16),
        scratch_shapes=[
            pltpu.VMEM((hb, seq_len, HEAD_DIM), BF16),
            pltpu.VMEM((hb, nb, HEAD_DIM + ONES_ROWS, MOBA_BLOCK), BF16),
            pltpu.VMEM((hb, nb, HEAD_DIM), F32),
            pltpu.VMEM((hb, KV_TILE, tq), F32),
            pltpu.VMEM((hb, 1, tq), F32),
            pltpu.VMEM((hb, HEAD_DIM + ONES_ROWS, tq), F32),
        ],
        compiler_params=_params("parallel", "parallel", "arbitrary"),
        name="prompt_attn",
    )(slopes, q_src3, k3, v3)


def _heads_to_rows(x, n_heads, dh):
    return jnp.concatenate([x[:, h * dh:(h + 1) * dh] for h in range(n_heads)], axis=0)


CACHE_RING = 3


def _sample_attn_kernel(pt_ref, slopes_ref, q_ref, kn_ref, vn_ref, u_ref, hist_ref, maps_ref, scale_ref,
                        ck_hbm, cv_hbm, o_ref, pool_ref, kbuf, vbuf, sem, ext_ref, *, past_len, n_pages, layer):
    seq = pl.program_id(0)
    n_seq = pl.num_programs(0)

    def page_copies(s):
        slot = s % CACHE_RING
        copies = []
        for pg in range(n_pages):
            src = pt_ref[s, pg]
            copies.append(pltpu.make_async_copy(ck_hbm.at[layer, src], kbuf.at[slot, pg], sem.at[slot]))
            copies.append(pltpu.make_async_copy(cv_hbm.at[layer, src], vbuf.at[slot, pg], sem.at[slot]))
        return copies

    def start(s):
        for n, c in enumerate(page_copies(s)):
            c.start(priority=n % 2)

    @pl.when(seq == 0)
    def _():
        for s in range(CACHE_RING - 1):
            start(s)

    @pl.when(seq + CACHE_RING - 1 < n_seq)
    def _():
        start(seq + CACHE_RING - 1)

    _pool_body(u_ref, hist_ref, maps_ref, scale_ref, pool_ref, ext_ref, 0, past_len, False)

    for c in page_copies(seq):
        c.wait()
    slot = seq % CACHE_RING
    k_refs = [kbuf.at[slot, pg] for pg in range(n_pages)]
    v_refs = [vbuf.at[slot, pg] for pg in range(n_pages)]
    t = q_ref.shape[1]
    page, n_heads, dh = k_refs[0].shape
    pages_per_block = MOBA_BLOCK // page
    nb = n_pages // pages_per_block
    rows = n_heads * t
    page_keys = page * n_heads

    q2 = _heads_to_rows(q_ref[0], n_heads, dh)
    q2b = (q2 * dh ** -0.5).astype(BF16)
    row_head = lax.broadcasted_iota(jnp.int32, (rows, 1), 0) // t
    row_query = lax.broadcasted_iota(jnp.int32, (rows, 1), 0) % t
    slope_rows = jnp.zeros((rows, 1), F32)
    for h in range(n_heads):
        slope_rows = jnp.where(row_head == h, slopes_ref[h], slope_rows)
    lane = lax.broadcasted_iota(jnp.int32, (rows, page_keys), 1)
    page_bias = jnp.where(lane % n_heads == row_head,
                          slope_rows * (row_query - lane // n_heads).astype(F32), -NEG_INF)

    def block_scores(b):
        ksum = jnp.zeros((n_heads, dh), F32)
        s_pages = []
        for pg in range(b * pages_per_block, (b + 1) * pages_per_block):
            kp = k_refs[pg][...]
            ksum = ksum + jnp.sum(kp, axis=0)
            s_pg = _dot_nt(q2b, kp.reshape(page_keys, dh).astype(BF16))
            s_pages.append(s_pg - page_bias - slope_rows * float(past_len - pg * page))
        kmean = ksum / MOBA_BLOCK
        kmean_rows = jnp.concatenate([jnp.broadcast_to(kmean[h:h + 1, :], (t, dh)) for h in range(n_heads)], axis=0)
        return jnp.sum(q2 * kmean_rows, axis=-1, keepdims=True), jnp.concatenate(s_pages, axis=-1)

    route, m_blk, l_blk, acc_blk = [], [], [], []
    ahead = block_scores(0)
    for b in range(nb):
        route_b, s_b = ahead
        if b + 1 < nb:
            ahead = block_scores(b + 1)
        pages = range(b * pages_per_block, (b + 1) * pages_per_block)
        vblk = jnp.concatenate([v_refs[pg][...].reshape(page_keys, dh) for pg in pages], axis=0)
        m_b = jnp.max(s_b, axis=-1, keepdims=True)
        p = jnp.exp(s_b - m_b)
        route.append(route_b)
        m_blk.append(m_b)
        l_blk.append(jnp.sum(p, axis=-1, keepdims=True))
        acc_blk.append(_dot(p.astype(BF16), vblk.astype(BF16)))

    sel = _topk_select_list(route)
    kn2 = _heads_to_rows(kn_ref[0], n_heads, dh)
    vn2 = _heads_to_rows(vn_ref[0], n_heads, dh)
    col = lax.broadcasted_iota(jnp.int32, (rows, rows), 1)
    dist = row_query - col % t
    keep = (col // t == row_head) & (dist >= 0)
    s_own = _dot_nt(q2b, kn2.astype(BF16)) - slope_rows * dist.astype(F32)
    s_own = jnp.where(keep, s_own, NEG_INF)
    m_all = jnp.max(s_own, axis=-1, keepdims=True)
    for b in range(nb):
        m_all = jnp.maximum(m_all, jnp.where(sel[b], m_blk[b], NEG_INF))
    p_own = jnp.where(keep, jnp.exp(s_own - m_all), 0.0)
    den = jnp.sum(p_own, axis=-1, keepdims=True)
    num = _dot(p_own.astype(BF16), vn2.astype(BF16))
    for b in range(nb):
        w_b = jnp.where(sel[b], jnp.exp(m_blk[b] - m_all), 0.0)
        den = den + w_b * l_blk[b]
        num = num + w_b * acc_blk[b]
    out = num / den
    o_ref[...] = jnp.concatenate([out[h * t:(h + 1) * t, :] for h in range(n_heads)], axis=-1)


def _sample_attention(q_src3, k3, v3, cache_k, cache_v, layer, page_table, slopes, pool_hist, maps, scale, *, q_col):
    n_seq, t, _ = k3.shape
    n_groups, group, _ = maps.shape
    w_pool = n_groups * group
    _, _, page, n_heads, dh = cache_k.shape
    width = n_heads * dh
    n_pages = page_table.shape[1]
    assert MOBA_BLOCK % page == 0 and (n_pages * page) % MOBA_BLOCK == 0 and t <= MOBA_BLOCK
    assert q_col % width == 0
    assert n_seq >= CACHE_RING - 1
    past_len = n_pages * page
    qb0 = q_col // width
    grid_spec = pltpu.PrefetchScalarGridSpec(
        num_scalar_prefetch=1,
        grid=(n_seq,),
        in_specs=[
            pl.BlockSpec(memory_space=pltpu.SMEM),
            pl.BlockSpec((1, t, width), lambda s, pt: (s, 0, qb0)),
            pl.BlockSpec((1, t, width), lambda s, pt: (s, 0, 0)),
            pl.BlockSpec((1, t, width), lambda s, pt: (s, 0, 0)),
            pl.BlockSpec((1, t, w_pool), lambda s, pt: (s, 0, 0)),
            pl.BlockSpec((1, pool_hist.shape[1], w_pool), lambda s, pt: (s, 0, 0)),
            pl.BlockSpec((n_groups, group, group), lambda s, pt: (0, 0, 0)),
            pl.BlockSpec((1, w_pool), lambda s, pt: (0, 0)),
            pl.BlockSpec(memory_space=pl.ANY),
            pl.BlockSpec(memory_space=pl.ANY),
        ],
        out_specs=[pl.BlockSpec((t, width), lambda s, pt: (s, 0)), pl.BlockSpec((t, w_pool), lambda s, pt: (s, 0))],
        scratch_shapes=[
            pltpu.VMEM((CACHE_RING, n_pages, page, n_heads, dh), F32),
            pltpu.VMEM((CACHE_RING, n_pages, page, n_heads, dh), F32),
            pltpu.SemaphoreType.DMA((CACHE_RING,)),
            pltpu.VMEM((1, HALO_ROWS + t, w_pool), F32),
        ],
    )
    return pl.pallas_call(
        functools.partial(_sample_attn_kernel, past_len=past_len, n_pages=n_pages, layer=layer),
        grid_spec=grid_spec,
        out_shape=[jax.ShapeDtypeStruct((n_seq * t, width), F32), jax.ShapeDtypeStruct((n_seq * t, w_pool), F32)],
        compiler_params=_params("arbitrary"),
        name="sample_attn",
    )(page_table, slopes, q_src3, k3, v3, q_src3, pool_hist, maps, scale, cache_k, cache_v)


def _merge_kernel(a_ref, b_ref, wp_ref, wa_ref, ga_ref, gb_ref, o_ref):
    pa = _dot(a_ref[...].astype(BF16), wp_ref[...])
    pb = _dot(b_ref[...].astype(BF16), wa_ref[...])
    merged = jax.nn.sigmoid(ga_ref[...]) * pa + jax.nn.sigmoid(gb_ref[...]) * pb
    o_ref[...] = merged.astype(o_ref.dtype)


def _merge(a_out, b_out, w_pool_up, w_att_up, proj, *, ga_col, gb_col, tm, tn):
    rows, wa = a_out.shape
    wb = b_out.shape[1]
    n = w_pool_up.shape[1]
    ga0, gb0 = ga_col // tn, gb_col // tn
    return pl.pallas_call(
        _merge_kernel,
        grid=(rows // tm, n // tn),
        in_specs=[
            pl.BlockSpec((tm, wa), lambda i, j: (i, 0)),
            pl.BlockSpec((tm, wb), lambda i, j: (i, 0)),
            pl.BlockSpec((wa, tn), lambda i, j: (0, j)),
            pl.BlockSpec((wb, tn), lambda i, j: (0, j)),
            pl.BlockSpec((tm, tn), lambda i, j: (i, ga0 + j)),
            pl.BlockSpec((tm, tn), lambda i, j: (i, gb0 + j)),
        ],
        out_specs=pl.BlockSpec((tm, tn), lambda i, j: (i, j)),
        out_shape=jax.ShapeDtypeStruct((rows, n), BF16),
        compiler_params=_params("parallel", "arbitrary"),
        name="merge",
    )(a_out, b_out, w_pool_up, w_att_up, proj, proj)


def _matmul_residual_kernel(a_ref, w_ref, r_ref, o_ref, *w_outs):
    wb = w_ref[...].astype(BF16)
    if w_outs:
        w_outs[0][...] = wb
    o_ref[...] = r_ref[...] + _dot(a_ref[...], wb)


def _matmul_residual(a, w, res, *, tm, tn):
    rows, k = a.shape
    n = w.shape[1]
    export_w = w.dtype != BF16
    out_specs = [pl.BlockSpec((tm, tn), lambda i, j: (i, j))]
    out_shape = [jax.ShapeDtypeStruct((rows, n), F32)]
    if export_w:
        out_specs.append(pl.BlockSpec((k, tn), lambda i, j: (0, j)))
        out_shape.append(jax.ShapeDtypeStruct((k, n), BF16))
    outs = pl.pallas_call(
        _matmul_residual_kernel,
        grid=(rows // tm, n // tn),
        in_specs=[
            pl.BlockSpec((tm, k), lambda i, j: (i, 0)),
            pl.BlockSpec((k, tn), lambda i, j: (0, j)),
            pl.BlockSpec((tm, tn), lambda i, j: (i, j)),
        ],
        out_specs=out_specs,
        out_shape=out_shape,
        compiler_params=_params("arbitrary", "arbitrary"),
        name="out_proj",
    )(a, w, res)
    return outs[0], (outs[1] if export_w else None)


FFN_CHUNK_COLS = 256


def _gelu_exact(x):
    return 0.5 * x * (1.0 + lax.erf(x * (2.0 ** -0.5)))


def _ffn_kernel(h_ref, g2_ref, wa_ref, wg_ref, cw_ref, cb_ref, wd_ref, gf_ref, *rest,
                seq_rows, tiles_per_seq, per_seq_hist, export_w):
    hist_ref = rest[0] if per_seq_hist else None
    n_in = 1 if per_seq_hist else 0
    y_ref, tail_ref = rest[n_in:n_in + 2]
    wa_out, wg_out, wd_out = rest[n_in + 2:n_in + 5] if export_w else (None, None, None)
    hn_ref, carry_ref = rest[-2:]
    i = pl.program_id(0)
    j = pl.program_id(1)
    tm = h_ref.shape[0]
    tn = wa_ref.shape[1]
    sub = FFN_CHUNK_COLS
    n_chunks = tn // sub

    @pl.when(j == 0)
    def _():
        h = h_ref[...]
        hn_ref[...] = _rmsnorm(h, g2_ref[...]).astype(BF16)
        y_ref[...] = h

    if not per_seq_hist:
        @pl.when(i % tiles_per_seq == 0)
        def _():
            carry_ref[j] = jnp.zeros((SUBLANES, tn), F32)

    hn = hn_ref[...]
    chunks = [slice(c * sub, (c + 1) * sub) for c in range(n_chunks)]
    wa_parts = [wa_ref[:, cols].astype(BF16) for cols in chunks]
    wg_parts = [wg_ref[:, cols].astype(BF16) for cols in chunks]
    wd_parts = [wd_ref[cols, :].astype(BF16) for cols in chunks]
    if export_w:
        for cols, wa_c, wg_c, wd_c in zip(chunks, wa_parts, wg_parts, wd_parts):
            wa_out[:, cols] = wa_c
            wg_out[:, cols] = wg_c
            wd_out[cols, :] = wd_c
    a_parts = [_dot(hn, wa_c) for wa_c in wa_parts]
    g_parts = [_dot(hn, wg_c) for wg_c in wg_parts]
    row = lax.broadcasted_iota(jnp.int32, (tm, sub), 0)
    down = None
    for cols, a, gv, wd_c in zip(chunks, a_parts, g_parts, wd_parts):
        if per_seq_hist:
            n_seq = tm // seq_rows
            hist = hist_ref[:, :, cols]
            prev = [jnp.broadcast_to(hist[:, r:r + 1, :], (n_seq, seq_rows, sub)).reshape(tm, sub)
                    for r in range(CONV_W - 1)]
            t_in_seq = row % seq_rows
            tail_ref[:, :, cols] = a.reshape(n_seq, seq_rows, sub)[:, seq_rows - (CONV_W - 1):, :]
        else:
            carry = carry_ref[j, :, cols]
            first = SUBLANES - (CONV_W - 1)
            prev = [jnp.broadcast_to(carry[first + r:first + r + 1, :], (tm, sub)) for r in range(CONV_W - 1)]
            t_in_seq = row
            tail = a[tm - SUBLANES:, :]
            carry_ref[j, :, cols] = tail
            tail_ref[0, :, cols] = tail
        ac = cb_ref[:, cols] + cw_ref[CONV_W - 1:CONV_W, cols] * a
        for back in range(1, CONV_W):
            shifted = pltpu.roll(a, back, axis=0)
            for r in range(back):
                shifted = jnp.where(t_in_seq == r, prev[CONV_W - 1 - back + r], shifted)
            ac = ac + cw_ref[CONV_W - 1 - back:CONV_W - back, cols] * shifted
        act = (_gelu_exact(ac) * gv).astype(BF16)
        part = _dot(act, wd_c)
        down = part if down is None else down + part
    y_ref[...] += down

    @pl.when(j == pl.num_programs(1) - 1)
    def _():
        y_ref[...] = _rmsnorm(y_ref[...], gf_ref[...])


def _ffn(h, g2, w_up, conv_w, conv_b, w_down, gf, hist, *, seq_rows, tm, tn):
    rows, d = h.shape
    d_ff = w_down.shape[0]
    n_col = d_ff // tn
    n_row = rows // tm
    per_seq_hist = hist is not None
    export_w = not isinstance(w_up, tuple)
    if export_w:
        w_a, w_g, g_off = w_up, w_up, n_col
        assert w_up.dtype == F32 and w_down.dtype == F32
    else:
        (w_a, w_g), g_off = w_up, 0
    if per_seq_hist:
        assert tm % seq_rows == 0 and seq_rows >= CONV_W - 1
        tiles_per_seq = 1
        tail_shape = jax.ShapeDtypeStruct((rows // seq_rows, CONV_W - 1, d_ff), F32)
        tail_spec = pl.BlockSpec((tm // seq_rows, CONV_W - 1, tn), lambda i, j: (i, 0, j))
        hist_args = (hist,)
        hist_specs = [pl.BlockSpec((tm // seq_rows, CONV_W - 1, tn), lambda i, j: (i, 0, j))]
    else:
        assert seq_rows % tm == 0
        tiles_per_seq = seq_rows // tm
        tail_shape = jax.ShapeDtypeStruct((n_row, SUBLANES, d_ff), F32)
        tail_spec = pl.BlockSpec((1, SUBLANES, tn), lambda i, j: (i, 0, j))
        hist_args = ()
        hist_specs = []
    out_specs = [pl.BlockSpec((tm, d), lambda i, j: (i, 0)), tail_spec]
    out_shape = [jax.ShapeDtypeStruct((rows, d), F32), tail_shape]
    if export_w:
        out_specs += [pl.BlockSpec((d, tn), lambda i, j: (0, j)), pl.BlockSpec((d, tn), lambda i, j: (0, j)),
                      pl.BlockSpec((tn, d), lambda i, j: (j, 0))]
        out_shape += [jax.ShapeDtypeStruct((d, d_ff), BF16), jax.ShapeDtypeStruct((d, d_ff), BF16),
                      jax.ShapeDtypeStruct((d_ff, d), BF16)]
    outs = pl.pallas_call(
        functools.partial(_ffn_kernel, seq_rows=seq_rows, tiles_per_seq=tiles_per_seq, per_seq_hist=per_seq_hist,
                          export_w=export_w),
        grid=(n_row, n_col),
        in_specs=[
            pl.BlockSpec((tm, d), lambda i, j: (i, 0), pipeline_mode=pl.Buffered(1 if export_w else 2)),
            pl.BlockSpec((1, d), lambda i, j: (0, 0)),
            pl.BlockSpec((d, tn), lambda i, j: (0, j)),
            pl.BlockSpec((d, tn), lambda i, j: (0, g_off + j)),
            pl.BlockSpec((CONV_W, tn), lambda i, j: (0, j)),
            pl.BlockSpec((1, tn), lambda i, j: (0, j)),
            pl.BlockSpec((tn, d), lambda i, j: (j, 0)),
            pl.BlockSpec((1, d), lambda i, j: (0, 0)),
            *hist_specs,
        ],
        out_specs=out_specs,
        out_shape=out_shape,
        scratch_shapes=[
            pltpu.VMEM((tm, d), BF16),
            pltpu.VMEM((n_col, SUBLANES, tn), F32),
        ],
        compiler_params=_params("arbitrary", "arbitrary"),
        name="ffn",
    )(h, g2, w_a, w_g, conv_w, conv_b, w_down, gf, *hist_args)
    return outs[0], outs[1], (tuple(outs[2:]) if export_w else None)


def _layer(x3, pool_hist, conv_hist, attend, pos0, w, *, tm, in_tn, ffn_tn, mid_tile):
    n_seq, seq_len, d = x3.shape
    rows = n_seq * seq_len
    x = x3.reshape(rows, d)
    w_pool = w["pool_scale"].shape[1]
    w_att = w["w_att_up"].shape[0]
    rest, k, v, w_in_b = _in_proj(x, w["g1"], w["w_in"], k_col=w_pool + w_att, kv_width=w_att, tm=tm, tn=in_tn)
    rest3 = rest.reshape(n_seq, seq_len, -1)
    k3, v3 = k.reshape(n_seq, seq_len, w_att), v.reshape(n_seq, seq_len, w_att)
    q_col, ga_col, gb_col = w_pool, w_pool + w_att, w_pool + w_att + d
    if pool_hist is None:
        a_out = _pool_mix(rest3, None, w["pool_maps"], w["pool_scale"], s_blk=1, l_blk=POOL_TILE_ROWS, pos0=pos0,
                          out_dtype=BF16)
        b_out = attend(rest3, k3, v3, q_col)
    else:
        b_out, a_out = attend(rest3, k3, v3, q_col, pool_hist)
    mid_tm, mid_tn = mid_tile
    merged = _merge(a_out, b_out, w["w_pool_up"], w["w_att_up"], rest, ga_col=ga_col, gb_col=gb_col,
                    tm=mid_tm, tn=mid_tn)
    h, w_out_b = _matmul_residual(merged, w["w_out"], x, tm=mid_tm, tn=mid_tn)
    y, tail, ffn_w = _ffn(h, w["g2"], w["w_up"], w["conv_w"], w["conv_b"], w["w_down"], w["gf"], conv_hist,
                          seq_rows=seq_len, tm=tm, tn=ffn_tn)
    return y.reshape(n_seq, seq_len, d), rest3, k3, v3, tail, (w_in_b, w_out_b, ffn_w)


def kernel(x_prompt, x_sample, cache_k, cache_v, state_pool, state_conv, page_table, norm1_g, w_in, pool_maps,
           pool_scale, w_pool_up, w_att_up, w_out, norm2_g, w_up, conv_w, conv_b, w_down, final_norm_g):
    depth = w_in.shape[0]
    assert depth == 1, "single-layer trunk"
    bsz, seq, d = x_prompt.shape
    n_dec, dec_seq, _ = x_sample.shape
    _, n_pool, page, n_heads, dh = cache_k.shape
    assert dh == HEAD_DIM
    w_att = n_heads * dh
    w_pool = pool_scale.shape[1]
    d_ff = w_down.shape[1]
    past_len = page_table.shape[1] * page
    slopes = 2.0 ** (-8.0 * jnp.arange(1, n_heads + 1, dtype=F32) / n_heads)

    l = 0
    w = dict(
        g1=norm1_g[l][None, :], w_in=w_in[l], pool_maps=pool_maps[l].astype(BF16),
        pool_scale=pool_scale[l][None, :], w_pool_up=w_pool_up[l].astype(BF16), w_att_up=w_att_up[l].astype(BF16),
        w_out=w_out[l], g2=norm2_g[l][None, :], w_up=w_up[l], conv_w=conv_w[l],
        conv_b=conv_b[l][None, :], w_down=w_down[l], gf=final_norm_g[None, :],
    )
    tm = ROW_TILE

    attend_s = lambda q_src3, k3, v3, qc, hist: _sample_attention(
        q_src3, k3, v3, cache_k, cache_v, l, page_table, slopes, hist, w["pool_maps"], w["pool_scale"], q_col=qc)
    y_s, rest_s, k_s, v_s, tail_s, (w_in_b, w_out_b, (w_ua_b, w_ug_b, w_down_b)) = _layer(
        x_sample, state_pool[l], state_conv[l], attend_s, past_len, w, tm=tm, in_tn=IN_PROJ_COLS_F32,
        ffn_tn=FFN_COLS_F32, mid_tile=(tm, OUT_PROJ_COLS_F32))

    w_p = dict(w, w_in=w_in_b, w_out=w_out_b, w_up=(w_ua_b, w_ug_b), w_down=w_down_b)
    attend_p = lambda q_src3, k3, v3, qc: _prompt_attention(q_src3, k3, v3, slopes, q_col=qc)
    y_p, rest_p, k_p, v_p, tail_p, _ = _layer(x_prompt, None, None, attend_p, 0, w_p, tm=tm, in_tn=IN_PROJ_COLS_BF16,
                                              ffn_tn=FFN_COLS_BF16, mid_tile=(tm // 2, d))

    heads = lambda a3: a3.reshape(a3.shape[0], a3.shape[1], n_heads, dh)[None]
    pool_p = rest_p[:, seq - POOL_HIST:, :w_pool][None]
    pool_s = jnp.concatenate([state_pool[l], rest_s[:, :, :w_pool]], axis=1)[:, -POOL_HIST:][None]
    tiles_per_seq = seq // tm
    conv_p = tail_p.reshape(bsz, tiles_per_seq, SUBLANES, d_ff)[:, -1, SUBLANES - (CONV_W - 1):][None]
    conv_s = tail_s[None]
    return (y_p, y_s, heads(k_p), heads(v_p), heads(k_s), heads(v_s), pool_p, pool_s, conv_p, conv_s)
```

```python
import functools

import jax
import jax.numpy as jnp
from jax import lax
from jax.experimental import pallas as pl
from jax.experimental.pallas import tpu as pltpu

POOL_WINDOWS = (2, 4, 8, 16)
assert all(w & (w - 1) == 0 for w in POOL_WINDOWS)
POOL_HIST = max(POOL_WINDOWS) - 1
HEAD_DIM = 128
MOBA_BLOCK = 256
MOBA_TOPK = 3
CONV_W = 3
RMS_EPS = 1e-6
NEG_INF = -1e30

SUBLANES = 8
HALO_ROWS = 16
ROW_TILE = 1024
POOL_TILE_ROWS = 1024
IN_PROJ_COLS_F32, IN_PROJ_COLS_BF16 = 512, 1024
FFN_COLS_F32, FFN_COLS_BF16 = 256, 512
OUT_PROJ_COLS_F32 = 1024
V7X_VMEM_BYTES = 64 * 1024 * 1024
VMEM_LIMIT_BYTES = V7X_VMEM_BYTES - 6 * 1024 * 1024

BF16 = jnp.bfloat16
F32 = jnp.float32


def _params(*semantics):
    return pltpu.CompilerParams(dimension_semantics=semantics, vmem_limit_bytes=VMEM_LIMIT_BYTES)


def _rmsnorm(x, g):
    return x * lax.rsqrt(jnp.mean(x * x, axis=-1, keepdims=True) + RMS_EPS) * g


def _dot(a, b):
    return jnp.dot(a, b, preferred_element_type=F32)


def _dot_nt(a, b):
    return lax.dot_general(a, b, (((1,), (1,)), ((), ())), preferred_element_type=F32)


def _split_bf16(x):
    hi = x.astype(BF16)
    lo = (x - hi.astype(F32)).astype(BF16)
    return hi, lo


def _dot_nt_3pass(a, b):
    a_hi, a_lo = _split_bf16(a)
    b_hi, b_lo = _split_bf16(b)
    return _dot_nt(a_hi, b_hi) + (_dot_nt(a_hi, b_lo) + _dot_nt(a_lo, b_hi))


def _in_proj_kernel(x_ref, g_ref, w_ref, rest_ref, k_ref, v_ref, *more, k_tiles, v_tiles, export_w):
    xn_ref = more[-1]
    j = pl.program_id(1)

    @pl.when(j == 0)
    def _():
        xn_ref[...] = _rmsnorm(x_ref[...], g_ref[...]).astype(BF16)

    def project_into(o_ref):
        wb = w_ref[...].astype(BF16)
        if export_w:
            more[0][...] = wb
        o_ref[...] = _dot(xn_ref[...], wb)

    is_k = (j >= k_tiles[0]) & (j < k_tiles[1])
    is_v = (j >= v_tiles[0]) & (j < v_tiles[1])
    pl.when(is_k)(lambda: project_into(k_ref))
    pl.when(is_v)(lambda: project_into(v_ref))
    pl.when(jnp.logical_not(is_k | is_v))(lambda: project_into(rest_ref))


def _in_proj(x, g, w, *, k_col, kv_width, tm, tn):
    rows, d = x.shape
    n = w.shape[1]
    export_w = w.dtype != BF16
    assert k_col % tn == 0 and kv_width % tn == 0
    k0, nkv = k_col // tn, kv_width // tn
    v0, after = k0 + nkv, k0 + 2 * nkv
    rest_map = lambda i, j: (i, jnp.where(j < k0, j, jnp.maximum(j - 2 * nkv, k0 - 1)))
    k_map = lambda i, j: (i, jnp.clip(j - k0, 0, nkv - 1))
    v_map = lambda i, j: (i, jnp.clip(j - v0, 0, nkv - 1))
    out_specs = [pl.BlockSpec((tm, tn), rest_map), pl.BlockSpec((tm, tn), k_map), pl.BlockSpec((tm, tn), v_map)]
    out_shape = [jax.ShapeDtypeStruct((rows, n - 2 * kv_width), F32),
                 jax.ShapeDtypeStruct((rows, kv_width), F32), jax.ShapeDtypeStruct((rows, kv_width), F32)]
    if export_w:
        out_specs.append(pl.BlockSpec((d, tn), lambda i, j: (0, j)))
        out_shape.append(jax.ShapeDtypeStruct((d, n), BF16))
    outs = pl.pallas_call(
        functools.partial(_in_proj_kernel, k_tiles=(k0, v0), v_tiles=(v0, after), export_w=export_w),
        grid=(rows // tm, n // tn),
        in_specs=[
            pl.BlockSpec((tm, d), lambda i, j: (i, 0)),
            pl.BlockSpec((1, d), lambda i, j: (0, 0)),
            pl.BlockSpec((d, tn), lambda i, j: (0, j)),
        ],
        out_specs=out_specs,
        out_shape=out_shape,
        scratch_shapes=[pltpu.VMEM((tm, d), BF16)],
        compiler_params=_params("arbitrary", "arbitrary"),
        name="in_proj",
    )(x, g, w)
    return (*outs, None) if not export_w else tuple(outs)


def _pool_kernel(u_ref, hist_ref, maps_ref, scale_ref, o_ref, ext_ref, *, pos0, zero_first_hist):
    _pool_body(u_ref, hist_ref, maps_ref, scale_ref, o_ref, ext_ref, pl.program_id(1), pos0, zero_first_hist)


def _pool_body(u_ref, hist_ref, maps_ref, scale_ref, o_ref, ext_ref, tile, pos0, zero_first_hist):
    s_blk, l_blk, width = u_ref.shape
    group = width // len(POOL_WINDOWS)
    u = u_ref[...]
    ext_ref[:, HALO_ROWS:, :] = u
    hist_rows = hist_ref.shape[1]
    ext_ref[:, HALO_ROWS - hist_rows:HALO_ROWS, :] = hist_ref[...]
    if hist_rows < HALO_ROWS:
        ext_ref[:, 0:HALO_ROWS - hist_rows, :] = jnp.zeros((s_blk, HALO_ROWS - hist_rows, width), F32)
    if zero_first_hist:
        @pl.when(tile == 0)
        def _():
            ext_ref[:, 0:HALO_ROWS, :] = jnp.zeros((s_blk, HALO_ROWS, width), F32)

    pos = pos0 + tile * l_blk + lax.broadcasted_iota(jnp.int32, (s_blk, l_blk, group), 1)
    outs = []
    for gi, w in enumerate(POOL_WINDOWS):
        cols = slice(gi * group, (gi + 1) * group)
        acc = ext_ref[:, :, cols]
        span = 1
        while span < w:
            acc = acc + pltpu.roll(acc, span, axis=1)
            span *= 2
        win = acc[:, HALO_ROWS:, :]
        cnt = jnp.minimum(w, pos + 1).astype(F32)
        d = win / cnt - u[:, :, cols]
        d2 = d.reshape(s_blk * l_blk, group).astype(BF16)
        outs.append(_dot(d2, maps_ref[gi]))
    m = jnp.concatenate(outs, axis=-1) * scale_ref[...]
    o_ref[...] = m.astype(o_ref.dtype)


def _pool_mix(proj3, hist, maps, scale, *, s_blk, l_blk, pos0, out_dtype):
    n_seq, seq_len, _ = proj3.shape
    n_groups, group, _ = maps.shape
    width = n_groups * group
    tiles = seq_len // l_blk
    if hist is None:
        halo_blocks = l_blk // HALO_ROWS
        hist_arr = proj3
        hist_spec = pl.BlockSpec((s_blk, HALO_ROWS, width),
                                 lambda s, t: (s, jnp.maximum(t * halo_blocks - 1, 0), 0))
    else:
        assert tiles == 1
        hist_arr = hist
        hist_spec = pl.BlockSpec((s_blk, hist.shape[1], width), lambda s, t: (s, 0, 0))
    kern = functools.partial(_pool_kernel, pos0=pos0, zero_first_hist=hist is None)
    return pl.pallas_call(
        kern,
        grid=(n_seq // s_blk, tiles),
        in_specs=[
            pl.BlockSpec((s_blk, l_blk, width), lambda s, t: (s, t, 0)),
            hist_spec,
            pl.BlockSpec((n_groups, group, group), lambda s, t: (0, 0, 0)),
            pl.BlockSpec((1, width), lambda s, t: (0, 0)),
        ],
        out_specs=pl.BlockSpec((s_blk * l_blk, width), lambda s, t: (s * tiles + t, 0)),
        out_shape=jax.ShapeDtypeStruct((n_seq * seq_len, width), out_dtype),
        scratch_shapes=[pltpu.VMEM((s_blk, HALO_ROWS + l_blk, width), F32)],
        compiler_params=_params("parallel", "arbitrary"),
        name="pool_mix",
    )(proj3, hist_arr, maps, scale)


def _topk_select(route, n_valid):
    nb, n = route.shape
    blk = lax.broadcasted_iota(jnp.int32, (nb, n), 0)
    rank = jnp.zeros((nb, n), jnp.int32)
    for other in range(nb):
        c = route[other:other + 1, :]
        beats = jnp.where((c > route) | ((c == route) & (other < blk)), 1, 0)
        rank = rank + beats * (other < n_valid).astype(jnp.int32)
    return (blk < n_valid) & (rank < MOBA_TOPK)


def _topk_select_list(scores):
    sel = []
    for j, s_j in enumerate(scores):
        rank = jnp.zeros(s_j.shape, jnp.int32)
        for other, s_o in enumerate(scores):
            if other < j:
                rank = rank + jnp.where(s_o >= s_j, 1, 0)
            elif other > j:
                rank = rank + jnp.where(s_o > s_j, 1, 0)
        sel.append(rank < MOBA_TOPK)
    return sel


KV_TILE = 2 * MOBA_BLOCK
assert KV_TILE == 2 * MOBA_BLOCK
ONES_ROWS = 16
M_FLOOR = -1e20
PROMPT_HEADS_PER_STEP = 4
SCORE_LOOKAHEAD = 4


def _prompt_attn_kernel(slopes_ref, q_ref, k_ref, v_ref, o_ref,
                        kb_ref, vt_ref, kmean_ref, alibi_ref, m_ref, acc_ref):
    hgroup = pl.program_id(1)
    qt = pl.program_id(2)
    n_hb, seq_len, dh = kb_ref.shape
    tq = q_ref.shape[1]
    nb = seq_len // MOBA_BLOCK
    blocks_per_tile = KV_TILE // MOBA_BLOCK
    slopes = [slopes_ref[hgroup * n_hb + hh] for hh in range(n_hb)]
    q_minus_k = (lax.broadcasted_iota(jnp.int32, (KV_TILE, tq), 1)
                 - lax.broadcasted_iota(jnp.int32, (KV_TILE, tq), 0))

    @pl.when(qt == 0)
    def _():
        for hh in range(n_hb):
            cols = slice(hh * dh, (hh + 1) * dh)
            k = k_ref[0, :, cols]
            kb_ref[hh] = k.astype(BF16)
            kmean_ref[hh] = jnp.mean(k.reshape(nb, MOBA_BLOCK, dh), axis=1)
            ones_rows = jnp.where(lax.broadcasted_iota(jnp.int32, (ONES_ROWS, MOBA_BLOCK), 0) == 0, 1.0, 0.0)
            for c in range(nb):
                vt_ref[hh, c, 0:dh, :] = v_ref[0, c * MOBA_BLOCK:(c + 1) * MOBA_BLOCK, cols].T.astype(BF16)
                vt_ref[hh, c, dh:, :] = ones_rows.astype(BF16)
            alibi_ref[hh] = slopes[hh] * q_minus_k.astype(F32)

    blk = lax.broadcasted_iota(jnp.int32, (nb, tq), 0)
    qbs, sel_bits = [], []
    for hh in range(n_hb):
        q = q_ref[0, :, hh * dh:(hh + 1) * dh]
        qbs.append((q * dh ** -0.5).astype(BF16))
        route = _dot_nt_3pass(kmean_ref[hh], q)
        sel = _topk_select(route, qt)
        bits = jnp.sum(jnp.where(sel, jnp.left_shift(1, blk).astype(F32), 0.0), axis=0, keepdims=True)
        sel_bits.append(bits.astype(jnp.int32))

    m_ref[...] = jnp.full(m_ref.shape, M_FLOOR, F32)
    acc_ref[...] = jnp.zeros(acc_ref.shape, F32)

    def block_bias(hh, j):
        bit = jnp.right_shift(sel_bits[hh], j) & 1
        return jnp.broadcast_to(jnp.where(bit == 1, 0.0, NEG_INF), (MOBA_BLOCK, tq))

    def attend(first_blk, n_blk, bias_fn, offset):
        keys = n_blk * MOBA_BLOCK
        start = pl.multiple_of(first_blk * MOBA_BLOCK, MOBA_BLOCK)
        score = lambda hh: _dot_nt(kb_ref[hh, pl.ds(start, keys), :], qbs[hh])
        scores = {hh: score(hh) for hh in range(min(SCORE_LOOKAHEAD, n_hb))}
        for hh in range(n_hb):
            if hh + SCORE_LOOKAHEAD < n_hb:
                scores[hh + SCORE_LOOKAHEAD] = score(hh + SCORE_LOOKAHEAD)
            z = scores.pop(hh) - alibi_ref[hh, 0:keys, :] + bias_fn(hh)
            shift = slopes[hh] * jnp.asarray(offset).astype(F32)
            m_old = m_ref[hh]
            m_new = jnp.maximum(m_old, jnp.max(z, axis=0, keepdims=True) - shift)
            alpha = jnp.exp(m_old - m_new)
            pr = jnp.exp(z - (m_new + shift))
            pv = None
            for b in range(n_blk):
                part = _dot(vt_ref[hh, first_blk + b], pr[b * MOBA_BLOCK:(b + 1) * MOBA_BLOCK].astype(BF16))
                pv = part if pv is None else pv + part
            acc_ref[hh] = alpha * acc_ref[hh] + pv
            m_ref[hh] = m_new

    def past_pair(p, carry):
        first = p * blocks_per_tile
        attend(first, blocks_per_tile,
               lambda hh: jnp.concatenate([block_bias(hh, first + b) for b in range(blocks_per_tile)], axis=0),
               (qt - first) * MOBA_BLOCK)
        return carry

    lax.fori_loop(0, qt // blocks_per_tile, past_pair, 0)

    causal = jnp.where(q_minus_k[0:MOBA_BLOCK, :] >= 0, 0.0, NEG_INF)

    @pl.when(qt % blocks_per_tile == 1)
    def _():
        attend(qt - 1, 2, lambda hh: jnp.concatenate([block_bias(hh, qt - 1), causal], axis=0), MOBA_BLOCK)

    @pl.when(qt % blocks_per_tile == 0)
    def _():
        attend(qt, 1, lambda hh: causal, 0)
    for hh in range(n_hb):
        acc = acc_ref[hh]
        o_ref[:, hh * dh:(hh + 1) * dh] = (acc[0:dh, :] / acc[dh:dh + 1, :]).T.astype(o_ref.dtype)


def _prompt_attention(q_src3, k3, v3, slopes, *, q_col):
    bsz, seq_len, w_att = k3.shape
    n_heads = w_att // HEAD_DIM
    tq = MOBA_BLOCK
    nq = seq_len // tq
    nb = seq_len // MOBA_BLOCK
    hb = PROMPT_HEADS_PER_STEP
    gw = hb * HEAD_DIM
    assert n_heads % hb == 0 and seq_len % KV_TILE == 0 and q_col % gw == 0
    qb0 = q_col // gw
    return pl.pallas_call(
        _prompt_attn_kernel,
        grid=(bsz, n_heads // hb, nq),
        in_specs=[
            pl.BlockSpec(memory_space=pltpu.SMEM),
            pl.BlockSpec((1, tq, gw), lambda b, h, i: (b, i, qb0 + h)),
            pl.BlockSpec((1, seq_len, gw), lambda b, h, i: (b, 0, h)),
            pl.BlockSpec((1, seq_len, gw), lambda b, h, i: (b, 0, h)),
        ],
        out_specs=pl.BlockSpec((tq, gw), lambda b, h, i: (b * nq + i, h)),
        out_shape=jax.ShapeDtypeStruct((bsz * seq_len, n_heads * HEAD_DIM), BF16),
        scratch_shapes=[
            pltpu.VMEM((hb, seq_len, HEAD_DIM), BF16),
            pltpu.VMEM((hb, nb, HEAD_DIM + ONES_ROWS, MOBA_BLOCK), BF16),
            pltpu.VMEM((hb, nb, HEAD_DIM), F32),
            pltpu.VMEM((hb, KV_TILE, tq), F32),
            pltpu.VMEM((hb, 1, tq), F32),
            pltpu.VMEM((hb, HEAD_DIM + ONES_ROWS, tq), F32),
        ],
        compiler_params=_params("parallel", "parallel", "arbitrary"),
        name="prompt_attn",
    )(slopes, q_src3, k3, v3)


def _heads_to_rows(x, n_heads, dh):
    return jnp.concatenate([x[:, h * dh:(h + 1) * dh] for h in range(n_heads)], axis=0)


CACHE_RING = 3


def _sample_attn_kernel(pt_ref, slopes_ref, q_ref, kn_ref, vn_ref, u_ref, hist_ref, maps_ref, scale_ref,
                        ck_hbm, cv_hbm, o_ref, pool_ref, kbuf, vbuf, sem, ext_ref, *, past_len, n_pages, layer):
    seq = pl.program_id(0)
    n_seq = pl.num_programs(0)

    def page_copies(s):
        slot = s % CACHE_RING
        copies = []
        for pg in range(n_pages):
            src = pt_ref[s, pg]
            copies.append(pltpu.make_async_copy(ck_hbm.at[layer, src], kbuf.at[slot, pg], sem.at[slot]))
            copies.append(pltpu.make_async_copy(cv_hbm.at[layer, src], vbuf.at[slot, pg], sem.at[slot]))
        return copies

    def start(s):
        for n, c in enumerate(page_copies(s)):
            c.start(priority=n % 2)

    _pool_body(u_ref, hist_ref, maps_ref, scale_ref, pool_ref, ext_ref, 0, past_len, False)

    @pl.when(seq == 0)
    def _():
        for s in range(CACHE_RING - 1):
            start(s)

    @pl.when(seq + CACHE_RING - 1 < n_seq)
    def _():
        start(seq + CACHE_RING - 1)

    for c in page_copies(seq):
        c.wait()
    slot = seq % CACHE_RING
    k_refs = [kbuf.at[slot, pg] for pg in range(n_pages)]
    v_refs = [vbuf.at[slot, pg] for pg in range(n_pages)]
    t = q_ref.shape[1]
    page, n_heads, dh = k_refs[0].shape
    pages_per_block = MOBA_BLOCK // page
    nb = n_pages // pages_per_block
    rows = n_heads * t
    page_keys = page * n_heads

    q2 = _heads_to_rows(q_ref[0], n_heads, dh)
    q2b = (q2 * dh ** -0.5).astype(BF16)
    row_head = lax.broadcasted_iota(jnp.int32, (rows, 1), 0) // t
    row_query = lax.broadcasted_iota(jnp.int32, (rows, 1), 0) % t
    slope_rows = jnp.zeros((rows, 1), F32)
    for h in range(n_heads):
        slope_rows = jnp.where(row_head == h, slopes_ref[h], slope_rows)
    lane = lax.broadcasted_iota(jnp.int32, (rows, page_keys), 1)
    page_bias = jnp.where(lane % n_heads == row_head,
                          slope_rows * (row_query - lane // n_heads).astype(F32), -NEG_INF)

    def block_scores(b):
        ksum = jnp.zeros((n_heads, dh), F32)
        s_pages = []
        for pg in range(b * pages_per_block, (b + 1) * pages_per_block):
            kp = k_refs[pg][...]
            ksum = ksum + jnp.sum(kp, axis=0)
            s_pg = _dot_nt(q2b, kp.reshape(page_keys, dh).astype(BF16))
            s_pages.append(s_pg - page_bias - slope_rows * float(past_len - pg * page))
        kmean = ksum / MOBA_BLOCK
        kmean_rows = jnp.concatenate([jnp.broadcast_to(kmean[h:h + 1, :], (t, dh)) for h in range(n_heads)], axis=0)
        return jnp.sum(q2 * kmean_rows, axis=-1, keepdims=True), jnp.concatenate(s_pages, axis=-1)

    route, m_blk, l_blk, acc_blk = [], [], [], []
    ahead = block_scores(0)
    for b in range(nb):
        route_b, s_b = ahead
        if b + 1 < nb:
            ahead = block_scores(b + 1)
        pages = range(b * pages_per_block, (b + 1) * pages_per_block)
        vblk = jnp.concatenate([v_refs[pg][...].reshape(page_keys, dh) for pg in pages], axis=0)
        m_b = jnp.max(s_b, axis=-1, keepdims=True)
        p = jnp.exp(s_b - m_b)
        route.append(route_b)
        m_blk.append(m_b)
        l_blk.append(jnp.sum(p, axis=-1, keepdims=True))
        acc_blk.append(_dot(p.astype(BF16), vblk.astype(BF16)))

    sel = _topk_select_list(route)
    kn2 = _heads_to_rows(kn_ref[0], n_heads, dh)
    vn2 = _heads_to_rows(vn_ref[0], n_heads, dh)
    col = lax.broadcasted_iota(jnp.int32, (rows, rows), 1)
    dist = row_query - col % t
    keep = (col // t == row_head) & (dist >= 0)
    s_own = _dot_nt(q2b, kn2.astype(BF16)) - slope_rows * dist.astype(F32)
    s_own = jnp.where(keep, s_own, NEG_INF)
    m_all = jnp.max(s_own, axis=-1, keepdims=True)
    for b in range(nb):
        m_all = jnp.maximum(m_all, jnp.where(sel[b], m_blk[b], NEG_INF))
    p_own = jnp.where(keep, jnp.exp(s_own - m_all), 0.0)
    den = jnp.sum(p_own, axis=-1, keepdims=True)
    num = _dot(p_own.astype(BF16), vn2.astype(BF16))
    for b in range(nb):
        w_b = jnp.where(sel[b], jnp.exp(m_blk[b] - m_all), 0.0)
        den = den + w_b * l_blk[b]
        num = num + w_b * acc_blk[b]
    out = num / den
    o_ref[...] = jnp.concatenate([out[h * t:(h + 1) * t, :] for h in range(n_heads)], axis=-1)


def _sample_attention(q_src3, k3, v3, cache_k, cache_v, layer, page_table, slopes, pool_hist, maps, scale, *, q_col):
    n_seq, t, _ = k3.shape
    n_groups, group, _ = maps.shape
    w_pool = n_groups * group
    _, _, page, n_heads, dh = cache_k.shape
    width = n_heads * dh
    n_pages = page_table.shape[1]
    assert MOBA_BLOCK % page == 0 and (n_pages * page) % MOBA_BLOCK == 0 and t <= MOBA_BLOCK
    assert q_col % width == 0
    assert n_seq >= CACHE_RING - 1
    past_len = n_pages * page
    qb0 = q_col // width
    grid_spec = pltpu.PrefetchScalarGridSpec(
        num_scalar_prefetch=1,
        grid=(n_seq,),
        in_specs=[
            pl.BlockSpec(memory_space=pltpu.SMEM),
            pl.BlockSpec((1, t, width), lambda s, pt: (s, 0, qb0)),
            pl.BlockSpec((1, t, width), lambda s, pt: (s, 0, 0)),
            pl.BlockSpec((1, t, width), lambda s, pt: (s, 0, 0)),
            pl.BlockSpec((1, t, w_pool), lambda s, pt: (s, 0, 0)),
            pl.BlockSpec((1, pool_hist.shape[1], w_pool), lambda s, pt: (s, 0, 0)),
            pl.BlockSpec((n_groups, group, group), lambda s, pt: (0, 0, 0)),
            pl.BlockSpec((1, w_pool), lambda s, pt: (0, 0)),
            pl.BlockSpec(memory_space=pl.ANY),
            pl.BlockSpec(memory_space=pl.ANY),
        ],
        out_specs=[pl.BlockSpec((t, width), lambda s, pt: (s, 0)), pl.BlockSpec((t, w_pool), lambda s, pt: (s, 0))],
        scratch_shapes=[
            pltpu.VMEM((CACHE_RING, n_pages, page, n_heads, dh), F32),
            pltpu.VMEM((CACHE_RING, n_pages, page, n_heads, dh), F32),
            pltpu.SemaphoreType.DMA((CACHE_RING,)),
            pltpu.VMEM((1, HALO_ROWS + t, w_pool), F32),
        ],
    )
    return pl.pallas_call(
        functools.partial(_sample_attn_kernel, past_len=past_len, n_pages=n_pages, layer=layer),
        grid_spec=grid_spec,
        out_shape=[jax.ShapeDtypeStruct((n_seq * t, width), F32), jax.ShapeDtypeStruct((n_seq * t, w_pool), F32)],
        compiler_params=_params("arbitrary"),
        name="sample_attn",
    )(page_table, slopes, q_src3, k3, v3, q_src3, pool_hist, maps, scale, cache_k, cache_v)


def _merge_kernel(a_ref, b_ref, wp_ref, wa_ref, ga_ref, gb_ref, o_ref):
    pa = _dot(a_ref[...].astype(BF16), wp_ref[...])
    pb = _dot(b_ref[...].astype(BF16), wa_ref[...])
    merged = jax.nn.sigmoid(ga_ref[...]) * pa + jax.nn.sigmoid(gb_ref[...]) * pb
    o_ref[...] = merged.astype(o_ref.dtype)


def _merge(a_out, b_out, w_pool_up, w_att_up, proj, *, ga_col, gb_col, tm, tn):
    rows, wa = a_out.shape
    wb = b_out.shape[1]
    n = w_pool_up.shape[1]
    ga0, gb0 = ga_col // tn, gb_col // tn
    return pl.pallas_call(
        _merge_kernel,
        grid=(rows // tm, n // tn),
        in_specs=[
            pl.BlockSpec((tm, wa), lambda i, j: (i, 0)),
            pl.BlockSpec((tm, wb), lambda i, j: (i, 0)),
            pl.BlockSpec((wa, tn), lambda i, j: (0, j)),
            pl.BlockSpec((wb, tn), lambda i, j: (0, j)),
            pl.BlockSpec((tm, tn), lambda i, j: (i, ga0 + j)),
            pl.BlockSpec((tm, tn), lambda i, j: (i, gb0 + j)),
        ],
        out_specs=pl.BlockSpec((tm, tn), lambda i, j: (i, j)),
        out_shape=jax.ShapeDtypeStruct((rows, n), BF16),
        compiler_params=_params("parallel", "arbitrary"),
        name="merge",
    )(a_out, b_out, w_pool_up, w_att_up, proj, proj)


def _matmul_residual_kernel(a_ref, w_ref, r_ref, o_ref, *w_outs):
    wb = w_ref[...].astype(BF16)
    if w_outs:
        w_outs[0][...] = wb
    o_ref[...] = r_ref[...] + _dot(a_ref[...], wb)


def _matmul_residual(a, w, res, *, tm, tn):
    rows, k = a.shape
    n = w.shape[1]
    export_w = w.dtype != BF16
    out_specs = [pl.BlockSpec((tm, tn), lambda i, j: (i, j))]
    out_shape = [jax.ShapeDtypeStruct((rows, n), F32)]
    if export_w:
        out_specs.append(pl.BlockSpec((k, tn), lambda i, j: (0, j)))
        out_shape.append(jax.ShapeDtypeStruct((k, n), BF16))
    outs = pl.pallas_call(
        _matmul_residual_kernel,
        grid=(rows // tm, n // tn),
        in_specs=[
            pl.BlockSpec((tm, k), lambda i, j: (i, 0)),
            pl.BlockSpec((k, tn), lambda i, j: (0, j)),
            pl.BlockSpec((tm, tn), lambda i, j: (i, j)),
        ],
        out_specs=out_specs,
        out_shape=out_shape,
        compiler_params=_params("arbitrary", "arbitrary"),
        name="out_proj",
    )(a, w, res)
    return outs[0], (outs[1] if export_w else None)


FFN_CHUNK_COLS = 256


def _gelu_exact(x):
    return 0.5 * x * (1.0 + lax.erf(x * (2.0 ** -0.5)))


def _ffn_kernel(h_ref, g2_ref, wa_ref, wg_ref, cw_ref, cb_ref, wd_ref, gf_ref, *rest,
                seq_rows, tiles_per_seq, per_seq_hist, export_w):
    hist_ref = rest[0] if per_seq_hist else None
    n_in = 1 if per_seq_hist else 0
    y_ref, tail_ref = rest[n_in:n_in + 2]
    wa_out, wg_out, wd_out = rest[n_in + 2:n_in + 5] if export_w else (None, None, None)
    hn_ref, carry_ref = rest[-2:]
    i = pl.program_id(0)
    j = pl.program_id(1)
    tm = h_ref.shape[0]
    tn = wa_ref.shape[1]
    sub = FFN_CHUNK_COLS
    n_chunks = tn // sub

    @pl.when(j == 0)
    def _():
        h = h_ref[...]
        hn_ref[...] = _rmsnorm(h, g2_ref[...]).astype(BF16)
        y_ref[...] = h

    if not per_seq_hist:
        @pl.when(i % tiles_per_seq == 0)
        def _():
            carry_ref[j] = jnp.zeros((SUBLANES, tn), F32)

    hn = hn_ref[...]
    chunks = [slice(c * sub, (c + 1) * sub) for c in range(n_chunks)]
    wa_parts = [wa_ref[:, cols].astype(BF16) for cols in chunks]
    wg_parts = [wg_ref[:, cols].astype(BF16) for cols in chunks]
    wd_parts = [wd_ref[cols, :].astype(BF16) for cols in chunks]
    if export_w:
        for cols, wa_c, wg_c, wd_c in zip(chunks, wa_parts, wg_parts, wd_parts):
            wa_out[:, cols] = wa_c
            wg_out[:, cols] = wg_c
            wd_out[cols, :] = wd_c
    a_parts = [_dot(hn, wa_c) for wa_c in wa_parts]
    g_parts = [_dot(hn, wg_c) for wg_c in wg_parts]
    row = lax.broadcasted_iota(jnp.int32, (tm, sub), 0)
    down = None
    for cols, a, gv, wd_c in zip(chunks, a_parts, g_parts, wd_parts):
        if per_seq_hist:
            n_seq = tm // seq_rows
            hist = hist_ref[:, :, cols]
            prev = [jnp.broadcast_to(hist[:, r:r + 1, :], (n_seq, seq_rows, sub)).reshape(tm, sub)
                    for r in range(CONV_W - 1)]
            t_in_seq = row % seq_rows
            tail_ref[:, :, cols] = a.reshape(n_seq, seq_rows, sub)[:, seq_rows - (CONV_W - 1):, :]
        else:
            carry = carry_ref[j, :, cols]
            first = SUBLANES - (CONV_W - 1)
            prev = [jnp.broadcast_to(carry[first + r:first + r + 1, :], (tm, sub)) for r in range(CONV_W - 1)]
            t_in_seq = row
            tail = a[tm - SUBLANES:, :]
            carry_ref[j, :, cols] = tail
            tail_ref[0, :, cols] = tail
        ac = cb_ref[:, cols] + cw_ref[CONV_W - 1:CONV_W, cols] * a
        for back in range(1, CONV_W):
            shifted = pltpu.roll(a, back, axis=0)
            for r in range(back):
                shifted = jnp.where(t_in_seq == r, prev[CONV_W - 1 - back + r], shifted)
            ac = ac + cw_ref[CONV_W - 1 - back:CONV_W - back, cols] * shifted
        act = (_gelu_exact(ac) * gv).astype(BF16)
        part = _dot(act, wd_c)
        down = part if down is None else down + part
    y_ref[...] += down

    @pl.when(j == pl.num_programs(1) - 1)
    def _():
        y_ref[...] = _rmsnorm(y_ref[...], gf_ref[...])


def _ffn(h, g2, w_up, conv_w, conv_b, w_down, gf, hist, *, seq_rows, tm, tn):
    rows, d = h.shape
    d_ff = w_down.shape[0]
    n_col = d_ff // tn
    n_row = rows // tm
    per_seq_hist = hist is not None
    export_w = not isinstance(w_up, tuple)
    if export_w:
        w_a, w_g, g_off = w_up, w_up, n_col
        assert w_up.dtype == F32 and w_down.dtype == F32
    else:
        (w_a, w_g), g_off = w_up, 0
    if per_seq_hist:
        assert tm % seq_rows == 0 and seq_rows >= CONV_W - 1
        tiles_per_seq = 1
        tail_shape = jax.ShapeDtypeStruct((rows // seq_rows, CONV_W - 1, d_ff), F32)
        tail_spec = pl.BlockSpec((tm // seq_rows, CONV_W - 1, tn), lambda i, j: (i, 0, j))
        hist_args = (hist,)
        hist_specs = [pl.BlockSpec((tm // seq_rows, CONV_W - 1, tn), lambda i, j: (i, 0, j))]
    else:
        assert seq_rows % tm == 0
        tiles_per_seq = seq_rows // tm
        tail_shape = jax.ShapeDtypeStruct((n_row, SUBLANES, d_ff), F32)
        tail_spec = pl.BlockSpec((1, SUBLANES, tn), lambda i, j: (i, 0, j))
        hist_args = ()
        hist_specs = []
    out_specs = [pl.BlockSpec((tm, d), lambda i, j: (i, 0)), tail_spec]
    out_shape = [jax.ShapeDtypeStruct((rows, d), F32), tail_shape]
    if export_w:
        out_specs += [pl.BlockSpec((d, tn), lambda i, j: (0, j)), pl.BlockSpec((d, tn), lambda i, j: (0, j)),
                      pl.BlockSpec((tn, d), lambda i, j: (j, 0))]
        out_shape += [jax.ShapeDtypeStruct((d, d_ff), BF16), jax.ShapeDtypeStruct((d, d_ff), BF16),
                      jax.ShapeDtypeStruct((d_ff, d), BF16)]
    outs = pl.pallas_call(
        functools.partial(_ffn_kernel, seq_rows=seq_rows, tiles_per_seq=tiles_per_seq, per_seq_hist=per_seq_hist,
                          export_w=export_w),
        grid=(n_row, n_col),
        in_specs=[
            pl.BlockSpec((tm, d), lambda i, j: (i, 0), pipeline_mode=pl.Buffered(1 if export_w else 2)),
            pl.BlockSpec((1, d), lambda i, j: (0, 0)),
            pl.BlockSpec((d, tn), lambda i, j: (0, j)),
            pl.BlockSpec((d, tn), lambda i, j: (0, g_off + j)),
            pl.BlockSpec((CONV_W, tn), lambda i, j: (0, j)),
            pl.BlockSpec((1, tn), lambda i, j: (0, j)),
            pl.BlockSpec((tn, d), lambda i, j: (j, 0)),
            pl.BlockSpec((1, d), lambda i, j: (0, 0)),
            *hist_specs,
        ],
        out_specs=out_specs,
        out_shape=out_shape,
        scratch_shapes=[
            pltpu.VMEM((tm, d), BF16),
            pltpu.VMEM((n_col, SUBLANES, tn), F32),
        ],
        compiler_params=_params("arbitrary", "arbitrary"),
        name="ffn",
    )(h, g2, w_a, w_g, conv_w, conv_b, w_down, gf, *hist_args)
    return outs[0], outs[1], (tuple(outs[2:]) if export_w else None)


def _layer(x3, pool_hist, conv_hist, attend, pos0, w, *, tm, in_tn, ffn_tn, mid_tile):
    n_seq, seq_len, d = x3.shape
    rows = n_seq * seq_len
    x = x3.reshape(rows, d)
    w_pool = w["pool_scale"].shape[1]
    w_att = w["w_att_up"].shape[0]
    rest, k, v, w_in_b = _in_proj(x, w["g1"], w["w_in"], k_col=w_pool + w_att, kv_width=w_att, tm=tm, tn=in_tn)
    rest3 = rest.reshape(n_seq, seq_len, -1)
    k3, v3 = k.reshape(n_seq, seq_len, w_att), v.reshape(n_seq, seq_len, w_att)
    q_col, ga_col, gb_col = w_pool, w_pool + w_att, w_pool + w_att + d
    if pool_hist is None:
        a_out = _pool_mix(rest3, None, w["pool_maps"], w["pool_scale"], s_blk=1, l_blk=POOL_TILE_ROWS, pos0=pos0,
                          out_dtype=BF16)
        b_out = attend(rest3, k3, v3, q_col)
    else:
        b_out, a_out = attend(rest3, k3, v3, q_col, pool_hist)
    mid_tm, mid_tn = mid_tile
    merged = _merge(a_out, b_out, w["w_pool_up"], w["w_att_up"], rest, ga_col=ga_col, gb_col=gb_col,
                    tm=mid_tm, tn=mid_tn)
    h, w_out_b = _matmul_residual(merged, w["w_out"], x, tm=mid_tm, tn=mid_tn)
    y, tail, ffn_w = _ffn(h, w["g2"], w["w_up"], w["conv_w"], w["conv_b"], w["w_down"], w["gf"], conv_hist,
                          seq_rows=seq_len, tm=tm, tn=ffn_tn)
    return y.reshape(n_seq, seq_len, d), rest3, k3, v3, tail, (w_in_b, w_out_b, ffn_w)


def kernel(x_prompt, x_sample, cache_k, cache_v, state_pool, state_conv, page_table, norm1_g, w_in, pool_maps,
           pool_scale, w_pool_up, w_att_up, w_out, norm2_g, w_up, conv_w, conv_b, w_down, final_norm_g):
    depth = w_in.shape[0]
    assert depth == 1, "single-layer trunk"
    bsz, seq, d = x_prompt.shape
    n_dec, dec_seq, _ = x_sample.shape
    _, n_pool, page, n_heads, dh = cache_k.shape
    assert dh == HEAD_DIM
    w_att = n_heads * dh
    w_pool = pool_scale.shape[1]
    d_ff = w_down.shape[1]
    past_len = page_table.shape[1] * page
    slopes = 2.0 ** (-8.0 * jnp.arange(1, n_heads + 1, dtype=F32) / n_heads)

    l = 0
    w = dict(
        g1=norm1_g[l][None, :], w_in=w_in[l], pool_maps=pool_maps[l].astype(BF16),
        pool_scale=pool_scale[l][None, :], w_pool_up=w_pool_up[l].astype(BF16), w_att_up=w_att_up[l].astype(BF16),
        w_out=w_out[l], g2=norm2_g[l][None, :], w_up=w_up[l], conv_w=conv_w[l],
        conv_b=conv_b[l][None, :], w_down=w_down[l], gf=final_norm_g[None, :],
    )
    tm = ROW_TILE

    attend_s = lambda q_src3, k3, v3, qc, hist: _sample_attention(
        q_src3, k3, v3, cache_k, cache_v, l, page_table, slopes, hist, w["pool_maps"], w["pool_scale"], q_col=qc)
    y_s, rest_s, k_s, v_s, tail_s, (w_in_b, w_out_b, (w_ua_b, w_ug_b, w_down_b)) = _layer(
        x_sample, state_pool[l], state_conv[l], attend_s, past_len, w, tm=tm, in_tn=IN_PROJ_COLS_F32,
        ffn_tn=FFN_COLS_F32, mid_tile=(tm, OUT_PROJ_COLS_F32))

    w_p = dict(w, w_in=w_in_b, w_out=w_out_b, w_up=(w_ua_b, w_ug_b), w_down=w_down_b)
    attend_p = lambda q_src3, k3, v3, qc: _prompt_attention(q_src3, k3, v3, slopes, q_col=qc)
    y_p, rest_p, k_p, v_p, tail_p, _ = _layer(x_prompt, None, None, attend_p, 0, w_p, tm=tm, in_tn=IN_PROJ_COLS_BF16,
                                              ffn_tn=FFN_COLS_BF16, mid_tile=(tm // 2, d))

    heads = lambda a3: a3.reshape(a3.shape[0], a3.shape[1], n_heads, dh)[None]
    pool_p = rest_p[:, seq - POOL_HIST:, :w_pool][None]
    pool_s = jnp.concatenate([state_pool[l], rest_s[:, :, :w_pool]], axis=1)[:, -POOL_HIST:][None]
    tiles_per_seq = seq // tm
    conv_p = tail_p.reshape(bsz, tiles_per_seq, SUBLANES, d_ff)[:, -1, SUBLANES - (CONV_W - 1):][None]
    conv_s = tail_s[None]
    return (y_p, y_s, heads(k_p), heads(v_p), heads(k_s), heads(v_s), pool_p, pool_s, conv_p, conv_s)
```
